```python
import jax, jax.numpy as jnp
from jax import lax
import numpy as np

D_MODEL = 1024
BATCH = 8
SEQ = 4096
DEPTH = 2

LRU_WIDTH = D_MODEL // 2
LRU_BLOCKS = 8
LRU_BLOCK = LRU_WIDTH // LRU_BLOCKS
CONV_WIDTH = 4
LRU_C = 8.0
RET_HEADS = 8
RET_DIM = 64
RET_WIDTH = RET_HEADS * RET_DIM
RET_CHUNK = 128
RET_THETA = 10000.0
AB_IN = 2 * LRU_WIDTH + 4 * RET_WIDTH
AB_OUT = LRU_WIDTH + RET_WIDTH

NSA_HEADS = 16
NSA_KV_GROUPS = 4
NSA_HEAD_DIM = 64
NSA_REP = NSA_HEADS // NSA_KV_GROUPS
KV_WIDTH = NSA_KV_GROUPS * NSA_HEAD_DIM
CMP_LEN = 32
CMP_STRIDE = 16
SLC_LEN = 64
SLC_TOPK = 16
WINDOW = 512
NSA_QBLOCK = 32
ROPE_THETA = 500000.0
ROT_DIM = NSA_HEAD_DIM // 4
NSA_IN = NSA_HEADS * NSA_HEAD_DIM + 6 * KV_WIDTH + 3 * NSA_HEADS

FFN_HIDDEN = ((8 * D_MODEL // 3 + 255) // 256) * 256

EPS = 1e-6
NEG = -1e30
BIG = 1e30
N_EVEN = (DEPTH + 1) // 2
N_ODD = DEPTH // 2

kernel_name = 'hybrid_rglru_retention_nsa'


def rmsnorm(x, g):
    xf = x.astype(jnp.float32)
    y = xf * lax.rsqrt(jnp.mean(xf * xf, axis=-1, keepdims=True) + EPS)
    return (y * g.astype(jnp.float32)).astype(x.dtype)


def rotary(x, pos, rot_dim, theta):
    half = rot_dim // 2
    inv = theta ** (-jnp.arange(0, rot_dim, 2, dtype=jnp.float32) / rot_dim)
    ang = jnp.asarray(pos).astype(jnp.float32)[:, None] * inv
    cos = jnp.cos(ang)[:, None, :]
    sin = jnp.sin(ang)[:, None, :]
    x1 = x[..., :half].astype(jnp.float32)
    x2 = x[..., half:rot_dim].astype(jnp.float32)
    rot = jnp.concatenate([x1 * cos - x2 * sin, x2 * cos + x1 * sin], axis=-1).astype(x.dtype)
    return jnp.concatenate([rot, x[..., rot_dim:]], axis=-1)


def swiglu(x, w1, w3, w2):
    return (jax.nn.silu(x @ w1) * (x @ w3)) @ w2


def causal_depthwise_conv(x, w, b):
    S = x.shape[1]
    xp = jnp.pad(x, ((0, 0), (CONV_WIDTH - 1, 0), (0, 0)))
    out = b
    for k in range(CONV_WIDTH):
        out = out + xp[:, k:k + S] * w[k]
    return out


def block_diag_linear(x, w, b):
    B, S, _ = x.shape
    xb = x.reshape(B, S, LRU_BLOCKS, LRU_BLOCK)
    return jnp.einsum('bsni,nij->bsnj', xb, w).reshape(B, S, LRU_WIDTH) + b


def rg_lru(x, ga_w, ga_b, gx_w, gx_b, lam):
    r = jax.nn.sigmoid(block_diag_linear(x, ga_w, ga_b)).astype(jnp.float32)
    i = jax.nn.sigmoid(block_diag_linear(x, gx_w, gx_b))
    log_a = -LRU_C * r * jax.nn.softplus(-lam.astype(jnp.float32))
    a = jnp.exp(log_a)
    mult = jnp.sqrt(-jnp.expm1(2.0 * log_a))
    u = mult * (i * x).astype(jnp.float32)

    def combine(left, right):
        a1, b1 = left
        a2, b2 = right
        return a1 * a2, a2 * b1 + b2

    _, h = lax.associative_scan(combine, (a, u), axis=1)
    return h.astype(x.dtype)


def retention(q, k, v):
    B, S, H, d = q.shape
    C = RET_CHUNK
    N = S // C
    pos = jnp.arange(S)
    q = rotary(q, pos, d, RET_THETA)
    k = rotary(k, pos, d, RET_THETA) * (d ** -0.5)
    log_g = jnp.log1p(-(2.0 ** (-5.0 - jnp.arange(H, dtype=jnp.float32))))
    ci = jnp.arange(C, dtype=jnp.float32)
    diff = ci[:, None] - ci[None, :]
    inner_decay = jnp.where(diff >= 0, jnp.exp(jnp.maximum(diff, 0.0) * log_g[:, None, None]), 0.0)
    q_decay = jnp.exp((ci + 1.0) * log_g[:, None])
    k_decay = jnp.exp((C - 1.0 - ci) * log_g[:, None])
    chunk_decay = jnp.exp(C * log_g)

    def to_chunks(t):
        return t.reshape(B, N, C, H, d).transpose(1, 0, 3, 2, 4)

    def step(state, inp):
        qi, ki, vi = inp
        qf, kf, vf = qi.astype(jnp.float32), ki.astype(jnp.float32), vi.astype(jnp.float32)
        s = jnp.einsum('bhid,bhjd->bhij', qf, kf) * inner_decay
        o = jnp.einsum('bhij,bhjd->bhid', s, vf) + jnp.einsum('bhid,bhde->bhie', qf * q_decay[..., None], state)
        state = state * chunk_decay[:, None, None] + jnp.einsum('bhjd,bhje->bhde', kf * k_decay[..., None], vf)
        return state, o

    state0 = jnp.zeros((B, H, d, d), jnp.float32)
    _, o = lax.scan(step, state0, (to_chunks(q), to_chunks(k), to_chunks(v)))
    o = o.transpose(1, 0, 3, 2, 4).reshape(B, S, H, d)
    mu = jnp.mean(o, axis=-1, keepdims=True)
    var = jnp.mean(jnp.square(o - mu), axis=-1, keepdims=True)
    return ((o - mu) * lax.rsqrt(var + EPS)).astype(v.dtype)


def hybrid_ab(h, w_in, conv_w, conv_b, ga_w, ga_b, gx_w, gx_b, lam, w_out):
    B, S, _ = h.shape
    proj = h @ w_in
    L, R = LRU_WIDTH, RET_WIDTH
    y, xr, q, k, v, g = jnp.split(proj, [L, 2 * L, 2 * L + R, 2 * L + 2 * R, 2 * L + 3 * R], axis=-1)
    xr = causal_depthwise_conv(xr, conv_w, conv_b)
    lru_out = rg_lru(xr, ga_w, ga_b, gx_w, gx_b, lam) * jax.nn.gelu(y)
    shp = (B, S, RET_HEADS, RET_DIM)
    ret = retention(q.reshape(shp), k.reshape(shp), v.reshape(shp)).reshape(B, S, R)
    ret_out = ret * jax.nn.silu(g)
    return jnp.concatenate([lru_out, ret_out], axis=-1) @ w_out


def nsa(h, w_in, pe_k, k_w1, k_w2, pe_v, v_w1, v_w2, w_out):
    B, S, _ = h.shape
    H, G, Rr, dh, QB = NSA_HEADS, NSA_KV_GROUPS, NSA_REP, NSA_HEAD_DIM, NSA_QBLOCK
    HD = H * dh
    proj = h @ w_in
    q = proj[..., :HD].reshape(B, S, H, dh)
    kv = proj[..., HD:HD + 6 * KV_WIDTH].reshape(B, S, 6, G, dh)
    k_cmp, v_cmp, k_slc, v_slc, k_win, v_win = [kv[:, :, j] for j in range(6)]
    gates = jax.nn.sigmoid(proj[..., HD + 6 * KV_WIDTH:].reshape(B, S, H, 3))
    pos = jnp.arange(S)
    q = rotary(q, pos, ROT_DIM, ROPE_THETA) * (dh ** -0.5)
    k_slc = rotary(k_slc, pos, ROT_DIM, ROPE_THETA)
    k_win = rotary(k_win, pos, ROT_DIM, ROPE_THETA)

    n_cmp = (S - CMP_LEN) // CMP_STRIDE + 1
    blk_idx = np.arange(n_cmp)[:, None] * CMP_STRIDE + np.arange(CMP_LEN)[None, :]
    cmp_end_np = blk_idx[:, -1]
    cmp_end = jnp.asarray(cmp_end_np)

    def compress(t, pe, w1, w2):
        blocks = t[:, blk_idx] + pe[:, None, :]
        blocks = blocks.transpose(0, 1, 3, 2, 4).reshape(B, n_cmp, G, CMP_LEN * dh)
        return jax.nn.gelu(blocks @ w1) @ w2

    kc = rotary(compress(k_cmp, pe_k, k_w1, k_w2), cmp_end_np, ROT_DIM, ROPE_THETA)
    vc = compress(v_cmp, pe_v, v_w1, v_w2)

    n_slc = S // SLC_LEN
    c_start = blk_idx[:, 0][:, None]
    s_start = np.arange(n_slc)[None, :] * SLC_LEN
    overlap = np.clip(np.minimum(c_start + CMP_LEN, s_start + SLC_LEN) - np.maximum(c_start, s_start), 0, None) / CMP_LEN
    cmp_to_slc = jnp.asarray(overlap, jnp.float32)
    top_k = min(SLC_TOPK, n_slc)
    ks_blocks = k_slc.reshape(B, n_slc, SLC_LEN, G, dh).transpose(0, 3, 1, 2, 4)
    vs_blocks = v_slc.reshape(B, n_slc, SLC_LEN, G, dh).transpose(0, 3, 1, 2, 4)
    gather = jax.vmap(jax.vmap(lambda blocks, idx: blocks[idx]))
    kw_pad = jnp.pad(k_win, ((0, 0), (WINDOW, 0), (0, 0), (0, 0)))
    vw_pad = jnp.pad(v_win, ((0, 0), (WINDOW, 0), (0, 0), (0, 0)))
    blk = jnp.arange(n_slc)

    def attend_block(qi):
        t0 = qi * QB
        qpos = t0 + jnp.arange(QB)
        qb = lax.dynamic_slice_in_dim(q, t0, QB, axis=1).reshape(B, QB, G, Rr, dh)
        s = jnp.einsum('bqgrd,bngd->bgrqn', qb, kc).astype(jnp.float32)
        vis = cmp_end[None, :] <= qpos[:, None]
        p_cmp = jnp.where(vis, jax.nn.softmax(jnp.where(vis, s, NEG), axis=-1), 0.0)
        o_cmp = jnp.einsum('bgrqn,bngd->bqgrd', p_cmp.astype(vc.dtype), vc)
        imp = jnp.einsum('bgrqn,nj->bgqj', p_cmp, cmp_to_slc)
        cur = qpos[:, None] // SLC_LEN
        forced = (blk[None, :] == 0) | (blk[None, :] == cur) | (blk[None, :] == cur - 1)
        causal_blk = blk[None, :] * SLC_LEN <= qpos[:, None]
        imp = jnp.where(forced, BIG, jnp.where(causal_blk, imp, NEG))
        _, sel = lax.top_k(imp, top_k)
        kg = gather(ks_blocks, sel)
        vg = gather(vs_blocks, sel)
        s = jnp.einsum('bqgrd,bgqkld->bgrqkl', qb, kg).astype(jnp.float32)
        tok = sel[..., None] * SLC_LEN + jnp.arange(SLC_LEN)
        ok = tok <= qpos[:, None, None]
        s = jnp.where(ok[:, :, None], s, NEG).reshape(B, G, Rr, QB, top_k * SLC_LEN)
        p_slc = jax.nn.softmax(s, axis=-1).reshape(B, G, Rr, QB, top_k, SLC_LEN)
        o_slc = jnp.einsum('bgrqkl,bgqkld->bqgrd', p_slc.astype(vg.dtype), vg)
        kw = lax.dynamic_slice_in_dim(kw_pad, t0, WINDOW + QB, axis=1)
        vw = lax.dynamic_slice_in_dim(vw_pad, t0, WINDOW + QB, axis=1)
        kpos = t0 - WINDOW + jnp.arange(WINDOW + QB)
        dist = qpos[:, None] - kpos[None, :]
        band = (dist >= 0) & (dist < WINDOW) & (kpos[None, :] >= 0)
        s = jnp.einsum('bqgrd,bkgd->bgrqk', qb, kw).astype(jnp.float32)
        p_win = jax.nn.softmax(jnp.where(band, s, NEG), axis=-1)
        o_win = jnp.einsum('bgrqk,bkgd->bqgrd', p_win.astype(vw.dtype), vw)
        gb = lax.dynamic_slice_in_dim(gates, t0, QB, axis=1).reshape(B, QB, G, Rr, 3)
        o = gb[..., 0:1] * o_cmp + gb[..., 1:2] * o_slc + gb[..., 2:3] * o_win
        return o.reshape(B, QB, HD)

    out = lax.map(attend_block, jnp.arange(S // QB))
    out = out.transpose(1, 0, 2, 3).reshape(B, S, HD)
    return out @ w_out


def setup_inputs(seed: int = 0) -> dict:
    key = jax.random.key(seed)
    ks = jax.random.split(key, 24)
    f32 = jnp.float32

    def w(k, shape, fan_in):
        return jax.random.normal(k, shape, f32) * (fan_in ** -0.5)

    def gain(k, shape):
        return 1.0 + 0.01 * jax.random.normal(k, shape, f32)

    a_pow_c = jax.random.uniform(ks[9], (N_EVEN, LRU_WIDTH), f32, 0.9, 0.999)
    root = a_pow_c ** (1.0 / LRU_C)
    lru_lambda = jnp.log(root) - jnp.log1p(-root)
    return {
        'x': jax.random.normal(ks[0], (BATCH, SEQ, D_MODEL), f32),
        'attn_norm': gain(ks[1], (DEPTH, D_MODEL)),
        'ab_w_in': w(ks[2], (N_EVEN, D_MODEL, AB_IN), D_MODEL),
        'conv_w': w(ks[3], (N_EVEN, CONV_WIDTH, LRU_WIDTH), CONV_WIDTH),
        'conv_b': 0.01 * jax.random.normal(ks[4], (N_EVEN, LRU_WIDTH), f32),
        'gate_a_w': w(ks[5], (N_EVEN, LRU_BLOCKS, LRU_BLOCK, LRU_BLOCK), LRU_BLOCK),
        'gate_a_b': 0.01 * jax.random.normal(ks[6], (N_EVEN, LRU_WIDTH), f32),
        'gate_x_w': w(ks[7], (N_EVEN, LRU_BLOCKS, LRU_BLOCK, LRU_BLOCK), LRU_BLOCK),
        'gate_x_b': 0.01 * jax.random.normal(ks[8], (N_EVEN, LRU_WIDTH), f32),
        'lru_lambda': lru_lambda,
        'ab_w_out': w(ks[10], (N_EVEN, AB_OUT, D_MODEL), AB_OUT),
        'nsa_w_in': w(ks[11], (N_ODD, D_MODEL, NSA_IN), D_MODEL),
        'cmp_pe_k': 0.02 * jax.random.normal(ks[12], (N_ODD, CMP_LEN, NSA_HEAD_DIM), f32),
        'cmp_k_w1': w(ks[13], (N_ODD, CMP_LEN * NSA_HEAD_DIM, NSA_HEAD_DIM), CMP_LEN * NSA_HEAD_DIM),
        'cmp_k_w2': w(ks[14], (N_ODD, NSA_HEAD_DIM, NSA_HEAD_DIM), NSA_HEAD_DIM),
        'cmp_pe_v': 0.02 * jax.random.normal(ks[15], (N_ODD, CMP_LEN, NSA_HEAD_DIM), f32),
        'cmp_v_w1': w(ks[16], (N_ODD, CMP_LEN * NSA_HEAD_DIM, NSA_HEAD_DIM), CMP_LEN * NSA_HEAD_DIM),
        'cmp_v_w2': w(ks[17], (N_ODD, NSA_HEAD_DIM, NSA_HEAD_DIM), NSA_HEAD_DIM),
        'nsa_w_out': w(ks[18], (N_ODD, NSA_HEADS * NSA_HEAD_DIM, D_MODEL), NSA_HEADS * NSA_HEAD_DIM),
        'ffn_norm': gain(ks[19], (DEPTH, D_MODEL)),
        'ffn_w1': w(ks[20], (DEPTH, D_MODEL, FFN_HIDDEN), D_MODEL),
        'ffn_w3': w(ks[21], (DEPTH, D_MODEL, FFN_HIDDEN), D_MODEL),
        'ffn_w2': w(ks[22], (DEPTH, FFN_HIDDEN, D_MODEL), FFN_HIDDEN),
        'final_norm': gain(ks[23], (D_MODEL,)),
    }


def reference(x, attn_norm, ab_w_in, conv_w, conv_b, gate_a_w, gate_a_b, gate_x_w, gate_x_b,
              lru_lambda, ab_w_out, nsa_w_in, cmp_pe_k, cmp_k_w1, cmp_k_w2, cmp_pe_v, cmp_v_w1,
              cmp_v_w2, nsa_w_out, ffn_norm, ffn_w1, ffn_w3, ffn_w2, final_norm):
    h = x
    for layer in range(DEPTH):
        hn = rmsnorm(h, attn_norm[layer])
        i = layer // 2
        if layer % 2 == 0:
            h = h + hybrid_ab(hn, ab_w_in[i], conv_w[i], conv_b[i], gate_a_w[i], gate_a_b[i],
                              gate_x_w[i], gate_x_b[i], lru_lambda[i], ab_w_out[i])
        else:
            h = h + nsa(hn, nsa_w_in[i], cmp_pe_k[i], cmp_k_w1[i], cmp_k_w2[i],
                        cmp_pe_v[i], cmp_v_w1[i], cmp_v_w2[i], nsa_w_out[i])
        h = h + swiglu(rmsnorm(h, ffn_norm[layer]), ffn_w1[layer], ffn_w3[layer], ffn_w2[layer])
    return rmsnorm(h, final_norm)
```

```python
import functools

import numpy as np
import jax
import jax.numpy as jnp
from jax import lax
from jax.experimental import pallas as pl
from jax.experimental.pallas import tpu as pltpu

F32 = jnp.float32
BF16 = jnp.bfloat16

EPS = 1e-6
NEG = -1e30
BIG = 1e30

LRU_BLOCKS = 8
CONV_WIDTH = 4
LRU_C = 8.0
RET_HEADS = 8
RET_DIM = 64
RET_THETA = 10000.0
NSA_HEADS = 16
NSA_KV_GROUPS = 4
NSA_HEAD_DIM = 64
CMP_LEN = 32
CMP_STRIDE = 16
SLC_LEN = 64
SLC_TOPK = 16
WINDOW = 512
ROPE_THETA = 500000.0
ROT_DIM = NSA_HEAD_DIM // 4

LANES = 128
SUBLANES = 8
VMEM_LIMIT = 56 * 1024 * 1024

RET_CHUNK = 256
TOK_TILE = 512
LRU_TILE = 512
ATT_TQ = 128
ATT_TK = 512
N_CMP_PAD = 256


def _cparams(sem):
    return pltpu.CompilerParams(dimension_semantics=sem, vmem_limit_bytes=VMEM_LIMIT)


def _dot(a, b):
    return jnp.dot(a, b, preferred_element_type=F32)


def _dot_nt(a, b):
    return lax.dot_general(a, b, (((1,), (1,)), ((), ())), preferred_element_type=F32)


def _rms(x, g):
    return x * lax.rsqrt(jnp.mean(x * x, axis=-1, keepdims=True) + EPS) * g


def _tile_lanes(t, reps):
    return t if reps == 1 else jnp.concatenate([t] * reps, axis=1)


def _rotate(x, cos, sgn, half):
    n = x.shape[1]
    lane = lax.broadcasted_iota(jnp.int32, x.shape, 1)
    first = (lane % NSA_HEAD_DIM) < half
    partner = jnp.where(first, pltpu.roll(x, n - half, axis=1), pltpu.roll(x, half, axis=1))
    return x * cos + partner * sgn


def _dup_groups(x):
    lane = lax.broadcasted_iota(jnp.int32, (x.shape[0], LANES), 1)
    lo = lane < NSA_HEAD_DIM
    out = []
    for c in range(x.shape[1] // LANES):
        xc = x[:, c * LANES:(c + 1) * LANES]
        sw = pltpu.roll(xc, NSA_HEAD_DIM, axis=1)
        out.append(jnp.where(lo, xc, sw))
        out.append(jnp.where(lo, sw, xc))
    return jnp.concatenate(out, axis=1)


def _rot_tables(pos, rot_dim, theta, scale):
    half = rot_dim // 2
    inv = theta ** (-np.arange(0, rot_dim, 2, dtype=np.float64) / rot_dim)
    ang = np.asarray(pos, np.float64)[:, None] * inv
    cos = np.ones((len(pos), NSA_HEAD_DIM))
    sgn = np.zeros((len(pos), NSA_HEAD_DIM))
    cos[:, :half] = np.cos(ang)
    cos[:, half:rot_dim] = np.cos(ang)
    sgn[:, :half] = -np.sin(ang)
    sgn[:, half:rot_dim] = np.sin(ang)
    cos = np.tile(cos * scale, (1, 2))
    sgn = np.tile(sgn * scale, (1, 2))
    return jnp.asarray(cos, F32), jnp.asarray(sgn, F32)


def _ret_decay_tables(chunk):
    log_g = np.log1p(-(2.0 ** (-5.0 - np.arange(RET_HEADS, dtype=np.float64))))
    ci = np.arange(chunk, dtype=np.float64)
    diff = ci[:, None] - ci[None, :]
    inner = np.where(diff >= 0, np.exp(np.maximum(diff, 0.0) * log_g[:, None, None]), 0.0)
    lane_head = np.repeat(np.arange(RET_HEADS), RET_DIM)
    qdec = np.exp((ci[:, None] + 1.0) * log_g[lane_head][None, :])
    kdec = np.exp((chunk - 1.0 - ci[:, None]) * log_g[lane_head][None, :])
    cdec = np.exp(chunk * log_g[lane_head])[None, :]
    return (jnp.asarray(inner, F32), jnp.asarray(qdec, F32), jnp.asarray(kdec, F32),
            jnp.asarray(cdec, F32))


def _proj_ab_kernel(x_ref, g_ref, w_ref, o_ref):
    xn = _rms(x_ref[...], g_ref[...]).astype(BF16)
    o_ref[...] = _dot(xn, w_ref[...])


def _proj_ab(x2, g, w):
    t, d = x2.shape
    n = w.shape[1]
    return pl.pallas_call(
        _proj_ab_kernel,
        grid=(t // TOK_TILE,),
        in_specs=[pl.BlockSpec((TOK_TILE, d), lambda i: (i, 0)),
                  pl.BlockSpec((1, d), lambda i: (0, 0)),
                  pl.BlockSpec((d, n), lambda i: (0, 0), pipeline_mode=pl.Buffered(1))],
        out_specs=pl.BlockSpec((TOK_TILE, n), lambda i: (i, 0)),
        out_shape=jax.ShapeDtypeStruct((t, n), F32),
        compiler_params=_cparams(("parallel",)),
        name="proj_ab",
    )(x2, g, w)


def _lru_kernel(y_ref, x_ref, cw_ref, cb_ref, gw_ref, gb_ref, lam_ref, o_ref,
                xp_ref, a_ref, u_ref, h_ref, carry_ref):
    ts, w = x_ref.shape
    hist = SUBLANES

    @pl.when(pl.program_id(1) == 0)
    def _():
        xp_ref[0:hist, :] = jnp.zeros((hist, w), F32)
        carry_ref[...] = jnp.zeros_like(carry_ref)

    xp_ref[hist:hist + ts, :] = x_ref[...]
    xc = cb_ref[...]
    for k in range(CONV_WIDTH):
        off = hist - (CONV_WIDTH - 1) + k
        xc = xc + xp_ref[off:off + ts, :] * cw_ref[k:k + 1, :]
    xp_ref[0:hist, :] = x_ref[ts - hist:ts, :]

    gates = _dot(xc.astype(BF16), gw_ref[...]) + gb_ref[...]
    r = jax.nn.sigmoid(gates[:, :w])
    i = jax.nn.sigmoid(gates[:, w:])
    log_a = -LRU_C * r * jax.nn.softplus(-lam_ref[...])
    a = jnp.exp(log_a)
    a_ref[...] = a
    u_ref[...] = jnp.sqrt(jnp.tanh(-log_a) * (a * a + 1.0)) * (i * xc)

    row = lax.broadcasted_iota(jnp.int32, (SUBLANES, w), 0)

    def body(gi, carry):
        r0 = pl.multiple_of(gi * SUBLANES, SUBLANES)
        a = a_ref[pl.ds(r0, SUBLANES), :]
        u = u_ref[pl.ds(r0, SUBLANES), :]
        for d in (1, 2, 4):
            a_s = pltpu.roll(a, d, axis=0)
            u_s = pltpu.roll(u, d, axis=0)
            m = row >= d
            u = jnp.where(m, a * u_s + u, u)
            a = jnp.where(m, a * a_s, a)
        h = a * carry + u
        h_ref[pl.ds(r0, SUBLANES), :] = h
        return h[SUBLANES - 1:SUBLANES, :]

    carry_ref[...] = lax.fori_loop(0, ts // SUBLANES, body, carry_ref[...])
    o_ref[...] = (h_ref[...] * jax.nn.gelu(y_ref[...])).astype(o_ref.dtype)


def _lru(proj, batch, seq, conv_w, conv_b, gate_w, gate_b, lam):
    w = conv_w.shape[1]
    ns = seq // LRU_TILE
    return pl.pallas_call(
        _lru_kernel,
        grid=(batch, ns),
        in_specs=[pl.BlockSpec((LRU_TILE, w), lambda b, s: (b * ns + s, 0)),
                  pl.BlockSpec((LRU_TILE, w), lambda b, s: (b * ns + s, 1)),
                  pl.BlockSpec((CONV_WIDTH, w), lambda b, s: (0, 0)),
                  pl.BlockSpec((1, w), lambda b, s: (0, 0)),
                  pl.BlockSpec((w, 2 * w), lambda b, s: (0, 0)),
                  pl.BlockSpec((1, 2 * w), lambda b, s: (0, 0)),
                  pl.BlockSpec((1, w), lambda b, s: (0, 0))],
        out_specs=pl.BlockSpec((LRU_TILE, w), lambda b, s: (b * ns + s, 0)),
        out_shape=jax.ShapeDtypeStruct((batch * seq, w), BF16),
        scratch_shapes=[pltpu.VMEM((LRU_TILE + SUBLANES, w), F32),
                        pltpu.VMEM((LRU_TILE, w), F32),
                        pltpu.VMEM((LRU_TILE, w), F32),
                        pltpu.VMEM((LRU_TILE, w), F32),
                        pltpu.VMEM((1, w), F32)],
        compiler_params=_cparams(("parallel", "arbitrary")),
        name="rg_lru",
    )(proj, proj, conv_w, conv_b, gate_w, gate_b, lam)


def _split_bf16(x):
    hi = x.astype(BF16)
    lo = (x - hi.astype(F32)).astype(BF16)
    return hi, lo


def _ret_kernel(q_ref, k_ref, v_ref, g_ref, cq_ref, sq_ref, ck_ref, sk_ref,
                inner_ref, qdec_ref, kdec_ref, cdec_ref, avg_ref, o_ref, st_ref):
    c, w = q_ref.shape
    npair = w // LANES
    b = pl.program_id(1)

    @pl.when(pl.program_id(0) == 0)
    def _():
        st_ref[b] = jnp.zeros(st_ref.shape[1:], F32)

    half = RET_DIM // 2
    q = _rotate(q_ref[...], _tile_lanes(cq_ref[...], npair), _tile_lanes(sq_ref[...], npair), half)
    k = _rotate(k_ref[...], _tile_lanes(ck_ref[...], npair), _tile_lanes(sk_ref[...], npair), half)
    v = v_ref[...]
    qd = (q * qdec_ref[...]).astype(BF16)
    kd = k * kdec_ref[...]
    q = q.astype(BF16)
    kb = k.astype(BF16)
    vb = v.astype(BF16)

    lane = lax.broadcasted_iota(jnp.int32, (c, LANES), 1)
    lo = lane < RET_DIM
    rr = lax.broadcasted_iota(jnp.int32, (LANES, LANES), 0) < RET_DIM
    cc = lax.broadcasted_iota(jnp.int32, (LANES, LANES), 1) < RET_DIM
    same_head = rr == cc
    zero = jnp.zeros((c, LANES), BF16)

    outs = []
    for p in range(npair):
        sl = slice(p * LANES, (p + 1) * LANES)
        qp, kp, vp = q[:, sl], kb[:, sl], vb[:, sl]
        s_lo = _dot_nt(jnp.where(lo, qp, zero), kp) * inner_ref[2 * p]
        s_hi = _dot_nt(jnp.where(lo, zero, qp), kp) * inner_ref[2 * p + 1]
        o = jnp.where(lo, _dot(s_lo.astype(BF16), vp), _dot(s_hi.astype(BF16), vp))
        st = st_ref[b, p]
        o = o + _dot(qd[:, sl], st.astype(BF16))
        kv = _dot(jnp.transpose(kd[:, sl]).astype(BF16), vp)
        st_ref[b, p] = st * cdec_ref[:, sl] + jnp.where(same_head, kv, 0.0)
        outs.append(o)

    o_all = jnp.concatenate(outs, axis=0)
    avg = avg_ref[...]
    o_hi, o_lo = _split_bf16(o_all)
    mu = _dot(o_hi, avg) + _dot(o_lo, avg)
    dlt = o_all - mu
    d_hi, d_lo = _split_bf16(dlt * dlt)
    var = _dot(d_hi, avg) + _dot(d_lo, avg)
    y = dlt * lax.rsqrt(var + EPS)
    y = jnp.concatenate([y[p * c:(p + 1) * c, :] for p in range(npair)], axis=1)
    o_ref[...] = (y * jax.nn.silu(g_ref[...])).astype(o_ref.dtype)


def _retention(proj, batch, seq):
    w = RET_HEADS * RET_DIM
    c = RET_CHUNK
    nc = seq // c
    npair = w // LANES
    pos = np.arange(seq)
    cq, sq = _rot_tables(pos, RET_DIM, RET_THETA, 1.0)
    ck, sk = _rot_tables(pos, RET_DIM, RET_THETA, RET_DIM ** -0.5)
    inner, qdec, kdec, cdec = _ret_decay_tables(c)
    blk = np.arange(LANES) // RET_DIM
    avg = jnp.asarray((blk[:, None] == blk[None, :]) / RET_DIM, BF16)
    col0 = 2
    tok = lambda j: pl.BlockSpec((c, w), lambda ci, b, j=j: (b * nc + ci, j))
    tab = pl.BlockSpec((c, LANES), lambda ci, b: (ci, 0))
    const = lambda shape: pl.BlockSpec(shape, lambda ci, b: (0,) * len(shape))
    return pl.pallas_call(
        _ret_kernel,
        grid=(nc, batch),
        in_specs=[tok(col0), tok(col0 + 1), tok(col0 + 2), tok(col0 + 3),
                  tab, tab, tab, tab,
                  const((RET_HEADS, c, c)), const((c, w)), const((c, w)), const((1, w)),
                  const((LANES, LANES))],
        out_specs=pl.BlockSpec((c, w), lambda ci, b: (b * nc + ci, 0)),
        out_shape=jax.ShapeDtypeStruct((batch * seq, w), BF16),
        scratch_shapes=[pltpu.VMEM((batch, npair, LANES, LANES), F32)],
        compiler_params=_cparams(("arbitrary", "arbitrary")),
        name="retention",
    )(proj, proj, proj, proj, cq, sq, ck, sk, inner, qdec, kdec, cdec, avg)


def _mix_ffn_kernel(*refs, n_mix, final):
    h_ref = refs[0]
    mix_refs = refs[1:1 + n_mix]
    wo_ref, g_ref, w1_ref, w3_ref, w2_ref = refs[1 + n_mix:6 + n_mix]
    rest = refs[6 + n_mix:]
    fg_ref = rest[0] if final else None
    o_ref = rest[-1]

    mixed = mix_refs[0][...] if n_mix == 1 else jnp.concatenate([r[...] for r in mix_refs], axis=1)
    h1 = h_ref[...] + _dot(mixed, wo_ref[...])
    xn = _rms(h1, g_ref[...]).astype(BF16)
    act = (jax.nn.silu(_dot(xn, w1_ref[...])) * _dot(xn, w3_ref[...])).astype(BF16)
    h2 = h1 + _dot(act, w2_ref[...])
    if final:
        h2 = _rms(h2, fg_ref[...])
    o_ref[...] = h2


def _mix_ffn(h2d, mixes, wo, g, w1, w3, w2, final_g=None):
    t, d = h2d.shape
    hid = w1.shape[1]
    final = final_g is not None
    row = lambda width: pl.BlockSpec((TOK_TILE, width), lambda i: (i, 0))
    const = lambda shape: pl.BlockSpec(shape, lambda i: (0, 0), pipeline_mode=pl.Buffered(1))
    in_specs = [row(d)] + [row(m.shape[1]) for m in mixes] + [
        const(wo.shape), const((1, d)), const((d, hid)), const((d, hid)), const((hid, d))]
    args = [h2d, *mixes, wo, g, w1, w3, w2]
    if final:
        in_specs.append(const((1, d)))
        args.append(final_g)
    return pl.pallas_call(
        functools.partial(_mix_ffn_kernel, n_mix=len(mixes), final=final),
        grid=(t // TOK_TILE,),
        in_specs=in_specs,
        out_specs=row(d),
        out_shape=jax.ShapeDtypeStruct((t, d), F32),
        compiler_params=_cparams(("parallel",)),
        name="mix_ffn_final" if final else "mix_ffn",
    )(*args)


def _proj_nsa_kernel(x_ref, g_ref, w_ref, cq_ref, sq_ref, ck_ref, sk_ref,
                     q_ref, cmp_ref, ks_ref, vs_ref, kw_ref, vw_ref, gt_ref):
    hd = q_ref.shape[1]
    kvw = cmp_ref.shape[1] // 2
    xn = _rms(x_ref[...], g_ref[...]).astype(BF16)
    proj = _dot(xn, w_ref[...])
    half = ROT_DIM // 2
    q = proj[:, :hd]
    q_ref[...] = _rotate(q, _tile_lanes(cq_ref[...], hd // LANES),
                         _tile_lanes(sq_ref[...], hd // LANES), half).astype(q_ref.dtype)
    kv = proj[:, hd:hd + 6 * kvw]
    cmp_ref[...] = kv[:, :2 * kvw]
    ck = _tile_lanes(ck_ref[...], kvw // LANES)
    sk = _tile_lanes(sk_ref[...], kvw // LANES)
    ks_ref[...] = _dup_groups(_rotate(kv[:, 2 * kvw:3 * kvw], ck, sk, half)).astype(ks_ref.dtype)
    vs_ref[...] = _dup_groups(kv[:, 3 * kvw:4 * kvw]).astype(vs_ref.dtype)
    kw_ref[...] = _dup_groups(_rotate(kv[:, 4 * kvw:5 * kvw], ck, sk, half)).astype(kw_ref.dtype)
    vw_ref[...] = _dup_groups(kv[:, 5 * kvw:6 * kvw]).astype(vw_ref.dtype)
    gt_ref[...] = jax.nn.sigmoid(proj[:, hd + 6 * kvw:])


def _proj_nsa(x2, g, w, seq):
    t, d = x2.shape
    hd = NSA_HEADS * NSA_HEAD_DIM
    kvw = NSA_KV_GROUPS * NSA_HEAD_DIM
    gw = NSA_KV_GROUPS * LANES
    n = w.shape[1]
    ns = seq // TOK_TILE
    pos = np.arange(seq)
    cq, sq = _rot_tables(pos, ROT_DIM, ROPE_THETA, NSA_HEAD_DIM ** -0.5)
    ck, sk = _rot_tables(pos, ROT_DIM, ROPE_THETA, 1.0)
    row = lambda width: pl.BlockSpec((TOK_TILE, width), lambda i: (i, 0))
    tab = pl.BlockSpec((TOK_TILE, LANES), lambda i: (i % ns, 0))
    outs = [(hd, BF16), (2 * kvw, F32), (gw, BF16), (gw, BF16), (gw, BF16), (gw, BF16), (gw, F32)]
    return pl.pallas_call(
        _proj_nsa_kernel,
        grid=(t // TOK_TILE,),
        in_specs=[row(d), pl.BlockSpec((1, d), lambda i: (0, 0)),
                  pl.BlockSpec((d, n), lambda i: (0, 0), pipeline_mode=pl.Buffered(1)),
                  tab, tab, tab, tab],
        out_specs=[row(wd) for wd, _ in outs],
        out_shape=[jax.ShapeDtypeStruct((t, wd), dt) for wd, dt in outs],
        compiler_params=_cparams(("parallel",)),
        name="proj_nsa",
    )(x2, g, w, cq, sq, ck, sk)


def _compress_kernel(x_ref, pe_ref, w1a_ref, w1b_ref, w2_ref, cos_ref, sgn_ref, o_ref):
    x = x_ref[0, 0, 0]
    pe = pe_ref[0]
    u = _dot((x + pe[0:1, :]).astype(BF16), w1a_ref[0])
    v = _dot((x + pe[1:2, :]).astype(BF16), w1b_ref[0])
    nrow = x.shape[0]
    pre = u + pltpu.roll(v, nrow - 1, axis=0)
    out = _dot(jax.nn.gelu(pre).astype(BF16), w2_ref[0])
    o_ref[0, 0, 0] = _rotate(out, cos_ref[0], sgn_ref[0], ROT_DIM // 2).astype(o_ref.dtype)


def _compress(cmp_in, pe, w1a, w1b, w2, cos, sgn):
    b, _, g, n, feat = cmp_in.shape
    per_kv = lambda shape: pl.BlockSpec((1,) + shape, lambda bi, j, gi: (j,) + (0,) * len(shape))
    return pl.pallas_call(
        _compress_kernel,
        grid=(b, 2, g),
        in_specs=[pl.BlockSpec((1, 1, 1, n, feat), lambda bi, j, gi: (bi, j, gi, 0, 0)),
                  per_kv((2, feat)), per_kv(w1a.shape[1:]), per_kv(w1b.shape[1:]),
                  per_kv(w2.shape[1:]), per_kv((n, LANES)), per_kv((n, LANES))],
        out_specs=pl.BlockSpec((1, 1, 1, n, LANES), lambda bi, j, gi: (bi, j, gi, 0, 0)),
        out_shape=jax.ShapeDtypeStruct((b, 2, g, n, LANES), BF16),
        compiler_params=_cparams(("parallel", "arbitrary", "arbitrary")),
        name="compress",
    )(cmp_in, pe, w1a, w1b, w2, cos, sgn)


def _softmax_rows(s, mask):
    sm = jnp.where(mask, s, NEG)
    m = jnp.max(sm, axis=-1, keepdims=True)
    e = jnp.where(mask, jnp.exp(sm - m), 0.0)
    l = jnp.sum(e, axis=-1, keepdims=True)
    return e / jnp.where(l > 0.0, l, 1.0)


def _topk_mask_t(imp_t, t0):
    nb, tq = imp_t.shape
    j = lax.broadcasted_iota(jnp.int32, (nb, tq), 0)
    cur = (t0 + lax.broadcasted_iota(jnp.int32, (nb, tq), 1)) // SLC_LEN
    forced = (j == 0) | (j == cur) | (j == cur - 1)
    causal = j <= cur
    val = jnp.where(forced, BIG, jnp.where(causal, imp_t, NEG))
    nchunk = nb // SUBLANES
    chunks = [val[c * SUBLANES:(c + 1) * SUBLANES, :] for c in range(nchunk)]
    rows = lax.broadcasted_iota(jnp.int32, (SUBLANES, tq), 0)
    cnt = [jnp.zeros((SUBLANES, tq), F32) for _ in range(nchunk)]
    for i in range(nb):
        vi = val[i:i + 1, :]
        for c in range(nchunk):
            ge = jnp.where(vi >= chunks[c], 1.0, 0.0)
            gt = jnp.where(vi > chunks[c], 1.0, 0.0)
            if i < c * SUBLANES:
                ahead = ge
            elif i >= (c + 1) * SUBLANES:
                ahead = gt
            else:
                ahead = jnp.where(rows + c * SUBLANES > i, ge, gt)
            cnt[c] = cnt[c] + ahead
    rank = jnp.concatenate(cnt, axis=0)
    return jnp.where((rank < float(SLC_TOPK)) & causal, 1.0, 0.0)


def _nsa_attn_kernel(q_ref, gt_ref, cmp_ref, ks_ref, vs_ref, kw_ref, vw_ref, c2s_ref, e_ref, o_ref):
    tq = q_ref.shape[0]
    tk = e_ref.shape[2]
    ncmp = cmp_ref.shape[3]
    nrep = NSA_HEADS // NSA_KV_GROUPS
    t0 = pl.program_id(2) * tq

    lane = lax.broadcasted_iota(jnp.int32, (tq, LANES), 1)
    lo = lane < NSA_HEAD_DIM
    q = q_ref[...]
    zero = jnp.zeros((tq, LANES), q.dtype)
    qs = []
    for p in range(nrep // 2):
        qp = q[:, p * LANES:(p + 1) * LANES]
        qs += [jnp.where(lo, qp, zero), jnp.where(lo, zero, qp)]
    qs = jnp.concatenate(qs, axis=0)

    kc = cmp_ref[0, 0, 0]
    vc = cmp_ref[0, 1, 0]
    s = _dot_nt(qs, kc).reshape(nrep, tq, ncmp)
    nidx = lax.broadcasted_iota(jnp.int32, (tq, ncmp), 1)
    tpos_c = t0 + lax.broadcasted_iota(jnp.int32, (tq, ncmp), 0)
    vis = (nidx * CMP_STRIDE + (CMP_LEN - 1)) <= tpos_c
    p_cmp = _softmax_rows(s, vis[None])
    o_cmp = _dot(p_cmp.reshape(nrep * tq, ncmp).astype(BF16), vc)

    p_sum = p_cmp[0]
    for r in range(1, nrep):
        p_sum = p_sum + p_cmp[r]
    p_hi, p_lo = _split_bf16(p_sum)
    c2s = c2s_ref[...]
    imp_t = _dot_nt(c2s, p_hi) + _dot_nt(c2s, p_lo)
    n_slc = e_ref.shape[0] * tk // SLC_LEN
    sel_t = _topk_mask_t(imp_t[:n_slc, :], t0)
    if n_slc < LANES:
        sel_t = jnp.concatenate([sel_t, jnp.zeros((LANES - n_slc, tq), F32)], axis=0)
    sel = jnp.transpose(sel_t).astype(BF16)

    tpos = t0 + lax.broadcasted_iota(jnp.int32, (tq, tk), 0)
    kiota = lax.broadcasted_iota(jnp.int32, (tq, tk), 1)

    def body(c, carry):
        m, l, acc = carry
        k0 = pl.multiple_of(c * tk, tk)
        kt = ks_ref[pl.ds(k0, tk), :]
        vt = vs_ref[pl.ds(k0, tk), :]
        sc = _dot_nt(qs, kt).reshape(nrep, tq, tk)
        allowed = (_dot(sel, e_ref[c]) > 0.5) & ((k0 + kiota) <= tpos)
        sc = jnp.where(allowed[None], sc, NEG)
        m_new = jnp.maximum(m, jnp.max(sc, axis=-1, keepdims=True))
        alpha = jnp.exp(m - m_new)
        p = jnp.exp(sc - m_new)
        l = alpha * l + jnp.sum(p, axis=-1, keepdims=True)
        pv = _dot(p.reshape(nrep * tq, tk).astype(BF16), vt).reshape(nrep, tq, LANES)
        return m_new, l, alpha * acc + pv

    n_kv = (t0 + tq + tk - 1) // tk
    init = (jnp.full((nrep, tq, 1), NEG, F32), jnp.zeros((nrep, tq, 1), F32),
            jnp.zeros((nrep, tq, LANES), F32))
    _, l_s, acc_s = lax.fori_loop(0, n_kv, body, init)
    o_slc = acc_s / l_s

    span = WINDOW + tq
    start = pl.multiple_of(jnp.maximum(t0 - WINDOW, 0), tq)
    kw = kw_ref[pl.ds(start, span), :]
    vw = vw_ref[pl.ds(start, span), :]
    sw = _dot_nt(qs, kw).reshape(nrep, tq, span)
    dist = (t0 + lax.broadcasted_iota(jnp.int32, (tq, span), 0)) - (
        start + lax.broadcasted_iota(jnp.int32, (tq, span), 1))
    band = (dist >= 0) & (dist < WINDOW)
    p_win = _softmax_rows(sw, band[None])
    o_win = _dot(p_win.reshape(nrep * tq, span).astype(BF16), vw).reshape(nrep, tq, LANES)

    o_cmp = o_cmp.reshape(nrep, tq, LANES)
    gt = gt_ref[...]
    heads = []
    for r in range(nrep):
        heads.append(gt[:, 3 * r:3 * r + 1] * o_cmp[r] + gt[:, 3 * r + 1:3 * r + 2] * o_slc[r]
                     + gt[:, 3 * r + 2:3 * r + 3] * o_win[r])
    slabs = [jnp.where(lo, heads[2 * p], heads[2 * p + 1]) for p in range(nrep // 2)]
    o_ref[...] = jnp.concatenate(slabs, axis=1).astype(o_ref.dtype)


def _nsa_attention(q, gt, cmp_kv, ks, vs, kw, vw, batch, seq):
    g = NSA_KV_GROUPS
    nq = seq // ATT_TQ
    nkt = seq // ATT_TK
    qw = (NSA_HEADS // g) * NSA_HEAD_DIM
    n_slc = seq // SLC_LEN
    n_cmp = (seq - CMP_LEN) // CMP_STRIDE + 1
    c_start = np.arange(n_cmp)[:, None] * CMP_STRIDE
    s_start = np.arange(n_slc)[None, :] * SLC_LEN
    overlap = np.clip(np.minimum(c_start + CMP_LEN, s_start + SLC_LEN)
                      - np.maximum(c_start, s_start), 0, None) / CMP_LEN
    c2s_t = np.zeros((LANES, cmp_kv.shape[3]))
    c2s_t[:n_slc, :n_cmp] = overlap.T
    expand = (np.arange(LANES)[None, :, None]
              == (np.arange(nkt)[:, None, None] * ATT_TK + np.arange(ATT_TK)[None, None, :]) // SLC_LEN)
    kvspec = pl.BlockSpec((seq, LANES), lambda b, gi, qi: (b, gi))
    return pl.pallas_call(
        _nsa_attn_kernel,
        grid=(batch, g, nq),
        in_specs=[pl.BlockSpec((ATT_TQ, qw), lambda b, gi, qi: (b * nq + qi, gi)),
                  pl.BlockSpec((ATT_TQ, LANES), lambda b, gi, qi: (b * nq + qi, gi)),
                  pl.BlockSpec((1, 2, 1) + cmp_kv.shape[3:], lambda b, gi, qi: (b, 0, gi, 0, 0)),
                  kvspec, kvspec, kvspec, kvspec,
                  pl.BlockSpec((LANES, cmp_kv.shape[3]), lambda b, gi, qi: (0, 0)),
                  pl.BlockSpec((nkt, LANES, ATT_TK), lambda b, gi, qi: (0, 0, 0))],
        out_specs=pl.BlockSpec((ATT_TQ, qw), lambda b, gi, qi: (b * nq + qi, gi)),
        out_shape=jax.ShapeDtypeStruct((batch * seq, NSA_HEADS * NSA_HEAD_DIM), BF16),
        compiler_params=_cparams(("parallel", "parallel", "arbitrary")),
        name="nsa_attention",
    )(q, gt, cmp_kv, ks, vs, kw, vw, jnp.asarray(c2s_t, BF16), jnp.asarray(expand, BF16))


def _block_diag(w):
    nb, bi, bo = w.shape
    eye = jnp.eye(nb, dtype=w.dtype)
    return (eye[:, None, :, None] * w[:, :, None, :]).reshape(nb * bi, nb * bo)


def _nsa_weight(w_in):
    hd = NSA_HEADS * NSA_HEAD_DIM
    kvw = NSA_KV_GROUPS * NSA_HEAD_DIM
    nrep = NSA_HEADS // NSA_KV_GROUPS
    d = w_in.shape[0]
    gates = w_in[:, hd + 6 * kvw:].reshape(d, NSA_KV_GROUPS, 3 * nrep)
    gates = jnp.pad(gates, ((0, 0), (0, 0), (0, LANES - 3 * nrep))).reshape(d, NSA_KV_GROUPS * LANES)
    return jnp.concatenate([w_in[:, :hd + 6 * kvw], gates], axis=1).astype(BF16)


def kernel(x, attn_norm, ab_w_in, conv_w, conv_b, gate_a_w, gate_a_b, gate_x_w, gate_x_b, lru_lambda,
           ab_w_out, nsa_w_in, cmp_pe_k, cmp_k_w1, cmp_k_w2, cmp_pe_v, cmp_v_w1, cmp_v_w2, nsa_w_out,
           ffn_norm, ffn_w1, ffn_w3, ffn_w2, final_norm):
    batch, seq, d = x.shape
    depth = attn_norm.shape[0]
    h = x.reshape(batch * seq, d)
    for layer in range(depth):
        i = layer // 2
        g_attn = attn_norm[layer][None, :]
        if layer % 2 == 0:
            proj = _proj_ab(h, g_attn, ab_w_in[i].astype(BF16))
            gate_w = jnp.concatenate([_block_diag(gate_a_w[i]), _block_diag(gate_x_w[i])], axis=1)
            gate_b = jnp.concatenate([gate_a_b[i], gate_x_b[i]])[None, :]
            lru_out = _lru(proj, batch, seq, conv_w[i], conv_b[i][None, :], gate_w.astype(BF16), gate_b,
                           lru_lambda[i][None, :])
            ret_out = _retention(proj, batch, seq)
            mixes, w_out = [lru_out, ret_out], ab_w_out[i]
        else:
            q, cmp_in, ks, vs, kw, vw, gt = _proj_nsa(h, g_attn, _nsa_weight(nsa_w_in[i]), seq)
            grp, dh = NSA_KV_GROUPS, NSA_HEAD_DIM
            half_feat = CMP_STRIDE * dh
            cmp_in = cmp_in.reshape(batch, seq // CMP_STRIDE, CMP_STRIDE, 2, grp, dh)
            cmp_in = cmp_in.transpose(0, 3, 4, 1, 2, 5).reshape(batch, 2, grp, seq // CMP_STRIDE, half_feat)
            pe = jnp.stack([cmp_pe_k[i], cmp_pe_v[i]]).reshape(2, 2, half_feat)
            w1 = jnp.stack([cmp_k_w1[i], cmp_v_w1[i]]).astype(BF16)
            w2 = jnp.stack([cmp_k_w2[i], cmp_v_w2[i]]).astype(BF16)
            w2 = jnp.concatenate([w2, w2], axis=2)
            n_blk = seq // CMP_STRIDE
            cos_k, sgn_k = _rot_tables(np.arange(n_blk) * CMP_STRIDE + CMP_LEN - 1, ROT_DIM, ROPE_THETA, 1.0)
            cos = jnp.stack([cos_k, jnp.ones_like(cos_k)])
            sgn = jnp.stack([sgn_k, jnp.zeros_like(sgn_k)])
            cmp_kv = _compress(cmp_in, pe, w1[:, :half_feat], w1[:, half_feat:], w2, cos, sgn)
            attn = _nsa_attention(q, gt, cmp_kv, ks, vs, kw, vw, batch, seq)
            mixes, w_out = [attn], nsa_w_out[i]
        last = layer == depth - 1
        h = _mix_ffn(h, mixes, w_out.astype(BF16), ffn_norm[layer][None, :],
                     ffn_w1[layer].astype(BF16), ffn_w3[layer].astype(BF16), ffn_w2[layer].astype(BF16),
                     final_g=final_norm[None, :] if last else None)
    return h.reshape(batch, seq, d)
```

```python
import functools

import numpy as np
import jax
import jax.numpy as jnp
from jax import lax
from jax.experimental import pallas as pl
from jax.experimental.pallas import tpu as pltpu

F32 = jnp.float32
BF16 = jnp.bfloat16

EPS = 1e-6
NEG = -1e30
BIG = 1e30
LOG2E = float(np.log2(np.e))

LRU_BLOCKS = 8
CONV_WIDTH = 4
LRU_C = 8.0
RET_HEADS = 8
RET_DIM = 64
RET_THETA = 10000.0
NSA_HEADS = 16
NSA_KV_GROUPS = 4
NSA_HEAD_DIM = 64
CMP_LEN = 32
CMP_STRIDE = 16
SLC_LEN = 64
SLC_TOPK = 16
WINDOW = 512
ROPE_THETA = 500000.0
ROT_DIM = NSA_HEAD_DIM // 4

LANES = 128
SUBLANES = 8
VMEM_LIMIT = 56 * 1024 * 1024

RET_CHUNK = 256
TOK_TILE = 512
LRU_TILE = 512
ATT_TQ = 128
ATT_TK = 512
N_CMP_PAD = 256


def _cparams(sem):
    return pltpu.CompilerParams(dimension_semantics=sem, vmem_limit_bytes=VMEM_LIMIT)


def _dot(a, b):
    return jnp.dot(a, b, preferred_element_type=F32)


def _dot_nt(a, b):
    return lax.dot_general(a, b, (((1,), (1,)), ((), ())), preferred_element_type=F32)


def _rms(x, g):
    return x * lax.rsqrt(jnp.mean(x * x, axis=-1, keepdims=True) + EPS) * g


def _tile_lanes(t, reps):
    return t if reps == 1 else jnp.concatenate([t] * reps, axis=1)


def _rotate(x, cos, sgn, half):
    n = x.shape[1]
    lane = lax.broadcasted_iota(jnp.int32, x.shape, 1)
    first = (lane % NSA_HEAD_DIM) < half
    partner = jnp.where(first, pltpu.roll(x, n - half, axis=1), pltpu.roll(x, half, axis=1))
    return x * cos + partner * sgn


def _group_slabs(x, fill):
    lane = lax.broadcasted_iota(jnp.int32, (x.shape[0], LANES), 1)
    lo = lane < NSA_HEAD_DIM
    out = []
    for c in range(x.shape[1] // LANES):
        xc = x[:, c * LANES:(c + 1) * LANES]
        out.append(jnp.where(lo, xc, fill))
        out.append(jnp.where(lo, pltpu.roll(xc, NSA_HEAD_DIM, axis=1), fill))
    return jnp.concatenate(out, axis=1)


def _rot_tables(pos, rot_dim, theta, scale):
    half = rot_dim // 2
    inv = theta ** (-np.arange(0, rot_dim, 2, dtype=np.float64) / rot_dim)
    ang = np.asarray(pos, np.float64)[:, None] * inv
    cos = np.ones((len(pos), NSA_HEAD_DIM))
    sgn = np.zeros((len(pos), NSA_HEAD_DIM))
    cos[:, :half] = np.cos(ang)
    cos[:, half:rot_dim] = np.cos(ang)
    sgn[:, :half] = -np.sin(ang)
    sgn[:, half:rot_dim] = np.sin(ang)
    cos = np.tile(cos * scale, (1, 2))
    sgn = np.tile(sgn * scale, (1, 2))
    return jnp.asarray(cos, F32), jnp.asarray(sgn, F32)


def _ret_decay_tables(chunk):
    log_g = np.log1p(-(2.0 ** (-5.0 - np.arange(RET_HEADS, dtype=np.float64))))
    ci = np.arange(chunk, dtype=np.float64)
    diff = ci[:, None] - ci[None, :]
    inner = np.where(diff >= 0, np.exp(np.maximum(diff, 0.0) * log_g[:, None, None]), 0.0)
    lane_head = np.repeat(np.arange(RET_HEADS), RET_DIM)
    qdec = np.exp((ci[:, None] + 1.0) * log_g[lane_head][None, :])
    kdec = np.exp((chunk - 1.0 - ci[:, None]) * log_g[lane_head][None, :])
    cdec = np.exp(chunk * log_g[lane_head])[None, :]
    return (jnp.asarray(inner, F32), jnp.asarray(qdec, F32), jnp.asarray(kdec, F32),
            jnp.asarray(cdec, F32))


def _proj_ab_kernel(x_ref, g_ref, w_ref, o_ref):
    xn = _rms(x_ref[...], g_ref[...]).astype(BF16)
    o_ref[...] = _dot(xn, w_ref[...])


def _proj_ab(x2, g, w):
    t, d = x2.shape
    n = w.shape[1]
    return pl.pallas_call(
        _proj_ab_kernel,
        grid=(t // TOK_TILE,),
        in_specs=[pl.BlockSpec((TOK_TILE, d), lambda i: (i, 0)),
                  pl.BlockSpec((1, d), lambda i: (0, 0)),
                  pl.BlockSpec((d, n), lambda i: (0, 0), pipeline_mode=pl.Buffered(1))],
        out_specs=pl.BlockSpec((TOK_TILE, n), lambda i: (i, 0)),
        out_shape=jax.ShapeDtypeStruct((t, n), F32),
        compiler_params=_cparams(("parallel",)),
        name="proj_ab",
    )(x2, g, w)


def _lru_kernel(y_ref, x_ref, cw_ref, cb_ref, gw_ref, gb_ref, lam_ref, o_ref,
                xp_ref, a_ref, u_ref, h_ref, carry_ref):
    ts, w = x_ref.shape
    hist = SUBLANES

    @pl.when(pl.program_id(1) == 0)
    def _():
        xp_ref[0:hist, :] = jnp.zeros((hist, w), F32)
        carry_ref[...] = jnp.zeros_like(carry_ref)

    xp_ref[hist:hist + ts, :] = x_ref[...]
    xc = cb_ref[...]
    for k in range(CONV_WIDTH):
        off = hist - (CONV_WIDTH - 1) + k
        xc = xc + xp_ref[off:off + ts, :] * cw_ref[k:k + 1, :]
    xp_ref[0:hist, :] = x_ref[ts - hist:ts, :]

    gates = _dot(xc.astype(BF16), gw_ref[...]) + gb_ref[...]
    r = jax.nn.sigmoid(gates[:, :w])
    i = jax.nn.sigmoid(gates[:, w:])
    log_a = -LRU_C * r * jax.nn.softplus(-lam_ref[...])
    a = jnp.exp(log_a)
    a_ref[...] = a
    u_ref[...] = jnp.sqrt(jnp.tanh(-log_a) * (a * a + 1.0)) * (i * xc)

    row = lax.broadcasted_iota(jnp.int32, (SUBLANES, w), 0)

    def body(gi, carry):
        r0 = pl.multiple_of(gi * SUBLANES, SUBLANES)
        a = a_ref[pl.ds(r0, SUBLANES), :]
        u = u_ref[pl.ds(r0, SUBLANES), :]
        for d in (1, 2, 4):
            a_s = pltpu.roll(a, d, axis=0)
            u_s = pltpu.roll(u, d, axis=0)
            m = row >= d
            u = jnp.where(m, a * u_s + u, u)
            a = jnp.where(m, a * a_s, a)
        h = a * carry + u
        h_ref[pl.ds(r0, SUBLANES), :] = h
        return h[SUBLANES - 1:SUBLANES, :]

    carry_ref[...] = lax.fori_loop(0, ts // SUBLANES, body, carry_ref[...])
    o_ref[...] = (h_ref[...] * jax.nn.gelu(y_ref[...])).astype(o_ref.dtype)


def _lru(proj, batch, seq, conv_w, conv_b, gate_w, gate_b, lam):
    w = conv_w.shape[1]
    ns = seq // LRU_TILE
    return pl.pallas_call(
        _lru_kernel,
        grid=(batch, ns),
        in_specs=[pl.BlockSpec((LRU_TILE, w), lambda b, s: (b * ns + s, 0)),
                  pl.BlockSpec((LRU_TILE, w), lambda b, s: (b * ns + s, 1)),
                  pl.BlockSpec((CONV_WIDTH, w), lambda b, s: (0, 0)),
                  pl.BlockSpec((1, w), lambda b, s: (0, 0)),
                  pl.BlockSpec((w, 2 * w), lambda b, s: (0, 0)),
                  pl.BlockSpec((1, 2 * w), lambda b, s: (0, 0)),
                  pl.BlockSpec((1, w), lambda b, s: (0, 0))],
        out_specs=pl.BlockSpec((LRU_TILE, w), lambda b, s: (b * ns + s, 0)),
        out_shape=jax.ShapeDtypeStruct((batch * seq, w), BF16),
        scratch_shapes=[pltpu.VMEM((LRU_TILE + SUBLANES, w), F32),
                        pltpu.VMEM((LRU_TILE, w), F32),
                        pltpu.VMEM((LRU_TILE, w), F32),
                        pltpu.VMEM((LRU_TILE, w), F32),
                        pltpu.VMEM((1, w), F32)],
        compiler_params=_cparams(("parallel", "arbitrary")),
        name="rg_lru",
    )(proj, proj, conv_w, conv_b, gate_w, gate_b, lam)


def _split_bf16(x):
    hi = x.astype(BF16)
    lo = (x - hi.astype(F32)).astype(BF16)
    return hi, lo


def _ret_kernel(q_ref, k_ref, v_ref, g_ref, cq_ref, sq_ref, ck_ref, sk_ref,
                inner_ref, qdec_ref, kdec_ref, cdec_ref, avg_ref, o_ref, st_ref):
    c, w = q_ref.shape
    npair = w // LANES
    b = pl.program_id(1)

    @pl.when(pl.program_id(0) == 0)
    def _():
        st_ref[b] = jnp.zeros(st_ref.shape[1:], F32)

    half = RET_DIM // 2
    q = _rotate(q_ref[...], _tile_lanes(cq_ref[...], npair), _tile_lanes(sq_ref[...], npair), half)
    k = _rotate(k_ref[...], _tile_lanes(ck_ref[...], npair), _tile_lanes(sk_ref[...], npair), half)
    v = v_ref[...]
    qd = (q * qdec_ref[...]).astype(BF16)
    kd = k * kdec_ref[...]
    q = q.astype(BF16)
    kb = k.astype(BF16)
    vb = v.astype(BF16)

    lane = lax.broadcasted_iota(jnp.int32, (c, LANES), 1)
    lo = lane < RET_DIM
    rr = lax.broadcasted_iota(jnp.int32, (LANES, LANES), 0) < RET_DIM
    cc = lax.broadcasted_iota(jnp.int32, (LANES, LANES), 1) < RET_DIM
    same_head = rr == cc
    zero = jnp.zeros((c, LANES), BF16)

    outs = []
    for p in range(npair):
        sl = slice(p * LANES, (p + 1) * LANES)
        qp, kp, vp = q[:, sl], kb[:, sl], vb[:, sl]
        s_lo = _dot_nt(jnp.where(lo, qp, zero), kp) * inner_ref[2 * p]
        s_hi = _dot_nt(jnp.where(lo, zero, qp), kp) * inner_ref[2 * p + 1]
        o = jnp.where(lo, _dot(s_lo.astype(BF16), vp), _dot(s_hi.astype(BF16), vp))
        st = st_ref[b, p]
        o = o + _dot(qd[:, sl], st.astype(BF16))
        kv = _dot(jnp.transpose(kd[:, sl]).astype(BF16), vp)
        st_ref[b, p] = st * cdec_ref[:, sl] + jnp.where(same_head, kv, 0.0)
        outs.append(o)

    o_all = jnp.concatenate(outs, axis=0)
    avg = avg_ref[...]
    o_hi, o_lo = _split_bf16(o_all)
    mu = _dot(o_hi, avg) + _dot(o_lo, avg)
    dlt = o_all - mu
    d_hi, d_lo = _split_bf16(dlt * dlt)
    var = _dot(d_hi, avg) + _dot(d_lo, avg)
    y = dlt * lax.rsqrt(var + EPS)
    y = jnp.concatenate([y[p * c:(p + 1) * c, :] for p in range(npair)], axis=1)
    o_ref[...] = (y * jax.nn.silu(g_ref[...])).astype(o_ref.dtype)


def _retention(proj, batch, seq):
    w = RET_HEADS * RET_DIM
    c = RET_CHUNK
    nc = seq // c
    npair = w // LANES
    pos = np.arange(seq)
    cq, sq = _rot_tables(pos, RET_DIM, RET_THETA, 1.0)
    ck, sk = _rot_tables(pos, RET_DIM, RET_THETA, RET_DIM ** -0.5)
    inner, qdec, kdec, cdec = _ret_decay_tables(c)
    blk = np.arange(LANES) // RET_DIM
    avg = jnp.asarray((blk[:, None] == blk[None, :]) / RET_DIM, BF16)
    col0 = 2
    tok = lambda j: pl.BlockSpec((c, w), lambda ci, b, j=j: (b * nc + ci, j))
    tab = pl.BlockSpec((c, LANES), lambda ci, b: (ci, 0))
    const = lambda shape: pl.BlockSpec(shape, lambda ci, b: (0,) * len(shape))
    return pl.pallas_call(
        _ret_kernel,
        grid=(nc, batch),
        in_specs=[tok(col0), tok(col0 + 1), tok(col0 + 2), tok(col0 + 3),
                  tab, tab, tab, tab,
                  const((RET_HEADS, c, c)), const((c, w)), const((c, w)), const((1, w)),
                  const((LANES, LANES))],
        out_specs=pl.BlockSpec((c, w), lambda ci, b: (b * nc + ci, 0)),
        out_shape=jax.ShapeDtypeStruct((batch * seq, w), BF16),
        scratch_shapes=[pltpu.VMEM((batch, npair, LANES, LANES), F32)],
        compiler_params=_cparams(("arbitrary", "arbitrary")),
        name="retention",
    )(proj, proj, proj, proj, cq, sq, ck, sk, inner, qdec, kdec, cdec, avg)


def _mix_ffn_kernel(*refs, n_mix, final):
    h_ref = refs[0]
    mix_refs = refs[1:1 + n_mix]
    wo_ref, g_ref, w1_ref, w3_ref, w2_ref = refs[1 + n_mix:6 + n_mix]
    rest = refs[6 + n_mix:]
    fg_ref = rest[0] if final else None
    o_ref = rest[-1]

    mixed = mix_refs[0][...] if n_mix == 1 else jnp.concatenate([r[...] for r in mix_refs], axis=1)
    h1 = h_ref[...] + _dot(mixed, wo_ref[...])
    xn = _rms(h1, g_ref[...]).astype(BF16)
    act = (jax.nn.silu(_dot(xn, w1_ref[...])) * _dot(xn, w3_ref[...])).astype(BF16)
    h2 = h1 + _dot(act, w2_ref[...])
    if final:
        h2 = _rms(h2, fg_ref[...])
    o_ref[...] = h2


def _mix_ffn(h2d, mixes, wo, g, w1, w3, w2, final_g=None):
    t, d = h2d.shape
    hid = w1.shape[1]
    final = final_g is not None
    row = lambda width: pl.BlockSpec((TOK_TILE, width), lambda i: (i, 0))
    const = lambda shape: pl.BlockSpec(shape, lambda i: (0, 0), pipeline_mode=pl.Buffered(1))
    in_specs = [row(d)] + [row(m.shape[1]) for m in mixes] + [
        const(wo.shape), const((1, d)), const((d, hid)), const((d, hid)), const((hid, d))]
    args = [h2d, *mixes, wo, g, w1, w3, w2]
    if final:
        in_specs.append(const((1, d)))
        args.append(final_g)
    return pl.pallas_call(
        functools.partial(_mix_ffn_kernel, n_mix=len(mixes), final=final),
        grid=(t // TOK_TILE,),
        in_specs=in_specs,
        out_specs=row(d),
        out_shape=jax.ShapeDtypeStruct((t, d), F32),
        compiler_params=_cparams(("parallel",)),
        name="mix_ffn_final" if final else "mix_ffn",
    )(*args)


def _proj_nsa_kernel(x_ref, g_ref, w_ref, cq_ref, sq_ref, ck_ref, sk_ref,
                     q_ref, cmp_ref, ks_ref, vs_ref, kw_ref, vw_ref, gt_ref, *, seq_tiles):
    hd = q_ref.shape[1]
    kvw = cmp_ref.shape[1] // 2
    xn = _rms(x_ref[...], g_ref[...]).astype(BF16)
    proj = _dot(xn, w_ref[...])
    half = ROT_DIM // 2
    q = proj[:, :hd]
    q_ref[...] = _rotate(q, _tile_lanes(cq_ref[...], hd // LANES),
                         _tile_lanes(sq_ref[...], hd // LANES), half).astype(q_ref.dtype)
    kv = proj[:, hd:hd + 6 * kvw]
    cmp_ref[...] = kv[:, :2 * kvw]
    ck = _tile_lanes(ck_ref[...], kvw // LANES)
    sk = _tile_lanes(sk_ref[...], kvw // LANES)
    rows = x_ref.shape[0]
    tpos = (pl.program_id(0) % seq_tiles) * rows + lax.broadcasted_iota(jnp.int32, (rows, LANES), 0)
    lane = lax.broadcasted_iota(jnp.int32, (rows, LANES), 1)
    block_onehot = jnp.where(lane - NSA_HEAD_DIM == tpos // SLC_LEN, 1.0, 0.0)
    ks_ref[...] = _group_slabs(_rotate(kv[:, 2 * kvw:3 * kvw], ck, sk, half), block_onehot).astype(ks_ref.dtype)
    vs_ref[...] = _group_slabs(kv[:, 3 * kvw:4 * kvw], 1.0).astype(vs_ref.dtype)
    kw_ref[...] = _group_slabs(_rotate(kv[:, 4 * kvw:5 * kvw], ck, sk, half), 0.0).astype(kw_ref.dtype)
    vw_ref[...] = _group_slabs(kv[:, 5 * kvw:6 * kvw], 1.0).astype(vw_ref.dtype)
    gt_ref[...] = jax.nn.sigmoid(proj[:, hd + 6 * kvw:])


def _proj_nsa(x2, g, w, seq):
    t, d = x2.shape
    hd = NSA_HEADS * NSA_HEAD_DIM
    kvw = NSA_KV_GROUPS * NSA_HEAD_DIM
    gw = NSA_KV_GROUPS * LANES
    n = w.shape[1]
    ns = seq // TOK_TILE
    pos = np.arange(seq)
    cq, sq = _rot_tables(pos, ROT_DIM, ROPE_THETA, NSA_HEAD_DIM ** -0.5 * LOG2E)
    ck, sk = _rot_tables(pos, ROT_DIM, ROPE_THETA, 1.0)
    row = lambda width: pl.BlockSpec((TOK_TILE, width), lambda i: (i, 0))
    tab = pl.BlockSpec((TOK_TILE, LANES), lambda i: (i % ns, 0))
    outs = [(hd, F32), (2 * kvw, F32), (gw, BF16), (gw, BF16), (gw, BF16), (gw, BF16), (gw, F32)]
    return pl.pallas_call(
        functools.partial(_proj_nsa_kernel, seq_tiles=ns),
        grid=(t // TOK_TILE,),
        in_specs=[row(d), pl.BlockSpec((1, d), lambda i: (0, 0)),
                  pl.BlockSpec((d, n), lambda i: (0, 0), pipeline_mode=pl.Buffered(1)),
                  tab, tab, tab, tab],
        out_specs=[row(wd) for wd, _ in outs],
        out_shape=[jax.ShapeDtypeStruct((t, wd), dt) for wd, dt in outs],
        compiler_params=_cparams(("parallel",)),
        name="proj_nsa",
    )(x2, g, w, cq, sq, ck, sk)


def _compress_kernel(x_ref, pe_ref, w1a_ref, w1b_ref, w2_ref, cos_ref, sgn_ref, o_ref):
    x = x_ref[0, 0, 0]
    pe = pe_ref[0]
    u = _dot((x + pe[0:1, :]).astype(BF16), w1a_ref[0])
    v = _dot((x + pe[1:2, :]).astype(BF16), w1b_ref[0])
    nrow = x.shape[0]
    pre = u + pltpu.roll(v, nrow - 1, axis=0)
    out = _dot(jax.nn.gelu(pre).astype(BF16), w2_ref[0])
    out = _rotate(out, cos_ref[0], sgn_ref[0], ROT_DIM // 2)
    lane = lax.broadcasted_iota(jnp.int32, out.shape, 1)
    ones_hi = jnp.where((lane >= NSA_HEAD_DIM) & (pl.program_id(1) == 1), 1.0, 0.0)
    o_ref[0, 0, 0] = (out + ones_hi).astype(o_ref.dtype)


def _compress(cmp_in, pe, w1a, w1b, w2, cos, sgn):
    b, _, g, n, feat = cmp_in.shape
    per_kv = lambda shape: pl.BlockSpec((1,) + shape, lambda bi, j, gi: (j,) + (0,) * len(shape))
    return pl.pallas_call(
        _compress_kernel,
        grid=(b, 2, g),
        in_specs=[pl.BlockSpec((1, 1, 1, n, feat), lambda bi, j, gi: (bi, j, gi, 0, 0)),
                  per_kv((2, feat)), per_kv(w1a.shape[1:]), per_kv(w1b.shape[1:]),
                  per_kv(w2.shape[1:]), per_kv((n, LANES)), per_kv((n, LANES))],
        out_specs=pl.BlockSpec((1, 1, 1, n, LANES), lambda bi, j, gi: (bi, j, gi, 0, 0)),
        out_shape=jax.ShapeDtypeStruct((b, 2, g, n, LANES), BF16),
        compiler_params=_cparams(("parallel", "arbitrary", "arbitrary")),
        name="compress",
    )(cmp_in, pe, w1a, w1b, w2, cos, sgn)


def _topk_bias_t(imp_t, t0, val_ref, cnt_ref):
    nb, tq = imp_t.shape
    j = lax.broadcasted_iota(jnp.int32, (nb, tq), 0)
    cur = (t0 + lax.broadcasted_iota(jnp.int32, (nb, tq), 1)) // SLC_LEN
    forced = (j == 0) | (j == cur) | (j == cur - 1)
    causal = j <= cur
    val_ref[...] = jnp.where(forced, BIG, jnp.where(causal, imp_t, NEG))
    cnt_ref[...] = jnp.zeros((nb, tq), F32)
    nchunk = nb // SUBLANES
    rows = lax.broadcasted_iota(jnp.int32, (SUBLANES, tq), 0)
    last_block = (t0 + tq - 1) // SLC_LEN
    for ic in range(nchunk):
        @pl.when(ic * SUBLANES <= last_block)
        def _():
            for c in range(nchunk):
                vc = val_ref[c * SUBLANES:(c + 1) * SUBLANES, :]
                acc = cnt_ref[c * SUBLANES:(c + 1) * SUBLANES, :]
                for ii in range(SUBLANES):
                    vi = val_ref[ic * SUBLANES + ii:ic * SUBLANES + ii + 1, :]
                    if c > ic:
                        ahead = jnp.where(vi >= vc, 1.0, 0.0)
                    elif c < ic:
                        ahead = jnp.where(vi > vc, 1.0, 0.0)
                    else:
                        ahead = jnp.where(rows > ii, jnp.where(vi >= vc, 1.0, 0.0),
                                          jnp.where(vi > vc, 1.0, 0.0))
                    acc = acc + ahead
                cnt_ref[c * SUBLANES:(c + 1) * SUBLANES, :] = acc
    return jnp.where((cnt_ref[...] < float(SLC_TOPK)) & causal, 0.0, NEG)


def _softmax_pv(s_all, bias, v, nrep, tq):
    ps = []
    for r in range(nrep):
        sr = s_all[r * tq:(r + 1) * tq, :] + bias
        m = jnp.max(sr, axis=-1, keepdims=True)
        ps.append(jnp.exp2(sr - m).astype(BF16))
    return _dot(jnp.concatenate(ps, axis=0), v)


def _nsa_attn_kernel(q_ref, gt_ref, cmp_ref, ks_ref, vs_ref, kw_ref, vw_ref, c2s_ref, o_ref,
                     val_ref, cnt_ref, *, n_slc, tk):
    tq = q_ref.shape[0]
    ncmp = cmp_ref.shape[3]
    nrep = NSA_HEADS // NSA_KV_GROUPS
    t0 = pl.program_id(2) * tq

    lane = lax.broadcasted_iota(jnp.int32, (tq, LANES), 1)
    lo = lane < NSA_HEAD_DIM
    q = q_ref[...]
    q_heads = []
    for p in range(nrep // 2):
        qp = q[:, p * LANES:(p + 1) * LANES]
        q_heads += [qp, pltpu.roll(qp, NSA_HEAD_DIM, axis=1)]
    qs = jnp.concatenate([jnp.where(lo, qh, 0.0) for qh in q_heads], axis=0).astype(BF16)

    kc = cmp_ref[0, 0, 0]
    vc = cmp_ref[0, 1, 0]
    s = _dot_nt(qs, kc)
    nidx = lax.broadcasted_iota(jnp.int32, (tq, ncmp), 1)
    tpos_c = t0 + lax.broadcasted_iota(jnp.int32, (tq, ncmp), 0)
    vis_bias = jnp.where((nidx * CMP_STRIDE + (CMP_LEN - 1)) <= tpos_c, 0.0, NEG)
    any_vis = jnp.where(t0 + lax.broadcasted_iota(jnp.int32, (tq, 1), 0) >= CMP_LEN - 1, 1.0, 0.0)
    p_sum = jnp.zeros((tq, ncmp), F32)
    p_cmp = []
    for r in range(nrep):
        sr = s[r * tq:(r + 1) * tq, :] + vis_bias
        e = jnp.exp2(sr - jnp.max(sr, axis=-1, keepdims=True))
        pr = e * (any_vis / jnp.sum(e, axis=-1, keepdims=True))
        p_sum = p_sum + pr
        p_cmp.append(pr.astype(BF16))
    o_cmp = _dot(jnp.concatenate(p_cmp, axis=0), vc)

    p_hi, p_lo = _split_bf16(p_sum)
    c2s = c2s_ref[...]
    imp_t = _dot_nt(c2s, p_hi) + _dot_nt(c2s, p_lo)
    bias_t = _topk_bias_t(imp_t[:n_slc, :], t0, val_ref, cnt_ref)
    pad = [jnp.full((NSA_HEAD_DIM - n_slc, tq), NEG, F32)] if n_slc < NSA_HEAD_DIM else []
    bias_t = jnp.concatenate([jnp.zeros((NSA_HEAD_DIM, tq), F32), bias_t] + pad, axis=0)
    sel_bias = jnp.transpose(bias_t)
    qs_sel = jnp.concatenate([jnp.where(lo, qh, sel_bias) for qh in q_heads], axis=0).astype(BF16)

    def tile(c, carry, token_bias):
        k0 = pl.multiple_of(c * tk, tk)
        kt = ks_ref[pl.ds(k0, tk), :]
        vt = vs_ref[pl.ds(k0, tk), :]
        sc = _dot_nt(qs_sel, kt)
        ps, ms, alphas = [], [], []
        for r in range(nrep):
            sr = sc[r * tq:(r + 1) * tq, :]
            if token_bias is not None:
                sr = sr + token_bias
            m_new = jnp.maximum(carry[r][0], jnp.max(sr, axis=-1, keepdims=True))
            alphas.append(jnp.exp2(carry[r][0] - m_new))
            ms.append(m_new)
            ps.append(jnp.exp2(sr - m_new).astype(BF16))
        pv = _dot(jnp.concatenate(ps, axis=0), vt)
        return tuple((ms[r], alphas[r] * carry[r][1] + pv[r * tq:(r + 1) * tq, :]) for r in range(nrep))

    last = (t0 + tq - 1) // tk
    init = tuple((jnp.full((tq, 1), NEG, F32), jnp.zeros((tq, LANES), F32)) for _ in range(nrep))
    carry = lax.fori_loop(0, last, lambda c, cr: tile(c, cr, None), init)
    k_last = last * tk + lax.broadcasted_iota(jnp.int32, (tq, tk), 1)
    causal_bias = jnp.where(k_last <= t0 + lax.broadcasted_iota(jnp.int32, (tq, tk), 0), 0.0, NEG)
    carry = tile(last, carry, causal_bias)

    span = WINDOW + tq
    start = pl.multiple_of(jnp.maximum(t0 - WINDOW, 0), tq)
    sw = _dot_nt(qs, kw_ref[pl.ds(start, span), :])
    dist = (t0 + lax.broadcasted_iota(jnp.int32, (tq, span), 0)) - (
        start + lax.broadcasted_iota(jnp.int32, (tq, span), 1))
    band_bias = jnp.where((dist >= 0) & (dist < WINDOW), 0.0, NEG)
    o_win = _softmax_pv(sw, band_bias, vw_ref[pl.ds(start, span), :], nrep, tq)

    gt = gt_ref[...]
    heads = []
    for r in range(nrep):
        acc_s = carry[r][1]
        o_w = o_win[r * tq:(r + 1) * tq, :]
        o_s = acc_s / pltpu.roll(acc_s, NSA_HEAD_DIM, axis=1)
        o_w = o_w / pltpu.roll(o_w, NSA_HEAD_DIM, axis=1)
        heads.append(gt[:, 3 * r:3 * r + 1] * o_cmp[r * tq:(r + 1) * tq, :]
                     + gt[:, 3 * r + 1:3 * r + 2] * o_s + gt[:, 3 * r + 2:3 * r + 3] * o_w)
    slabs = [jnp.where(lo, heads[2 * p], pltpu.roll(heads[2 * p + 1], NSA_HEAD_DIM, axis=1))
             for p in range(nrep // 2)]
    o_ref[...] = jnp.concatenate(slabs, axis=1).astype(o_ref.dtype)


def _nsa_attention(q, gt, cmp_kv, ks, vs, kw, vw, batch, seq):
    g = NSA_KV_GROUPS
    nq = seq // ATT_TQ
    qw =(NSA_HEADS // g) * NSA_HEAD_DIM
    n_slc = seq // SLC_LEN
    n_cmp = (seq - CMP_LEN) // CMP_STRIDE + 1
    c_start = np.arange(n_cmp)[:, None] * CMP_STRIDE
    s_start = np.arange(n_slc)[None, :] * SLC_LEN
    overlap = np.clip(np.minimum(c_start + CMP_LEN, s_start + SLC_LEN)
                      - np.maximum(c_start, s_start), 0, None) / CMP_LEN
    assert n_slc <= NSA_HEAD_DIM and n_slc % SUBLANES == 0 and seq % ATT_TK == 0
    c2s_t = np.zeros((NSA_HEAD_DIM, cmp_kv.shape[3]))
    c2s_t[:n_slc, :n_cmp] = overlap.T
    kvspec = pl.BlockSpec((seq, LANES), lambda b, gi, qi: (b, gi))
    return pl.pallas_call(
        functools.partial(_nsa_attn_kernel, n_slc=n_slc, tk=ATT_TK),
        grid=(batch, g, nq),
        in_specs=[pl.BlockSpec((ATT_TQ, qw), lambda b, gi, qi: (b * nq + qi, gi)),
                  pl.BlockSpec((ATT_TQ, LANES), lambda b, gi, qi: (b * nq + qi, gi)),
                  pl.BlockSpec((1, 2, 1) + cmp_kv.shape[3:], lambda b, gi, qi: (b, 0, gi, 0, 0)),
                  kvspec, kvspec, kvspec, kvspec,
                  pl.BlockSpec(c2s_t.shape, lambda b, gi, qi: (0, 0))],
        out_specs=pl.BlockSpec((ATT_TQ, qw), lambda b, gi, qi: (b * nq + qi, gi)),
        out_shape=jax.ShapeDtypeStruct((batch * seq, NSA_HEADS * NSA_HEAD_DIM), BF16),
        scratch_shapes=[pltpu.VMEM((n_slc, ATT_TQ), F32), pltpu.VMEM((n_slc, ATT_TQ), F32)],
        compiler_params=_cparams(("parallel", "parallel", "arbitrary")),
        name="nsa_attention",
    )(q, gt, cmp_kv, ks, vs, kw, vw, jnp.asarray(c2s_t, BF16))


def _block_diag(w):
    nb, bi, bo = w.shape
    eye = jnp.eye(nb, dtype=w.dtype)
    return (eye[:, None, :, None] * w[:, :, None, :]).reshape(nb * bi, nb * bo)


def _nsa_weight(w_in):
    hd = NSA_HEADS * NSA_HEAD_DIM
    kvw = NSA_KV_GROUPS * NSA_HEAD_DIM
    nrep = NSA_HEADS // NSA_KV_GROUPS
    d = w_in.shape[0]
    gates = w_in[:, hd + 6 * kvw:].reshape(d, NSA_KV_GROUPS, 3 * nrep)
    gates = jnp.pad(gates, ((0, 0), (0, 0), (0, LANES - 3 * nrep))).reshape(d, NSA_KV_GROUPS * LANES)
    return jnp.concatenate([w_in[:, :hd + 6 * kvw], gates], axis=1).astype(BF16)


def kernel(x, attn_norm, ab_w_in, conv_w, conv_b, gate_a_w, gate_a_b, gate_x_w, gate_x_b, lru_lambda,
           ab_w_out, nsa_w_in, cmp_pe_k, cmp_k_w1, cmp_k_w2, cmp_pe_v, cmp_v_w1, cmp_v_w2, nsa_w_out,
           ffn_norm, ffn_w1, ffn_w3, ffn_w2, final_norm):
    batch, seq, d = x.shape
    depth = attn_norm.shape[0]
    h = x.reshape(batch * seq, d)
    for layer in range(depth):
        i = layer // 2
        g_attn = attn_norm[layer][None, :]
        if layer % 2 == 0:
            proj = _proj_ab(h, g_attn, ab_w_in[i].astype(BF16))
            gate_w = jnp.concatenate([_block_diag(gate_a_w[i]), _block_diag(gate_x_w[i])], axis=1)
            gate_b = jnp.concatenate([gate_a_b[i], gate_x_b[i]])[None, :]
            lru_out = _lru(proj, batch, seq, conv_w[i], conv_b[i][None, :], gate_w.astype(BF16), gate_b,
                           lru_lambda[i][None, :])
            ret_out = _retention(proj, batch, seq)
            mixes, w_out = [lru_out, ret_out], ab_w_out[i]
        else:
            q, cmp_in, ks, vs, kw, vw, gt = _proj_nsa(h, g_attn, _nsa_weight(nsa_w_in[i]), seq)
            grp, dh = NSA_KV_GROUPS, NSA_HEAD_DIM
            half_feat = CMP_STRIDE * dh
            cmp_in = cmp_in.reshape(batch, seq // CMP_STRIDE, CMP_STRIDE, 2, grp, dh)
            cmp_in = cmp_in.transpose(0, 3, 4, 1, 2, 5).reshape(batch, 2, grp, seq // CMP_STRIDE, half_feat)
            pe = jnp.stack([cmp_pe_k[i], cmp_pe_v[i]]).reshape(2, 2, half_feat)
            w1 = jnp.stack([cmp_k_w1[i], cmp_v_w1[i]]).astype(BF16)
            w2 = jnp.stack([cmp_k_w2[i], cmp_v_w2[i]]).astype(BF16)
            w2 = jnp.pad(w2, ((0, 0), (0, 0), (0, LANES - dh)))
            n_blk = seq // CMP_STRIDE
            cos_k, sgn_k = _rot_tables(np.arange(n_blk) * CMP_STRIDE + CMP_LEN - 1, ROT_DIM, ROPE_THETA, 1.0)
            cos = jnp.stack([cos_k, jnp.ones_like(cos_k)])
            sgn = jnp.stack([sgn_k, jnp.zeros_like(sgn_k)])
            cmp_kv = _compress(cmp_in, pe, w1[:, :half_feat], w1[:, half_feat:], w2, cos, sgn)
            attn = _nsa_attention(q, gt, cmp_kv, ks, vs, kw, vw, batch, seq)
            mixes, w_out = [attn], nsa_w_out[i]
        last = layer == depth - 1
        h = _mix_ffn(h, mixes, w_out.astype(BF16), ffn_norm[layer][None, :],
                     ffn_w1[layer].astype(BF16), ffn_w3[layer].astype(BF16), ffn_w2[layer].astype(BF16),
                     final_g=final_norm[None, :] if last else None)
    return h.reshape(batch, seq, d)
```

```python
import functools

import numpy as np
import jax
import jax.numpy as jnp
from jax import lax
from jax.experimental import pallas as pl
from jax.experimental.pallas import tpu as pltpu

F32 = jnp.float32
BF16 = jnp.bfloat16

EPS = 1e-6
NEG = -1e30
BIG = 1e30
LOG2E = float(np.log2(np.e))

LRU_BLOCKS = 8
CONV_WIDTH = 4
LRU_C = 8.0
RET_HEADS = 8
RET_DIM = 64
RET_THETA = 10000.0
NSA_HEADS = 16
NSA_KV_GROUPS = 4
NSA_HEAD_DIM = 64
CMP_LEN = 32
CMP_STRIDE = 16
SLC_LEN = 64
SLC_TOPK = 16
WINDOW = 512
ROPE_THETA = 500000.0
ROT_DIM = NSA_HEAD_DIM // 4

LANES = 128
SUBLANES = 8
VMEM_LIMIT = 56 * 1024 * 1024

RET_CHUNK = 256
TOK_TILE = 512
LRU_TILE = 512
ATT_TQ = 128
ATT_TK = 512
V_ROWS = NSA_HEAD_DIM + 16
MAX_FIXED_SHIFT = 32.0
NORM_MARGIN = 1.01


def _cparams(sem):
    return pltpu.CompilerParams(dimension_semantics=sem, vmem_limit_bytes=VMEM_LIMIT)


def _dot(a, b):
    return jnp.dot(a, b, preferred_element_type=F32)


def _dot_nt(a, b):
    return lax.dot_general(a, b, (((1,), (1,)), ((), ())), preferred_element_type=F32)


def _rms(x, g):
    return x * lax.rsqrt(jnp.mean(x * x, axis=-1, keepdims=True) + EPS) * g


def _tile_lanes(t, reps):
    return t if reps == 1 else jnp.concatenate([t] * reps, axis=1)


def _rotate(x, cos, sgn, half):
    n = x.shape[1]
    lane = lax.broadcasted_iota(jnp.int32, x.shape, 1)
    first = (lane % NSA_HEAD_DIM) < half
    partner = jnp.where(first, pltpu.roll(x, n - half, axis=1), pltpu.roll(x, half, axis=1))
    return x * cos + partner * sgn


def _group_slabs(x, fill):
    lane = lax.broadcasted_iota(jnp.int32, (x.shape[0], LANES), 1)
    lo = lane < NSA_HEAD_DIM
    out = []
    for c in range(x.shape[1] // LANES):
        xc = x[:, c * LANES:(c + 1) * LANES]
        out.append(jnp.where(lo, xc, fill))
        out.append(jnp.where(lo, pltpu.roll(xc, NSA_HEAD_DIM, axis=1), fill))
    return jnp.concatenate(out, axis=1)


def _values_t(v_t, ncols):
    return jnp.concatenate([v_t, jnp.ones((V_ROWS - NSA_HEAD_DIM, ncols), F32)], axis=0).astype(BF16)


def _store_values_t(v, o_ref):
    rows = v.shape[0]
    v_t = jnp.transpose(v)
    for g in range(v.shape[1] // NSA_HEAD_DIM):
        slab = _values_t(v_t[g * NSA_HEAD_DIM:(g + 1) * NSA_HEAD_DIM, :], rows)
        for c in range(rows // LANES):
            o_ref[0, g, c] = slab[:, c * LANES:(c + 1) * LANES]


def _rot_tables(pos, rot_dim, theta, scale):
    half = rot_dim // 2
    inv = theta ** (-np.arange(0, rot_dim, 2, dtype=np.float64) / rot_dim)
    ang = np.asarray(pos, np.float64)[:, None] * inv
    cos = np.ones((len(pos), NSA_HEAD_DIM))
    sgn = np.zeros((len(pos), NSA_HEAD_DIM))
    cos[:, :half] = np.cos(ang)
    cos[:, half:rot_dim] = np.cos(ang)
    sgn[:, :half] = -np.sin(ang)
    sgn[:, half:rot_dim] = np.sin(ang)
    cos = np.tile(cos * scale, (1, 2))
    sgn = np.tile(sgn * scale, (1, 2))
    return jnp.asarray(cos, F32), jnp.asarray(sgn, F32)


def _ret_decay_tables(chunk):
    log_g = np.log1p(-(2.0 ** (-5.0 - np.arange(RET_HEADS, dtype=np.float64))))
    ci = np.arange(chunk, dtype=np.float64)
    diff = ci[:, None] - ci[None, :]
    inner = np.where(diff >= 0, np.exp(np.maximum(diff, 0.0) * log_g[:, None, None]), 0.0)
    lane_head = np.repeat(np.arange(RET_HEADS), RET_DIM)
    qdec = np.exp((ci[:, None] + 1.0) * log_g[lane_head][None, :])
    kdec = np.exp((chunk - 1.0 - ci[:, None]) * log_g[lane_head][None, :])
    cdec = np.exp(chunk * log_g[lane_head])[None, :]
    return (jnp.asarray(inner, F32), jnp.asarray(qdec, F32), jnp.asarray(kdec, F32),
            jnp.asarray(cdec, F32))


def _proj_ab_kernel(x_ref, g_ref, w_ref, o_ref):
    xn = _rms(x_ref[...], g_ref[...]).astype(BF16)
    o_ref[...] = _dot(xn, w_ref[...])


def _proj_ab(x2, g, w):
    t, d = x2.shape
    n = w.shape[1]
    return pl.pallas_call(
        _proj_ab_kernel,
        grid=(t // TOK_TILE,),
        in_specs=[pl.BlockSpec((TOK_TILE, d), lambda i: (i, 0)),
                  pl.BlockSpec((1, d), lambda i: (0, 0)),
                  pl.BlockSpec((d, n), lambda i: (0, 0), pipeline_mode=pl.Buffered(1))],
        out_specs=pl.BlockSpec((TOK_TILE, n), lambda i: (i, 0)),
        out_shape=jax.ShapeDtypeStruct((t, n), F32),
        compiler_params=_cparams(("parallel",)),
        name="proj_ab",
    )(x2, g, w)


def _lru_kernel(y_ref, x_ref, cw_ref, cb_ref, gw_ref, gb_ref, lam_ref, o_ref,
                xp_ref, a_ref, u_ref, h_ref, carry_ref):
    ts, w = x_ref.shape
    hist = SUBLANES

    @pl.when(pl.program_id(1) == 0)
    def _():
        xp_ref[0:hist, :] = jnp.zeros((hist, w), F32)
        carry_ref[...] = jnp.zeros_like(carry_ref)

    xp_ref[hist:hist + ts, :] = x_ref[...]
    xc = cb_ref[...]
    for k in range(CONV_WIDTH):
        off = hist - (CONV_WIDTH - 1) + k
        xc = xc + xp_ref[off:off + ts, :] * cw_ref[k:k + 1, :]
    xp_ref[0:hist, :] = x_ref[ts - hist:ts, :]

    gates = _dot(xc.astype(BF16), gw_ref[...]) + gb_ref[...]
    r = jax.nn.sigmoid(gates[:, :w])
    i = jax.nn.sigmoid(gates[:, w:])
    log_a = -LRU_C * r * jax.nn.softplus(-lam_ref[...])
    a = jnp.exp(log_a)
    a_ref[...] = a
    u_ref[...] = jnp.sqrt(jnp.tanh(-log_a) * (a * a + 1.0)) * (i * xc)

    row = lax.broadcasted_iota(jnp.int32, (SUBLANES, w), 0)

    def body(gi, carry):
        r0 = pl.multiple_of(gi * SUBLANES, SUBLANES)
        a = a_ref[pl.ds(r0, SUBLANES), :]
        u = u_ref[pl.ds(r0, SUBLANES), :]
        for d in (1, 2, 4):
            a_s = pltpu.roll(a, d, axis=0)
            u_s = pltpu.roll(u, d, axis=0)
            m = row >= d
            u = jnp.where(m, a * u_s + u, u)
            a = jnp.where(m, a * a_s, a)
        h = a * carry + u
        h_ref[pl.ds(r0, SUBLANES), :] = h
        return h[SUBLANES - 1:SUBLANES, :]

    carry_ref[...] = lax.fori_loop(0, ts // SUBLANES, body, carry_ref[...])
    o_ref[...] = (h_ref[...] * jax.nn.gelu(y_ref[...])).astype(o_ref.dtype)


def _lru(proj, batch, seq, conv_w, conv_b, gate_w, gate_b, lam):
    w = conv_w.shape[1]
    ns = seq // LRU_TILE
    return pl.pallas_call(
        _lru_kernel,
        grid=(batch, ns),
        in_specs=[pl.BlockSpec((LRU_TILE, w), lambda b, s: (b * ns + s, 0)),
                  pl.BlockSpec((LRU_TILE, w), lambda b, s: (b * ns + s, 1)),
                  pl.BlockSpec((CONV_WIDTH, w), lambda b, s: (0, 0)),
                  pl.BlockSpec((1, w), lambda b, s: (0, 0)),
                  pl.BlockSpec((w, 2 * w), lambda b, s: (0, 0)),
                  pl.BlockSpec((1, 2 * w), lambda b, s: (0, 0)),
                  pl.BlockSpec((1, w), lambda b, s: (0, 0))],
        out_specs=pl.BlockSpec((LRU_TILE, w), lambda b, s: (b * ns + s, 0)),
        out_shape=jax.ShapeDtypeStruct((batch * seq, w), BF16),
        scratch_shapes=[pltpu.VMEM((LRU_TILE + SUBLANES, w), F32),
                        pltpu.VMEM((LRU_TILE, w), F32),
                        pltpu.VMEM((LRU_TILE, w), F32),
                        pltpu.VMEM((LRU_TILE, w), F32),
                        pltpu.VMEM((1, w), F32)],
        compiler_params=_cparams(("parallel", "arbitrary")),
        name="rg_lru",
    )(proj, proj, conv_w, conv_b, gate_w, gate_b, lam)


def _split_bf16(x):
    hi = x.astype(BF16)
    lo = (x - hi.astype(F32)).astype(BF16)
    return hi, lo


def _ret_kernel(q_ref, k_ref, v_ref, g_ref, cq_ref, sq_ref, ck_ref, sk_ref,
                inner_ref, qdec_ref, kdec_ref, cdec_ref, avg_ref, o_ref, st_ref):
    c, w = q_ref.shape
    npair = w // LANES
    b = pl.program_id(1)

    @pl.when(pl.program_id(0) == 0)
    def _():
        st_ref[b] = jnp.zeros(st_ref.shape[1:], F32)

    half = RET_DIM // 2
    q = _rotate(q_ref[...], _tile_lanes(cq_ref[...], npair), _tile_lanes(sq_ref[...], npair), half)
    k = _rotate(k_ref[...], _tile_lanes(ck_ref[...], npair), _tile_lanes(sk_ref[...], npair), half)
    v = v_ref[...]
    qd = (q * qdec_ref[...]).astype(BF16)
    kd = k * kdec_ref[...]
    q = q.astype(BF16)
    kb = k.astype(BF16)
    vb = v.astype(BF16)

    lane = lax.broadcasted_iota(jnp.int32, (c, LANES), 1)
    lo = lane < RET_DIM
    rr = lax.broadcasted_iota(jnp.int32, (LANES, LANES), 0) < RET_DIM
    cc = lax.broadcasted_iota(jnp.int32, (LANES, LANES), 1) < RET_DIM
    same_head = rr == cc
    zero = jnp.zeros((c, LANES), BF16)

    outs = []
    for p in range(npair):
        sl = slice(p * LANES, (p + 1) * LANES)
        qp, kp, vp = q[:, sl], kb[:, sl], vb[:, sl]
        s_lo = _dot_nt(jnp.where(lo, qp, zero), kp) * inner_ref[2 * p]
        s_hi = _dot_nt(jnp.where(lo, zero, qp), kp) * inner_ref[2 * p + 1]
        o = jnp.where(lo, _dot(s_lo.astype(BF16), vp), _dot(s_hi.astype(BF16), vp))
        st = st_ref[b, p]
        o = o + _dot(qd[:, sl], st.astype(BF16))
        kv = _dot(jnp.transpose(kd[:, sl]).astype(BF16), vp)
        st_ref[b, p] = st * cdec_ref[:, sl] + jnp.where(same_head, kv, 0.0)
        outs.append(o)

    o_all = jnp.concatenate(outs, axis=0)
    avg = avg_ref[...]
    o_hi, o_lo = _split_bf16(o_all)
    mu = _dot(o_hi, avg) + _dot(o_lo, avg)
    dlt = o_all - mu
    d_hi, d_lo = _split_bf16(dlt * dlt)
    var = _dot(d_hi, avg) + _dot(d_lo, avg)
    y = dlt * lax.rsqrt(var + EPS)
    y = jnp.concatenate([y[p * c:(p + 1) * c, :] for p in range(npair)], axis=1)
    o_ref[...] = (y * jax.nn.silu(g_ref[...])).astype(o_ref.dtype)


def _retention(proj, batch, seq):
    w = RET_HEADS * RET_DIM
    c = RET_CHUNK
    nc = seq // c
    npair = w // LANES
    pos = np.arange(seq)
    cq, sq = _rot_tables(pos, RET_DIM, RET_THETA, 1.0)
    ck, sk = _rot_tables(pos, RET_DIM, RET_THETA, RET_DIM ** -0.5)
    inner, qdec, kdec, cdec = _ret_decay_tables(c)
    blk = np.arange(LANES) // RET_DIM
    avg = jnp.asarray((blk[:, None] == blk[None, :]) / RET_DIM, BF16)
    col0 = 2
    tok = lambda j: pl.BlockSpec((c, w), lambda ci, b, j=j: (b * nc + ci, j))
    tab = pl.BlockSpec((c, LANES), lambda ci, b: (ci, 0))
    const = lambda shape: pl.BlockSpec(shape, lambda ci, b: (0,) * len(shape))
    return pl.pallas_call(
        _ret_kernel,
        grid=(nc, batch),
        in_specs=[tok(col0), tok(col0 + 1), tok(col0 + 2), tok(col0 + 3),
                  tab, tab, tab, tab,
                  const((RET_HEADS, c, c)), const((c, w)), const((c, w)), const((1, w)),
                  const((LANES, LANES))],
        out_specs=pl.BlockSpec((c, w), lambda ci, b: (b * nc + ci, 0)),
        out_shape=jax.ShapeDtypeStruct((batch * seq, w), BF16),
        scratch_shapes=[pltpu.VMEM((batch, npair, LANES, LANES), F32)],
        compiler_params=_cparams(("arbitrary", "arbitrary")),
        name="retention",
    )(proj, proj, proj, proj, cq, sq, ck, sk, inner, qdec, kdec, cdec, avg)


def _mix_ffn_kernel(*refs, n_mix, final):
    h_ref = refs[0]
    mix_refs = refs[1:1 + n_mix]
    wo_ref, g_ref, w1_ref, w3_ref, w2_ref = refs[1 + n_mix:6 + n_mix]
    rest = refs[6 + n_mix:]
    fg_ref = rest[0] if final else None
    o_ref = rest[-1]

    mixed = mix_refs[0][...] if n_mix == 1 else jnp.concatenate([r[...] for r in mix_refs], axis=1)
    h1 = h_ref[...] + _dot(mixed, wo_ref[...])
    xn = _rms(h1, g_ref[...]).astype(BF16)
    act = (jax.nn.silu(_dot(xn, w1_ref[...])) * _dot(xn, w3_ref[...])).astype(BF16)
    h2 = h1 + _dot(act, w2_ref[...])
    if final:
        h2 = _rms(h2, fg_ref[...])
    o_ref[...] = h2


def _mix_ffn(h2d, mixes, wo, g, w1, w3, w2, final_g=None):
    t, d = h2d.shape
    hid = w1.shape[1]
    final = final_g is not None
    row = lambda width: pl.BlockSpec((TOK_TILE, width), lambda i: (i, 0))
    const = lambda shape: pl.BlockSpec(shape, lambda i: (0, 0), pipeline_mode=pl.Buffered(1))
    in_specs = [row(d)] + [row(m.shape[1]) for m in mixes] + [
        const(wo.shape), const((1, d)), const((d, hid)), const((d, hid)), const((hid, d))]
    args = [h2d, *mixes, wo, g, w1, w3, w2]
    if final:
        in_specs.append(const((1, d)))
        args.append(final_g)
    return pl.pallas_call(
        functools.partial(_mix_ffn_kernel, n_mix=len(mixes), final=final),
        grid=(t // TOK_TILE,),
        in_specs=in_specs,
        out_specs=row(d),
        out_shape=jax.ShapeDtypeStruct((t, d), F32),
        compiler_params=_cparams(("parallel",)),
        name="mix_ffn_final" if final else "mix_ffn",
    )(*args)


def _proj_nsa_kernel(x_ref, g_ref, w_ref, cq_ref, sq_ref, ck_ref, sk_ref,
                     q_ref, cmp_ref, ks_ref, vs_ref, kw_ref, vw_ref, gt_ref, *, seq_tiles):
    hd = q_ref.shape[1]
    kvw = cmp_ref.shape[1] // 2
    xn = _rms(x_ref[...], g_ref[...]).astype(BF16)
    proj = _dot(xn, w_ref[...])
    half = ROT_DIM // 2
    q = proj[:, :hd]
    q_ref[...] = _rotate(q, _tile_lanes(cq_ref[...], hd // LANES),
                         _tile_lanes(sq_ref[...], hd // LANES), half).astype(q_ref.dtype)
    kv = proj[:, hd:hd + 6 * kvw]
    cmp_ref[...] = kv[:, :2 * kvw]
    ck = _tile_lanes(ck_ref[...], kvw // LANES)
    sk = _tile_lanes(sk_ref[...], kvw // LANES)
    rows = x_ref.shape[0]
    tpos = (pl.program_id(0) % seq_tiles) * rows + lax.broadcasted_iota(jnp.int32, (rows, LANES), 0)
    lane = lax.broadcasted_iota(jnp.int32, (rows, LANES), 1)
    block_onehot = jnp.where(lane - NSA_HEAD_DIM == tpos // SLC_LEN, 1.0, 0.0)
    ks_ref[...] = _group_slabs(_rotate(kv[:, 2 * kvw:3 * kvw], ck, sk, half), block_onehot).astype(ks_ref.dtype)
    one_lane = jnp.where(lane == NSA_HEAD_DIM, 1.0, 0.0)
    kw_ref[...] = _group_slabs(_rotate(kv[:, 4 * kvw:5 * kvw], ck, sk, half), one_lane).astype(kw_ref.dtype)
    _store_values_t(kv[:, 3 * kvw:4 * kvw], vs_ref)
    _store_values_t(kv[:, 5 * kvw:6 * kvw], vw_ref)
    gt_ref[...] = jax.nn.sigmoid(proj[:, hd + 6 * kvw:])


def _proj_nsa(x2, g, w, seq):
    t, d = x2.shape
    hd = NSA_HEADS * NSA_HEAD_DIM
    kvw = NSA_KV_GROUPS * NSA_HEAD_DIM
    gw = NSA_KV_GROUPS * LANES
    n = w.shape[1]
    ns = seq // TOK_TILE
    pos = np.arange(seq)
    cq, sq = _rot_tables(pos, ROT_DIM, ROPE_THETA, NSA_HEAD_DIM ** -0.5 * LOG2E)
    ck, sk = _rot_tables(pos, ROT_DIM, ROPE_THETA, 1.0)
    row = lambda width: pl.BlockSpec((TOK_TILE, width), lambda i: (i, 0))
    tab = pl.BlockSpec((TOK_TILE, LANES), lambda i: (i % ns, 0))
    chunks = TOK_TILE // LANES
    val_t = (pl.BlockSpec((1, NSA_KV_GROUPS, chunks, V_ROWS, LANES), lambda i: (i // ns, 0, i % ns, 0, 0)),
             jax.ShapeDtypeStruct((t // seq, NSA_KV_GROUPS, seq // LANES, V_ROWS, LANES), BF16))
    tok = lambda wd, dt: (row(wd), jax.ShapeDtypeStruct((t, wd), dt))
    outs = [tok(hd, F32), tok(2 * kvw, F32), tok(gw, BF16), val_t, tok(gw, BF16), val_t, tok(gw, F32)]
    return pl.pallas_call(
        functools.partial(_proj_nsa_kernel, seq_tiles=ns),
        grid=(t // TOK_TILE,),
        in_specs=[row(d), pl.BlockSpec((1, d), lambda i: (0, 0)),
                  pl.BlockSpec((d, n), lambda i: (0, 0), pipeline_mode=pl.Buffered(1)),
                  tab, tab, tab, tab],
        out_specs=[spec for spec, _ in outs],
        out_shape=[shape for _, shape in outs],
        compiler_params=_cparams(("parallel",)),
        name="proj_nsa",
    )(x2, g, w, cq, sq, ck, sk)


def _compress_kernel(x_ref, pe_ref, w1a_ref, w1b_ref, w2k_ref, w2vt_ref, cos_ref, sgn_ref, kc_ref, vct_ref):
    nrow = x_ref.shape[3]

    def hidden(j):
        x = x_ref[0, j, 0]
        u = _dot((x + pe_ref[j, 0:1, :]).astype(BF16), w1a_ref[j])
        v = _dot((x + pe_ref[j, 1:2, :]).astype(BF16), w1b_ref[j])
        return jax.nn.gelu(u + pltpu.roll(v, nrow - 1, axis=0)).astype(BF16)

    kc = _dot(hidden(0), w2k_ref[...])
    kc_ref[0, 0] = _rotate(kc, cos_ref[...], sgn_ref[...], ROT_DIM // 2).astype(kc_ref.dtype)
    vct_ref[0, 0] = _values_t(_dot_nt(w2vt_ref[...], hidden(1)), nrow)


def _compress(cmp_in, pe, w1a, w1b, w2k, w2v_t, cos, sgn):
    b, _, g, n, feat = cmp_in.shape
    full = lambda a: pl.BlockSpec(a.shape, lambda bi, gi: (0,) * a.ndim)
    return pl.pallas_call(
        _compress_kernel,
        grid=(b, g),
        in_specs=[pl.BlockSpec((1, 2, 1, n, feat), lambda bi, gi: (bi, 0, gi, 0, 0)),
                  full(pe), full(w1a), full(w1b), full(w2k), full(w2v_t), full(cos), full(sgn)],
        out_specs=[pl.BlockSpec((1, 1, n, LANES), lambda bi, gi: (bi, gi, 0, 0)),
                   pl.BlockSpec((1, 1, V_ROWS, n), lambda bi, gi: (bi, gi, 0, 0))],
        out_shape=[jax.ShapeDtypeStruct((b, g, n, LANES), BF16),
                   jax.ShapeDtypeStruct((b, g, V_ROWS, n), BF16)],
        compiler_params=_cparams(("parallel", "arbitrary")),
        name="compress",
    )(cmp_in, pe, w1a, w1b, w2k, w2v_t, cos, sgn)


def _topk_bias_t(imp_t, tpos, last_block, val_ref, cnt_ref):
    nb, tq = imp_t.shape
    j = lax.broadcasted_iota(jnp.int32, (nb, tq), 0)
    cur = tpos // SLC_LEN
    forced = (j == 0) | (j == cur) | (j == cur - 1)
    causal = j <= cur
    val_ref[...] = jnp.where(forced, BIG, jnp.where(causal, imp_t, NEG))
    cnt_ref[...] = jnp.zeros((nb, tq), F32)
    nchunk = nb // SUBLANES
    rows = lax.broadcasted_iota(jnp.int32, (SUBLANES, tq), 0)
    for c in range(nchunk):
        for ic in range(nchunk):
            @pl.when(max(c, ic) * SUBLANES <= last_block)
            def _():
                vc = val_ref[c * SUBLANES:(c + 1) * SUBLANES, :]
                acc = cnt_ref[c * SUBLANES:(c + 1) * SUBLANES, :]
                for ii in range(SUBLANES):
                    vi = val_ref[ic * SUBLANES + ii:ic * SUBLANES + ii + 1, :]
                    if c > ic:
                        ahead = jnp.where(vi >= vc, 1.0, 0.0)
                    elif c < ic:
                        ahead = jnp.where(vi > vc, 1.0, 0.0)
                    else:
                        ahead = jnp.where(rows > ii, jnp.where(vi >= vc, 1.0, 0.0),
                                          jnp.where(vi > vc, 1.0, 0.0))
                    acc = acc + ahead
                cnt_ref[c * SUBLANES:(c + 1) * SUBLANES, :] = acc
    return jnp.where((cnt_ref[...] < float(SLC_TOPK)) & causal, 0.0, NEG)


def _value_chunks_t(vt_ref, g, first, count):
    return jnp.concatenate([vt_ref[0, g, first + i] for i in range(count)], axis=1)


def _max_key_norms(k_ref, chunk):
    seq, width = k_ref.shape
    row = lax.broadcasted_iota(jnp.int32, (width, LANES), 0)
    lane = lax.broadcasted_iota(jnp.int32, (width, LANES), 1)
    dims_of_group = jnp.where((row // LANES == lane) & (row % LANES < NSA_HEAD_DIM), 1.0, 0.0).astype(BF16)
    best = jnp.zeros((1, LANES), F32)
    for i in range(seq // chunk):
        k = k_ref[i * chunk:(i + 1) * chunk, :].astype(F32)
        sq_hi, sq_lo = _split_bf16(k * k)
        norm2 = _dot(sq_hi, dims_of_group) + _dot(sq_lo, dims_of_group)
        best = jnp.maximum(best, jnp.max(norm2, axis=0, keepdims=True))
    return jnp.sqrt(best) * NORM_MARGIN


def _nsa_attn_kernel(q_ref, gt_ref, kc_ref, vct_ref, ks_ref, vst_ref, kw_ref, vwt_ref, c2s_ref, o_ref,
                     val_ref, cnt_ref, acc_ref, kmax_ref, *, n_slc, tk):
    tq = q_ref.shape[0]
    ncmp = kc_ref.shape[2]
    ngrp = NSA_KV_GROUPS
    nrep = NSA_HEADS // ngrp
    dh = NSA_HEAD_DIM
    groups = range(ngrp)
    t0 = pl.program_id(1) * tq
    head = lambda x, r: x[:, r * tq:(r + 1) * tq]
    grp_lanes = lambda g: slice(g * LANES, (g + 1) * LANES)

    @pl.when(pl.program_id(1) == 0)
    def _():
        kmax_ref[0:1, :] = _max_key_norms(ks_ref, tk)
        kmax_ref[1:2, :] = _max_key_norms(kw_ref, tk)

    lane = lax.broadcasted_iota(jnp.int32, (tq, LANES), 1)
    tpos = t0 + lax.broadcasted_iota(jnp.int32, (1, tq), 1)
    q_t = []
    for g in groups:
        cols = []
        for p in range(nrep // 2):
            qp = q_ref[:, (g * nrep // 2 + p) * LANES:(g * nrep // 2 + p + 1) * LANES]
            cols += [jnp.transpose(jnp.where(lane < dh, qh, 0.0)) for qh in (qp, pltpu.roll(qp, dh, axis=1))]
        q_t.append(jnp.concatenate(cols, axis=1))

    cmp_end = lax.broadcasted_iota(jnp.int32, (ncmp, tq), 0) * CMP_STRIDE + (CMP_LEN - 1)
    vis_bias = jnp.where(cmp_end <= tpos, 0.0, NEG)
    any_vis = jnp.where(tpos >= CMP_LEN - 1, 1.0, 0.0)
    c2s = c2s_ref[...]
    o_cmp, imp_t = [], []
    for g in groups:
        s = _dot(kc_ref[0, g], q_t[g].astype(BF16))
        p_sum = jnp.zeros((ncmp, tq), F32)
        p_cmp = []
        for r in range(nrep):
            sr = head(s, r) + vis_bias
            e = jnp.exp2(sr - jnp.max(sr, axis=0, keepdims=True))
            pr = e * (any_vis / jnp.sum(e, axis=0, keepdims=True))
            p_sum = p_sum + pr
            p_cmp.append(pr.astype(BF16))
        o_cmp.append(_dot(vct_ref[0, g], jnp.concatenate(p_cmp, axis=1)))
        p_hi, p_lo = _split_bf16(p_sum)
        imp_t.append((_dot(c2s, p_hi) + _dot(c2s, p_lo))[:n_slc, :])

    bias_t = _topk_bias_t(jnp.concatenate(imp_t, axis=1), jnp.concatenate([tpos] * ngrp, axis=1),
                          (t0 + tq - 1) // SLC_LEN, val_ref, cnt_ref)
    if n_slc < dh:
        bias_t = jnp.concatenate([bias_t, jnp.full((dh - n_slc, ngrp * tq), NEG, F32)], axis=0)
    bias_rows = [jnp.concatenate([head(bias_t, g)] * nrep, axis=1) for g in groups]

    q_norm = [jnp.sqrt(jnp.sum(q_t[g] * q_t[g], axis=0, keepdims=True)) for g in groups]
    shift_s = [q_norm[g] * kmax_ref[0:1, g:g + 1] for g in groups]
    shift_w = [q_norm[g] * kmax_ref[1:2, g:g + 1] for g in groups]
    worst = functools.reduce(jnp.maximum, shift_s + shift_w)
    fixed_shift_ok = jnp.max(worst) < MAX_FIXED_SHIFT

    last = (t0 + tq - 1) // tk
    k_last = last * tk + lax.broadcasted_iota(jnp.int32, (tk, tq), 0)
    causal_bias = jnp.where(k_last <= tpos, 0.0, NEG)
    span = WINDOW + tq
    start = pl.multiple_of(jnp.maximum(t0 - WINDOW, 0), tq)
    dist = tpos - (start + lax.broadcasted_iota(jnp.int32, (span, tq), 0))
    band_bias = jnp.where((dist >= 0) & (dist < WINDOW), 0.0, NEG)
    key_tile = lambda c, g: ks_ref[pl.ds(pl.multiple_of(c * tk, tk), tk), grp_lanes(g)]
    val_tile = lambda c, g: _value_chunks_t(vst_ref, g, c * (tk // LANES), tk // LANES)
    win_keys = lambda g: kw_ref[pl.ds(start, span), grp_lanes(g)]
    win_vals = lambda g: _value_chunks_t(vwt_ref, g, start // LANES, span // LANES)
    all_heads = lambda x: jnp.concatenate([x] * nrep, axis=1)

    @pl.when(fixed_shift_ok)
    def _():
        qt_sel = [jnp.concatenate([q_t[g][:dh, :], bias_rows[g] - shift_s[g]], axis=0).astype(BF16) for g in groups]

        def body(c, accs):
            return tuple(accs[g] + _dot(val_tile(c, g), jnp.exp2(_dot(key_tile(c, g), qt_sel[g])).astype(BF16))
                         for g in groups)

        accs = lax.fori_loop(0, last, body, tuple(jnp.zeros((V_ROWS, nrep * tq), F32) for _ in groups))
        zeros_hi = jnp.zeros((dh - 1, nrep * tq), F32)
        for g in groups:
            p = jnp.exp2(_dot(key_tile(last, g), qt_sel[g]) + all_heads(causal_bias)).astype(BF16)
            acc_ref[0, g] = accs[g] + _dot(val_tile(last, g), p)
            qt_win = jnp.concatenate([q_t[g][:dh, :], -shift_w[g], zeros_hi], axis=0).astype(BF16)
            p = jnp.exp2(_dot(win_keys(g), qt_win) + all_heads(band_bias)).astype(BF16)
            acc_ref[1, g] = _dot(win_vals(g), p)

    @pl.when(jnp.logical_not(fixed_shift_ok))
    def _():
        qt_sel = [jnp.concatenate([q_t[g][:dh, :], bias_rows[g]], axis=0).astype(BF16) for g in groups]

        def tile(c, carry, token_bias):
            out = []
            for g in groups:
                m, acc = carry[g]
                sc = _dot(key_tile(c, g), qt_sel[g])
                ps, ms, alphas = [], [], []
                for r in range(nrep):
                    sr = head(sc, r)
                    if token_bias is not None:
                        sr = sr + token_bias
                    m_new = jnp.maximum(head(m, r), jnp.max(sr, axis=0, keepdims=True))
                    alphas.append(jnp.exp2(head(m, r) - m_new))
                    ms.append(m_new)
                    ps.append(jnp.exp2(sr - m_new).astype(BF16))
                pv = _dot(val_tile(c, g), jnp.concatenate(ps, axis=1))
                out.append((jnp.concatenate(ms, axis=1), jnp.concatenate(alphas, axis=1) * acc + pv))
            return tuple(out)

        init = tuple((jnp.full((1, nrep * tq), NEG, F32), jnp.zeros((V_ROWS, nrep * tq), F32)) for _ in groups)
        carry = lax.fori_loop(0, last, lambda c, cr: tile(c, cr, None), init)
        carry = tile(last, carry, causal_bias)
        for g in groups:
            acc_ref[0, g] = carry[g][1]
            sw = _dot(win_keys(g), q_t[g].astype(BF16))
            p_win = []
            for r in range(nrep):
                sr = head(sw, r) + band_bias
                p_win.append(jnp.exp2(sr - jnp.max(sr, axis=0, keepdims=True)).astype(BF16))
            acc_ref[1, g] = _dot(win_vals(g), jnp.concatenate(p_win, axis=1))

    gt_t = jnp.transpose(gt_ref[...])
    heads = []
    for g in groups:
        acc_s = acc_ref[0, g]
        acc_w = acc_ref[1, g]
        o_slc = acc_s[:dh, :] * (1.0 / acc_s[dh:dh + 1, :])
        o_win = acc_w[:dh, :] * (1.0 / acc_w[dh:dh + 1, :])
        for r in range(nrep):
            row = g * LANES + 3 * r
            heads.append(gt_t[row:row + 1, :] * head(o_cmp[g], r)[:dh, :]
                         + gt_t[row + 1:row + 2, :] * head(o_slc, r) + gt_t[row + 2:row + 3, :] * head(o_win, r))
    o_ref[...] = jnp.transpose(jnp.concatenate(heads, axis=0)).astype(o_ref.dtype)


def _nsa_attention(q, gt, kc, vc_t, ks, vs_t, kw, vw_t, batch, seq):
    g = NSA_KV_GROUPS
    nq = seq // ATT_TQ
    qw = (NSA_HEADS // g) * NSA_HEAD_DIM
    n_slc = seq // SLC_LEN
    n_cmp = (seq - CMP_LEN) // CMP_STRIDE + 1
    n_cmp_pad = kc.shape[2]
    c_start = np.arange(n_cmp)[:, None] * CMP_STRIDE
    s_start = np.arange(n_slc)[None, :] * SLC_LEN
    overlap = np.clip(np.minimum(c_start + CMP_LEN, s_start + SLC_LEN)
                      - np.maximum(c_start, s_start), 0, None) / CMP_LEN
    assert n_slc <= NSA_HEAD_DIM and n_slc % SUBLANES == 0 and seq % ATT_TK == 0
    c2s_t = np.zeros((NSA_HEAD_DIM, n_cmp_pad))
    c2s_t[:n_slc, :n_cmp] = overlap.T
    hd = NSA_HEADS * NSA_HEAD_DIM
    tok_spec = lambda width: pl.BlockSpec((ATT_TQ, width), lambda b, qi: (b * nq + qi, 0))
    key_spec = pl.BlockSpec((seq, g * LANES), lambda b, qi: (b, 0))
    val_spec = pl.BlockSpec((1, g, seq // LANES, V_ROWS, LANES), lambda b, qi: (b, 0, 0, 0, 0))
    return pl.pallas_call(
        functools.partial(_nsa_attn_kernel, n_slc=n_slc, tk=ATT_TK),
        grid=(batch, nq),
        in_specs=[tok_spec(hd), tok_spec(g * LANES),
                  pl.BlockSpec((1, g, n_cmp_pad, LANES), lambda b, qi: (b, 0, 0, 0)),
                  pl.BlockSpec((1, g, V_ROWS, n_cmp_pad), lambda b, qi: (b, 0, 0, 0)),
                  key_spec, val_spec, key_spec, val_spec,
                  pl.BlockSpec(c2s_t.shape, lambda b, qi: (0, 0))],
        out_specs=tok_spec(hd),
        out_shape=jax.ShapeDtypeStruct((batch * seq, hd), BF16),
        scratch_shapes=[pltpu.VMEM((n_slc, g * ATT_TQ), F32), pltpu.VMEM((n_slc, g * ATT_TQ), F32),
                        pltpu.VMEM((2, g, V_ROWS, (NSA_HEADS // g) * ATT_TQ), F32),
                        pltpu.VMEM((2, LANES), F32)],
        compiler_params=_cparams(("parallel", "arbitrary")),
        name="nsa_attention",
    )(q, gt, kc, vc_t, ks, vs_t, kw, vw_t, jnp.asarray(c2s_t, BF16))


def _block_diag(w):
    nb, bi, bo = w.shape
    eye = jnp.eye(nb, dtype=w.dtype)
    return (eye[:, None, :, None] * w[:, :, None, :]).reshape(nb * bi, nb * bo)


def _nsa_weight(w_in):
    hd = NSA_HEADS * NSA_HEAD_DIM
    kvw = NSA_KV_GROUPS * NSA_HEAD_DIM
    nrep = NSA_HEADS // NSA_KV_GROUPS
    d = w_in.shape[0]
    gates = w_in[:, hd + 6 * kvw:].reshape(d, NSA_KV_GROUPS, 3 * nrep)
    gates = jnp.pad(gates, ((0, 0), (0, 0), (0, LANES - 3 * nrep))).reshape(d, NSA_KV_GROUPS * LANES)
    return jnp.concatenate([w_in[:, :hd + 6 * kvw], gates], axis=1).astype(BF16)


def _ab_mixer(h, g_attn, w_in, conv_w, conv_b, ga_w, ga_b, gx_w, gx_b, lam, batch, seq):
    proj = _proj_ab(h, g_attn, w_in.astype(BF16))
    gate_w = jnp.concatenate([_block_diag(ga_w), _block_diag(gx_w)], axis=1).astype(BF16)
    gate_b = jnp.concatenate([ga_b, gx_b])[None, :]
    lru_out = _lru(proj, batch, seq, conv_w, conv_b[None, :], gate_w, gate_b, lam[None, :])
    return [lru_out, _retention(proj, batch, seq)]


def _nsa_mixer(h, g_attn, w_in, pe_k, k_w1, k_w2, pe_v, v_w1, v_w2, batch, seq):
    q, cmp_in, ks, vs_t, kw, vw_t, gt = _proj_nsa(h, g_attn, _nsa_weight(w_in), seq)
    grp, dh = NSA_KV_GROUPS, NSA_HEAD_DIM
    half_feat = CMP_STRIDE * dh
    n_blk = seq // CMP_STRIDE
    cmp_in = cmp_in.reshape(batch, n_blk, CMP_STRIDE, 2, grp, dh)
    cmp_in = cmp_in.transpose(0, 3, 4, 1, 2, 5).reshape(batch, 2, grp, n_blk, half_feat)
    pe = jnp.stack([pe_k, pe_v]).reshape(2, 2, half_feat)
    w1 = jnp.stack([k_w1, v_w1]).astype(BF16)
    w2k = jnp.pad(k_w2, ((0, 0), (0, LANES - dh))).astype(BF16)
    w2v_t = v_w2.T.astype(BF16)
    cos, sgn = _rot_tables(np.arange(n_blk) * CMP_STRIDE + CMP_LEN - 1, ROT_DIM, ROPE_THETA, 1.0)
    kc, vc_t = _compress(cmp_in, pe, w1[:, :half_feat], w1[:, half_feat:], w2k, w2v_t, cos, sgn)
    return [_nsa_attention(q, gt, kc, vc_t, ks, vs_t, kw, vw_t, batch, seq)]


def kernel(x, attn_norm, ab_w_in, conv_w, conv_b, gate_a_w, gate_a_b, gate_x_w, gate_x_b, lru_lambda,
           ab_w_out, nsa_w_in, cmp_pe_k, cmp_k_w1, cmp_k_w2, cmp_pe_v, cmp_v_w1, cmp_v_w2, nsa_w_out,
           ffn_norm, ffn_w1, ffn_w3, ffn_w2, final_norm):
    batch, seq, d = x.shape
    depth = attn_norm.shape[0]
    h = x.reshape(batch * seq, d)
    for layer in range(depth):
        i = layer // 2
        g_attn = attn_norm[layer][None, :]
        if layer % 2 == 0:
            mixes = _ab_mixer(h, g_attn, ab_w_in[i], conv_w[i], conv_b[i], gate_a_w[i], gate_a_b[i],
                              gate_x_w[i], gate_x_b[i], lru_lambda[i], batch, seq)
            w_out = ab_w_out[i]
        else:
            mixes = _nsa_mixer(h, g_attn, nsa_w_in[i], cmp_pe_k[i], cmp_k_w1[i], cmp_k_w2[i],
                               cmp_pe_v[i], cmp_v_w1[i], cmp_v_w2[i], batch, seq)
            w_out = nsa_w_out[i]
        last = layer == depth - 1
        h = _mix_ffn(h, mixes, w_out.astype(BF16), ffn_norm[layer][None, :],
                     ffn_w1[layer].astype(BF16), ffn_w3[layer].astype(BF16), ffn_w2[layer].astype(BF16),
                     final_g=final_norm[None, :] if last else None)
    return h.reshape(batch, seq, d)
```

```python
import functools

import numpy as np
import jax
import jax.numpy as jnp
from jax import lax
from jax.experimental import pallas as pl
from jax.experimental.pallas import tpu as pltpu

F32 = jnp.float32
BF16 = jnp.bfloat16

EPS = 1e-6
NEG = -1e30
BIG = 1e30
LOG2E = float(np.log2(np.e))

LRU_BLOCKS = 8
CONV_WIDTH = 4
LRU_C = 8.0
RET_HEADS = 8
RET_DIM = 64
RET_THETA = 10000.0
NSA_HEADS = 16
NSA_KV_GROUPS = 4
NSA_HEAD_DIM = 64
CMP_LEN = 32
CMP_STRIDE = 16
SLC_LEN = 64
SLC_TOPK = 16
WINDOW = 512
ROPE_THETA = 500000.0
ROT_DIM = NSA_HEAD_DIM // 4

LANES = 128
SUBLANES = 8
VMEM_LIMIT = 56 * 1024 * 1024

RET_CHUNK = 256
TOK_TILE = 512
LRU_TILE = 512
ATT_TQ = 128
ATT_TK = 512
V_ROWS = NSA_HEAD_DIM + 16
MAX_FIXED_SHIFT = 32.0
NORM_MARGIN = 1.01


def _cparams(sem):
    return pltpu.CompilerParams(dimension_semantics=sem, vmem_limit_bytes=VMEM_LIMIT)


def _dot(a, b):
    return jnp.dot(a, b, preferred_element_type=F32)


def _dot_nt(a, b):
    return lax.dot_general(a, b, (((1,), (1,)), ((), ())), preferred_element_type=F32)


def _rms(x, g):
    return x * lax.rsqrt(jnp.mean(x * x, axis=-1, keepdims=True) + EPS) * g


def _tile_lanes(t, reps):
    return t if reps == 1 else jnp.concatenate([t] * reps, axis=1)


def _rotate(x, cos, sgn, half):
    n = x.shape[1]
    lane = lax.broadcasted_iota(jnp.int32, x.shape, 1)
    first = (lane % NSA_HEAD_DIM) < half
    partner = jnp.where(first, pltpu.roll(x, n - half, axis=1), pltpu.roll(x, half, axis=1))
    return x * cos + partner * sgn


def _group_slabs(x, fill):
    lane = lax.broadcasted_iota(jnp.int32, (x.shape[0], LANES), 1)
    lo = lane < NSA_HEAD_DIM
    out = []
    for c in range(x.shape[1] // LANES):
        xc = x[:, c * LANES:(c + 1) * LANES]
        out.append(jnp.where(lo, xc, fill))
        out.append(jnp.where(lo, pltpu.roll(xc, NSA_HEAD_DIM, axis=1), fill))
    return jnp.concatenate(out, axis=1)


def _values_t(v_t, ncols):
    return jnp.concatenate([v_t, jnp.ones((V_ROWS - NSA_HEAD_DIM, ncols), F32)], axis=0).astype(BF16)


def _store_values_t(v, o_ref):
    rows = v.shape[0]
    v_t = jnp.transpose(v)
    for g in range(v.shape[1] // NSA_HEAD_DIM):
        slab = _values_t(v_t[g * NSA_HEAD_DIM:(g + 1) * NSA_HEAD_DIM, :], rows)
        for c in range(rows // LANES):
            o_ref[0, g, c] = slab[:, c * LANES:(c + 1) * LANES]


def _rot_tables(pos, rot_dim, theta, scale):
    half = rot_dim // 2
    inv = theta ** (-np.arange(0, rot_dim, 2, dtype=np.float64) / rot_dim)
    ang = np.asarray(pos, np.float64)[:, None] * inv
    cos = np.ones((len(pos), NSA_HEAD_DIM))
    sgn = np.zeros((len(pos), NSA_HEAD_DIM))
    cos[:, :half] = np.cos(ang)
    cos[:, half:rot_dim] = np.cos(ang)
    sgn[:, :half] = -np.sin(ang)
    sgn[:, half:rot_dim] = np.sin(ang)
    cos = np.tile(cos * scale, (1, 2))
    sgn = np.tile(sgn * scale, (1, 2))
    return jnp.asarray(cos, F32), jnp.asarray(sgn, F32)


def _ret_decay_tables(chunk):
    log_g = np.log1p(-(2.0 ** (-5.0 - np.arange(RET_HEADS, dtype=np.float64))))
    ci = np.arange(chunk, dtype=np.float64)
    diff = ci[:, None] - ci[None, :]
    inner = np.where(diff >= 0, np.exp(np.maximum(diff, 0.0) * log_g[:, None, None]), 0.0)
    lane_head = np.repeat(np.arange(RET_HEADS), RET_DIM)
    qdec = np.exp((ci[:, None] + 1.0) * log_g[lane_head][None, :])
    kdec = np.exp((chunk - 1.0 - ci[:, None]) * log_g[lane_head][None, :])
    cdec = np.exp(chunk * log_g[lane_head])[None, :]
    return (jnp.asarray(inner, F32), jnp.asarray(qdec, F32), jnp.asarray(kdec, F32),
            jnp.asarray(cdec, F32))


def _proj_ab_kernel(x_ref, g_ref, w_ref, o_ref):
    xn = _rms(x_ref[...], g_ref[...]).astype(BF16)
    o_ref[...] = _dot(xn, w_ref[...])


def _proj_ab(x2, g, w):
    t, d = x2.shape
    n = w.shape[1]
    return pl.pallas_call(
        _proj_ab_kernel,
        grid=(t // TOK_TILE,),
        in_specs=[pl.BlockSpec((TOK_TILE, d), lambda i: (i, 0)),
                  pl.BlockSpec((1, d), lambda i: (0, 0)),
                  pl.BlockSpec((d, n), lambda i: (0, 0), pipeline_mode=pl.Buffered(1))],
        out_specs=pl.BlockSpec((TOK_TILE, n), lambda i: (i, 0)),
        out_shape=jax.ShapeDtypeStruct((t, n), F32),
        compiler_params=_cparams(("parallel",)),
        name="proj_ab",
    )(x2, g, w)


def _lru_kernel(y_ref, x_ref, cw_ref, cb_ref, gw_ref, gb_ref, lam_ref, o_ref,
                xp_ref, a_ref, u_ref, h_ref, carry_ref):
    ts, w = x_ref.shape
    hist = SUBLANES

    @pl.when(pl.program_id(1) == 0)
    def _():
        xp_ref[0:hist, :] = jnp.zeros((hist, w), F32)
        carry_ref[...] = jnp.zeros_like(carry_ref)

    xp_ref[hist:hist + ts, :] = x_ref[...]
    xc = cb_ref[...]
    for k in range(CONV_WIDTH):
        off = hist - (CONV_WIDTH - 1) + k
        xc = xc + xp_ref[off:off + ts, :] * cw_ref[k:k + 1, :]
    xp_ref[0:hist, :] = x_ref[ts - hist:ts, :]

    gates = _dot(xc.astype(BF16), gw_ref[...]) + gb_ref[...]
    r = jax.nn.sigmoid(gates[:, :w])
    i = jax.nn.sigmoid(gates[:, w:])
    log_a = -LRU_C * r * jax.nn.softplus(-lam_ref[...])
    a = jnp.exp(log_a)
    a_ref[...] = a
    u_ref[...] = jnp.sqrt(jnp.tanh(-log_a) * (a * a + 1.0)) * (i * xc)

    row = lax.broadcasted_iota(jnp.int32, (SUBLANES, w), 0)

    def body(gi, carry):
        r0 = pl.multiple_of(gi * SUBLANES, SUBLANES)
        a = a_ref[pl.ds(r0, SUBLANES), :]
        u = u_ref[pl.ds(r0, SUBLANES), :]
        for d in (1, 2, 4):
            a_s = pltpu.roll(a, d, axis=0)
            u_s = pltpu.roll(u, d, axis=0)
            m = row >= d
            u = jnp.where(m, a * u_s + u, u)
            a = jnp.where(m, a * a_s, a)
        h = a * carry + u
        h_ref[pl.ds(r0, SUBLANES), :] = h
        return h[SUBLANES - 1:SUBLANES, :]

    carry_ref[...] = lax.fori_loop(0, ts // SUBLANES, body, carry_ref[...])
    o_ref[...] = (h_ref[...] * jax.nn.gelu(y_ref[...])).astype(o_ref.dtype)


def _lru(proj, batch, seq, conv_w, conv_b, gate_w, gate_b, lam):
    w = conv_w.shape[1]
    ns = seq // LRU_TILE
    return pl.pallas_call(
        _lru_kernel,
        grid=(batch, ns),
        in_specs=[pl.BlockSpec((LRU_TILE, w), lambda b, s: (b * ns + s, 0)),
                  pl.BlockSpec((LRU_TILE, w), lambda b, s: (b * ns + s, 1)),
                  pl.BlockSpec((CONV_WIDTH, w), lambda b, s: (0, 0)),
                  pl.BlockSpec((1, w), lambda b, s: (0, 0)),
                  pl.BlockSpec((w, 2 * w), lambda b, s: (0, 0)),
                  pl.BlockSpec((1, 2 * w), lambda b, s: (0, 0)),
                  pl.BlockSpec((1, w), lambda b, s: (0, 0))],
        out_specs=pl.BlockSpec((LRU_TILE, w), lambda b, s: (b * ns + s, 0)),
        out_shape=jax.ShapeDtypeStruct((batch * seq, w), BF16),
        scratch_shapes=[pltpu.VMEM((LRU_TILE + SUBLANES, w), F32),
                        pltpu.VMEM((LRU_TILE, w), F32),
                        pltpu.VMEM((LRU_TILE, w), F32),
                        pltpu.VMEM((LRU_TILE, w), F32),
                        pltpu.VMEM((1, w), F32)],
        compiler_params=_cparams(("parallel", "arbitrary")),
        name="rg_lru",
    )(proj, proj, conv_w, conv_b, gate_w, gate_b, lam)


def _split_bf16(x):
    hi = x.astype(BF16)
    lo = (x - hi.astype(F32)).astype(BF16)
    return hi, lo


def _ret_kernel(q_ref, k_ref, v_ref, g_ref, cq_ref, sq_ref, ck_ref, sk_ref,
                inner_ref, qdec_ref, kdec_ref, cdec_ref, avg_ref, o_ref, st_ref):
    c, w = q_ref.shape
    npair = w // LANES
    b = pl.program_id(1)

    @pl.when(pl.program_id(0) == 0)
    def _():
        st_ref[b] = jnp.zeros(st_ref.shape[1:], F32)

    half = RET_DIM // 2
    q = _rotate(q_ref[...], _tile_lanes(cq_ref[...], npair), _tile_lanes(sq_ref[...], npair), half)
    k = _rotate(k_ref[...], _tile_lanes(ck_ref[...], npair), _tile_lanes(sk_ref[...], npair), half)
    v = v_ref[...]
    qd = (q * qdec_ref[...]).astype(BF16)
    kd = k * kdec_ref[...]
    q = q.astype(BF16)
    kb = k.astype(BF16)
    vb = v.astype(BF16)

    lane = lax.broadcasted_iota(jnp.int32, (c, LANES), 1)
    lo = lane < RET_DIM
    rr = lax.broadcasted_iota(jnp.int32, (LANES, LANES), 0) < RET_DIM
    cc = lax.broadcasted_iota(jnp.int32, (LANES, LANES), 1) < RET_DIM
    same_head = rr == cc
    zero = jnp.zeros((c, LANES), BF16)

    outs = []
    for p in range(npair):
        sl = slice(p * LANES, (p + 1) * LANES)
        qp, kp, vp = q[:, sl], kb[:, sl], vb[:, sl]
        s_lo = _dot_nt(jnp.where(lo, qp, zero), kp) * inner_ref[2 * p]
        s_hi = _dot_nt(jnp.where(lo, zero, qp), kp) * inner_ref[2 * p + 1]
        o = jnp.where(lo, _dot(s_lo.astype(BF16), vp), _dot(s_hi.astype(BF16), vp))
        st = st_ref[b, p]
        o = o + _dot(qd[:, sl], st.astype(BF16))
        kv = _dot(jnp.transpose(kd[:, sl]).astype(BF16), vp)
        st_ref[b, p] = st * cdec_ref[:, sl] + jnp.where(same_head, kv, 0.0)
        outs.append(o)

    o_all = jnp.concatenate(outs, axis=0)
    avg = avg_ref[...]
    o_hi, o_lo = _split_bf16(o_all)
    mu = _dot(o_hi, avg) + _dot(o_lo, avg)
    dlt = o_all - mu
    d_hi, d_lo = _split_bf16(dlt * dlt)
    var = _dot(d_hi, avg) + _dot(d_lo, avg)
    y = dlt * lax.rsqrt(var + EPS)
    y = jnp.concatenate([y[p * c:(p + 1) * c, :] for p in range(npair)], axis=1)
    o_ref[...] = (y * jax.nn.silu(g_ref[...])).astype(o_ref.dtype)


def _retention(proj, batch, seq):
    w = RET_HEADS * RET_DIM
    c = RET_CHUNK
    nc = seq // c
    npair = w // LANES
    pos = np.arange(seq)
    cq, sq = _rot_tables(pos, RET_DIM, RET_THETA, 1.0)
    ck, sk = _rot_tables(pos, RET_DIM, RET_THETA, RET_DIM ** -0.5)
    inner, qdec, kdec, cdec = _ret_decay_tables(c)
    blk = np.arange(LANES) // RET_DIM
    avg = jnp.asarray((blk[:, None] == blk[None, :]) / RET_DIM, BF16)
    col0 = 2
    tok = lambda j: pl.BlockSpec((c, w), lambda ci, b, j=j: (b * nc + ci, j))
    tab = pl.BlockSpec((c, LANES), lambda ci, b: (ci, 0))
    const = lambda shape: pl.BlockSpec(shape, lambda ci, b: (0,) * len(shape))
    return pl.pallas_call(
        _ret_kernel,
        grid=(nc, batch),
        in_specs=[tok(col0), tok(col0 + 1), tok(col0 + 2), tok(col0 + 3),
                  tab, tab, tab, tab,
                  const((RET_HEADS, c, c)), const((c, w)), const((c, w)), const((1, w)),
                  const((LANES, LANES))],
        out_specs=pl.BlockSpec((c, w), lambda ci, b: (b * nc + ci, 0)),
        out_shape=jax.ShapeDtypeStruct((batch * seq, w), BF16),
        scratch_shapes=[pltpu.VMEM((batch, npair, LANES, LANES), F32)],
        compiler_params=_cparams(("arbitrary", "arbitrary")),
        name="retention",
    )(proj, proj, proj, proj, cq, sq, ck, sk, inner, qdec, kdec, cdec, avg)


def _mix_ffn_kernel(*refs, n_mix, final):
    h_ref = refs[0]
    mix_refs = refs[1:1 + n_mix]
    wo_ref, g_ref, w1_ref, w3_ref, w2_ref = refs[1 + n_mix:6 + n_mix]
    rest = refs[6 + n_mix:]
    fg_ref = rest[0] if final else None
    o_ref = rest[-1]

    mixed = mix_refs[0][...] if n_mix == 1 else jnp.concatenate([r[...] for r in mix_refs], axis=1)
    h1 = h_ref[...] + _dot(mixed, wo_ref[...])
    xn = _rms(h1, g_ref[...]).astype(BF16)
    act = (jax.nn.silu(_dot(xn, w1_ref[...])) * _dot(xn, w3_ref[...])).astype(BF16)
    h2 = h1 + _dot(act, w2_ref[...])
    if final:
        h2 = _rms(h2, fg_ref[...])
    o_ref[...] = h2


def _mix_ffn(h2d, mixes, wo, g, w1, w3, w2, final_g=None):
    t, d = h2d.shape
    hid = w1.shape[1]
    final = final_g is not None
    row = lambda width: pl.BlockSpec((TOK_TILE, width), lambda i: (i, 0))
    const = lambda shape: pl.BlockSpec(shape, lambda i: (0, 0), pipeline_mode=pl.Buffered(1))
    in_specs = [row(d)] + [row(m.shape[1]) for m in mixes] + [
        const(wo.shape), const((1, d)), const((d, hid)), const((d, hid)), const((hid, d))]
    args = [h2d, *mixes, wo, g, w1, w3, w2]
    if final:
        in_specs.append(const((1, d)))
        args.append(final_g)
    return pl.pallas_call(
        functools.partial(_mix_ffn_kernel, n_mix=len(mixes), final=final),
        grid=(t // TOK_TILE,),
        in_specs=in_specs,
        out_specs=row(d),
        out_shape=jax.ShapeDtypeStruct((t, d), F32),
        compiler_params=_cparams(("parallel",)),
        name="mix_ffn_final" if final else "mix_ffn",
    )(*args)


def _proj_nsa_kernel(x_ref, g_ref, w_ref, cq_ref, sq_ref, ck_ref, sk_ref,
                     q_ref, cmp_ref, ks_ref, vs_ref, kw_ref, vw_ref, gt_ref, *, seq_tiles):
    hd = q_ref.shape[0]
    kvw = NSA_KV_GROUPS * NSA_HEAD_DIM
    xn = _rms(x_ref[...], g_ref[...]).astype(BF16)
    proj = _dot(xn, w_ref[...])
    half = ROT_DIM // 2
    q = proj[:, :hd]
    q_ref[...] = jnp.transpose(_rotate(q, _tile_lanes(cq_ref[...], hd // LANES),
                                       _tile_lanes(sq_ref[...], hd // LANES), half))
    kv = proj[:, hd:hd + 6 * kvw]
    for c in range(cmp_ref.shape[0]):
        cmp_ref[c] = kv[:, c * LANES:(c + 1) * LANES]
    ck = _tile_lanes(ck_ref[...], kvw // LANES)
    sk = _tile_lanes(sk_ref[...], kvw // LANES)
    rows = x_ref.shape[0]
    tpos = (pl.program_id(0) % seq_tiles) * rows + lax.broadcasted_iota(jnp.int32, (rows, LANES), 0)
    lane = lax.broadcasted_iota(jnp.int32, (rows, LANES), 1)
    block_onehot = jnp.where(lane - NSA_HEAD_DIM == tpos // SLC_LEN, 1.0, 0.0)
    ks_ref[...] = _group_slabs(_rotate(kv[:, 2 * kvw:3 * kvw], ck, sk, half), block_onehot).astype(ks_ref.dtype)
    one_lane = jnp.where(lane == NSA_HEAD_DIM, 1.0, 0.0)
    kw_ref[...] = _group_slabs(_rotate(kv[:, 4 * kvw:5 * kvw], ck, sk, half), one_lane).astype(kw_ref.dtype)
    _store_values_t(kv[:, 3 * kvw:4 * kvw], vs_ref)
    _store_values_t(kv[:, 5 * kvw:6 * kvw], vw_ref)
    gt_ref[...] = jnp.transpose(jax.nn.sigmoid(proj[:, hd + 6 * kvw:]))


def _proj_nsa(x2, g, w, seq):
    t, d = x2.shape
    hd = NSA_HEADS * NSA_HEAD_DIM
    kvw = NSA_KV_GROUPS * NSA_HEAD_DIM
    gw = NSA_KV_GROUPS * LANES
    n = w.shape[1]
    ns = seq // TOK_TILE
    pos = np.arange(seq)
    cq, sq = _rot_tables(pos, ROT_DIM, ROPE_THETA, NSA_HEAD_DIM ** -0.5 * LOG2E)
    ck, sk = _rot_tables(pos, ROT_DIM, ROPE_THETA, 1.0)
    row = lambda width: pl.BlockSpec((TOK_TILE, width), lambda i: (i, 0))
    tab = pl.BlockSpec((TOK_TILE, LANES), lambda i: (i % ns, 0))
    chunks = TOK_TILE // LANES
    val_t = (pl.BlockSpec((1, NSA_KV_GROUPS, chunks, V_ROWS, LANES), lambda i: (i // ns, 0, i % ns, 0, 0)),
             jax.ShapeDtypeStruct((t // seq, NSA_KV_GROUPS, seq // LANES, V_ROWS, LANES), BF16))
    tok = lambda wd, dt: (row(wd), jax.ShapeDtypeStruct((t, wd), dt))
    cmp_planes = (pl.BlockSpec((2 * kvw // LANES, TOK_TILE, LANES), lambda i: (0, i, 0)),
                  jax.ShapeDtypeStruct((2 * kvw // LANES, t, LANES), F32))
    tok_t = lambda wd: (pl.BlockSpec((wd, TOK_TILE), lambda i: (0, i)), jax.ShapeDtypeStruct((wd, t), F32))
    outs = [tok_t(hd), cmp_planes, tok(gw, BF16), val_t, tok(gw, BF16), val_t, tok_t(gw)]
    return pl.pallas_call(
        functools.partial(_proj_nsa_kernel, seq_tiles=ns),
        grid=(t // TOK_TILE,),
        in_specs=[row(d), pl.BlockSpec((1, d), lambda i: (0, 0)),
                  pl.BlockSpec((d, n), lambda i: (0, 0), pipeline_mode=pl.Buffered(1)),
                  tab, tab, tab, tab],
        out_specs=[spec for spec, _ in outs],
        out_shape=[shape for _, shape in outs],
        compiler_params=_cparams(("parallel",)),
        name="proj_nsa",
    )(x2, g, w, cq, sq, ck, sk)


def _compress_kernel(x_ref, pe_ref, w1_ref, wc_ref, w2k_ref, w2vt_ref, cos_ref, sgn_ref, kc_ref, vct_ref):
    n = kc_ref.shape[2]
    dh = NSA_HEAD_DIM
    for kv in range(2):
        pe_term = _dot(pe_ref[kv].astype(BF16), w1_ref[kv])[0:1, :]
        pe_term = jnp.concatenate([pe_term, pe_term], axis=1)
        for cp in range(NSA_KV_GROUPS // 2):
            col = kv * (NSA_KV_GROUPS // 2) + cp
            acc = jnp.zeros((n, 2 * LANES), F32)
            for j in range(CMP_STRIDE):
                rows = x_ref[col, pl.ds(j, n, stride=CMP_STRIDE), :]
                acc = acc + _dot(rows.astype(BF16), wc_ref[kv, j])
            hid = jax.nn.gelu(acc[:, :LANES] + pltpu.roll(acc[:, LANES:], n - 1, axis=0) + pe_term).astype(BF16)
            if kv == 0:
                kc = _dot(hid, w2k_ref[...])
                kc = _rotate(kc, _tile_lanes(cos_ref[...], 2), _tile_lanes(sgn_ref[...], 2), ROT_DIM // 2)
                kc_ref[0, 2 * cp] = kc[:, :LANES].astype(kc_ref.dtype)
                kc_ref[0, 2 * cp + 1] = kc[:, LANES:].astype(kc_ref.dtype)
            else:
                vc_t = _dot_nt(w2vt_ref[...], hid)
                vct_ref[0, 2 * cp] = _values_t(vc_t[:dh, :], n)
                vct_ref[0, 2 * cp + 1] = _values_t(vc_t[dh:, :], n)


def _compress(cmp_in, batch, seq, pe, w1, wc, w2k, w2v_t, cos, sgn):
    g = NSA_KV_GROUPS
    n = seq // CMP_STRIDE
    full = lambda a: pl.BlockSpec(a.shape, lambda bi: (0,) * a.ndim)
    return pl.pallas_call(
        _compress_kernel,
        grid=(batch,),
        in_specs=[pl.BlockSpec((cmp_in.shape[0], seq, LANES), lambda bi: (0, bi, 0)),
                  full(pe), full(w1), full(wc), full(w2k), full(w2v_t), full(cos), full(sgn)],
        out_specs=[pl.BlockSpec((1, g, n, LANES), lambda bi: (bi, 0, 0, 0)),
                   pl.BlockSpec((1, g, V_ROWS, n), lambda bi: (bi, 0, 0, 0))],
        out_shape=[jax.ShapeDtypeStruct((batch, g, n, LANES), BF16),
                   jax.ShapeDtypeStruct((batch, g, V_ROWS, n), BF16)],
        compiler_params=_cparams(("parallel",)),
        name="compress",
    )(cmp_in, pe, w1, wc, w2k, w2v_t, cos, sgn)


def _topk_bias_t(imp_t, tpos, last_block, val_ref, cnt_ref):
    nb, tq = imp_t.shape
    j = lax.broadcasted_iota(jnp.int32, (nb, tq), 0)
    cur = tpos // SLC_LEN
    forced = (j == 0) | (j == cur) | (j == cur - 1)
    causal = j <= cur
    val_ref[...] = jnp.where(forced, BIG, jnp.where(causal, imp_t, NEG))
    cnt_ref[...] = jnp.zeros((nb, tq), F32)
    nchunk = nb // SUBLANES
    rows = lax.broadcasted_iota(jnp.int32, (SUBLANES, tq), 0)
    for c in range(nchunk):
        for ic in range(nchunk):
            @pl.when(max(c, ic) * SUBLANES <= last_block)
            def _():
                vc = val_ref[c * SUBLANES:(c + 1) * SUBLANES, :]
                acc = cnt_ref[c * SUBLANES:(c + 1) * SUBLANES, :]
                for ii in range(SUBLANES):
                    vi = val_ref[ic * SUBLANES + ii:ic * SUBLANES + ii + 1, :]
                    if c > ic:
                        ahead = jnp.where(vi >= vc, 1.0, 0.0)
                    elif c < ic:
                        ahead = jnp.where(vi > vc, 1.0, 0.0)
                    else:
                        ahead = jnp.where(rows > ii, jnp.where(vi >= vc, 1.0, 0.0),
                                          jnp.where(vi > vc, 1.0, 0.0))
                    acc = acc + ahead
                cnt_ref[c * SUBLANES:(c + 1) * SUBLANES, :] = acc
    return jnp.where((cnt_ref[...] < float(SLC_TOPK)) & causal, 0.0, NEG)


def _value_chunks_t(vt_ref, g, first, count):
    return jnp.concatenate([vt_ref[0, g, first + i] for i in range(count)], axis=1)


def _max_key_norms(k_ref, chunk):
    seq, width = k_ref.shape
    row = lax.broadcasted_iota(jnp.int32, (width, LANES), 0)
    lane = lax.broadcasted_iota(jnp.int32, (width, LANES), 1)
    dims_of_group = jnp.where((row // LANES == lane) & (row % LANES < NSA_HEAD_DIM), 1.0, 0.0).astype(BF16)
    best = jnp.zeros((1, LANES), F32)
    for i in range(seq // chunk):
        k = k_ref[i * chunk:(i + 1) * chunk, :].astype(F32)
        sq_hi, sq_lo = _split_bf16(k * k)
        norm2 = _dot(sq_hi, dims_of_group) + _dot(sq_lo, dims_of_group)
        best = jnp.maximum(best, jnp.max(norm2, axis=0, keepdims=True))
    return jnp.sqrt(best) * NORM_MARGIN


def _nsa_attn_kernel(q_ref, gt_ref, kc_ref, vct_ref, ks_ref, vst_ref, kw_ref, vwt_ref, c2s_ref, o_ref,
                     val_ref, cnt_ref, acc_ref, kmax_ref, *, n_slc, tk):
    tq = q_ref.shape[1]
    ncmp = kc_ref.shape[2]
    ngrp = NSA_KV_GROUPS
    nrep = NSA_HEADS // ngrp
    dh = NSA_HEAD_DIM
    groups = range(ngrp)
    t0 = pl.program_id(1) * tq
    head = lambda x, r: x[:, r * tq:(r + 1) * tq]
    grp_lanes = lambda g: slice(g * LANES, (g + 1) * LANES)

    @pl.when(pl.program_id(1) == 0)
    def _():
        kmax_ref[0:1, :] = _max_key_norms(ks_ref, tk)
        kmax_ref[1:2, :] = _max_key_norms(kw_ref, tk)

    tpos = t0 + lax.broadcasted_iota(jnp.int32, (1, tq), 1)
    q_t = []
    for g in groups:
        cols = [q_ref[(g * nrep + r) * dh:(g * nrep + r + 1) * dh, :] for r in range(nrep)]
        q_t.append(jnp.concatenate([jnp.concatenate(cols, axis=1), jnp.zeros((LANES - dh, nrep * tq), F32)], axis=0))

    cmp_end = lax.broadcasted_iota(jnp.int32, (ncmp, tq), 0) * CMP_STRIDE + (CMP_LEN - 1)
    vis_bias = jnp.where(cmp_end <= tpos, 0.0, NEG)
    any_vis = jnp.where(tpos >= CMP_LEN - 1, 1.0, 0.0)
    c2s = c2s_ref[...]
    o_cmp, imp_t = [], []
    for g in groups:
        s = _dot(kc_ref[0, g], q_t[g].astype(BF16))
        p_sum = jnp.zeros((ncmp, tq), F32)
        p_cmp = []
        for r in range(nrep):
            sr = head(s, r) + vis_bias
            e = jnp.exp2(sr - jnp.max(sr, axis=0, keepdims=True))
            pr = e * (any_vis / jnp.sum(e, axis=0, keepdims=True))
            p_sum = p_sum + pr
            p_cmp.append(pr.astype(BF16))
        o_cmp.append(_dot(vct_ref[0, g], jnp.concatenate(p_cmp, axis=1)))
        p_hi, p_lo = _split_bf16(p_sum)
        imp_t.append((_dot(c2s, p_hi) + _dot(c2s, p_lo))[:n_slc, :])

    bias_t = _topk_bias_t(jnp.concatenate(imp_t, axis=1), jnp.concatenate([tpos] * ngrp, axis=1),
                          (t0 + tq - 1) // SLC_LEN, val_ref, cnt_ref)
    if n_slc < dh:
        bias_t = jnp.concatenate([bias_t, jnp.full((dh - n_slc, ngrp * tq), NEG, F32)], axis=0)
    bias_rows = [jnp.concatenate([head(bias_t, g)] * nrep, axis=1) for g in groups]

    q_norm = [jnp.sqrt(jnp.sum(q_t[g] * q_t[g], axis=0, keepdims=True)) for g in groups]
    shift_s = [q_norm[g] * kmax_ref[0:1, g:g + 1] for g in groups]
    shift_w = [q_norm[g] * kmax_ref[1:2, g:g + 1] for g in groups]
    worst = functools.reduce(jnp.maximum, shift_s + shift_w)
    fixed_shift_ok = jnp.max(worst) < MAX_FIXED_SHIFT

    last = (t0 + tq - 1) // tk
    k_last = last * tk + lax.broadcasted_iota(jnp.int32, (tk, tq), 0)
    causal_bias = jnp.where(k_last <= tpos, 0.0, NEG)
    span = WINDOW + tq
    start = pl.multiple_of(jnp.maximum(t0 - WINDOW, 0), tq)
    dist = tpos - (start + lax.broadcasted_iota(jnp.int32, (span, tq), 0))
    band_bias = jnp.where((dist >= 0) & (dist < WINDOW), 0.0, NEG)
    key_tile = lambda c, g: ks_ref[pl.ds(pl.multiple_of(c * tk, tk), tk), grp_lanes(g)]
    val_tile = lambda c, g: _value_chunks_t(vst_ref, g, c * (tk // LANES), tk // LANES)
    win_keys = lambda g: kw_ref[pl.ds(start, span), grp_lanes(g)]
    win_vals = lambda g: _value_chunks_t(vwt_ref, g, start // LANES, span // LANES)
    all_heads = lambda x: jnp.concatenate([x] * nrep, axis=1)

    @pl.when(fixed_shift_ok)
    def _():
        qt_sel = [jnp.concatenate([q_t[g][:dh, :], bias_rows[g] - shift_s[g]], axis=0).astype(BF16) for g in groups]

        def partial_pv(c, g, token_bias=None):
            sc = _dot(key_tile(c, g), qt_sel[g])
            if token_bias is not None:
                sc = sc + all_heads(token_bias)
            return _dot(val_tile(c, g), jnp.exp2(sc).astype(BF16))

        def body(i, accs):
            return tuple(accs[g] + partial_pv(2 * i, g) + partial_pv(2 * i + 1, g) for g in groups)

        accs = lax.fori_loop(0, last // 2, body, tuple(jnp.zeros((V_ROWS, nrep * tq), F32) for _ in groups))
        for g in groups:
            acc_ref[0, g] = accs[g] + partial_pv(last, g, causal_bias)

        @pl.when(last % 2 == 1)
        def _():
            for g in groups:
                acc_ref[0, g] += partial_pv(last - 1, g)

        zeros_hi = jnp.zeros((dh - 1, nrep * tq), F32)
        for g in groups:
            qt_win = jnp.concatenate([q_t[g][:dh, :], -shift_w[g], zeros_hi], axis=0).astype(BF16)
            p = jnp.exp2(_dot(win_keys(g), qt_win) + all_heads(band_bias)).astype(BF16)
            acc_ref[1, g] = _dot(win_vals(g), p)

    @pl.when(jnp.logical_not(fixed_shift_ok))
    def _():
        qt_sel = [jnp.concatenate([q_t[g][:dh, :], bias_rows[g]], axis=0).astype(BF16) for g in groups]

        def tile(c, carry, token_bias):
            out = []
            for g in groups:
                m, acc = carry[g]
                sc = _dot(key_tile(c, g), qt_sel[g])
                ps, ms, alphas = [], [], []
                for r in range(nrep):
                    sr = head(sc, r)
                    if token_bias is not None:
                        sr = sr + token_bias
                    m_new = jnp.maximum(head(m, r), jnp.max(sr, axis=0, keepdims=True))
                    alphas.append(jnp.exp2(head(m, r) - m_new))
                    ms.append(m_new)
                    ps.append(jnp.exp2(sr - m_new).astype(BF16))
                pv = _dot(val_tile(c, g), jnp.concatenate(ps, axis=1))
                out.append((jnp.concatenate(ms, axis=1), jnp.concatenate(alphas, axis=1) * acc + pv))
            return tuple(out)

        init = tuple((jnp.full((1, nrep * tq), NEG, F32), jnp.zeros((V_ROWS, nrep * tq), F32)) for _ in groups)
        carry = lax.fori_loop(0, last, lambda c, cr: tile(c, cr, None), init)
        carry = tile(last, carry, causal_bias)
        for g in groups:
            acc_ref[0, g] = carry[g][1]
            sw = _dot(win_keys(g), q_t[g].astype(BF16))
            p_win = []
            for r in range(nrep):
                sr = head(sw, r) + band_bias
                p_win.append(jnp.exp2(sr - jnp.max(sr, axis=0, keepdims=True)).astype(BF16))
            acc_ref[1, g] = _dot(win_vals(g), jnp.concatenate(p_win, axis=1))

    gt_t = gt_ref[...]
    heads = []
    for g in groups:
        acc_s = acc_ref[0, g]
        acc_w = acc_ref[1, g]
        o_slc = acc_s[:dh, :] * (1.0 / acc_s[dh:dh + 1, :])
        o_win = acc_w[:dh, :] * (1.0 / acc_w[dh:dh + 1, :])
        for r in range(nrep):
            row = g * LANES + 3 * r
            heads.append(gt_t[row:row + 1, :] * head(o_cmp[g], r)[:dh, :]
                         + gt_t[row + 1:row + 2, :] * head(o_slc, r) + gt_t[row + 2:row + 3, :] * head(o_win, r))
    o_ref[...] = jnp.transpose(jnp.concatenate(heads, axis=0)).astype(o_ref.dtype)


def _nsa_attention(q, gt, kc, vc_t, ks, vs_t, kw, vw_t, batch, seq):
    g = NSA_KV_GROUPS
    nq = seq // ATT_TQ
    qw = (NSA_HEADS // g) * NSA_HEAD_DIM
    n_slc = seq // SLC_LEN
    n_cmp = (seq - CMP_LEN) // CMP_STRIDE + 1
    n_cmp_pad = kc.shape[2]
    c_start = np.arange(n_cmp)[:, None] * CMP_STRIDE
    s_start = np.arange(n_slc)[None, :] * SLC_LEN
    overlap = np.clip(np.minimum(c_start + CMP_LEN, s_start + SLC_LEN)
                      - np.maximum(c_start, s_start), 0, None) / CMP_LEN
    assert n_slc <= NSA_HEAD_DIM and n_slc % SUBLANES == 0 and seq % ATT_TK == 0
    c2s_t = np.zeros((NSA_HEAD_DIM, n_cmp_pad))
    c2s_t[:n_slc, :n_cmp] = overlap.T
    hd = NSA_HEADS * NSA_HEAD_DIM
    tok_spec = lambda width: pl.BlockSpec((ATT_TQ, width), lambda b, qi: (b * nq + qi, 0))
    key_spec = pl.BlockSpec((seq, g * LANES), lambda b, qi: (b, 0))
    val_spec = pl.BlockSpec((1, g, seq // LANES, V_ROWS, LANES), lambda b, qi: (b, 0, 0, 0, 0))
    return pl.pallas_call(
        functools.partial(_nsa_attn_kernel, n_slc=n_slc, tk=ATT_TK),
        grid=(batch, nq),
        in_specs=[pl.BlockSpec((hd, ATT_TQ), lambda b, qi: (0, b * nq + qi)),
                  pl.BlockSpec((g * LANES, ATT_TQ), lambda b, qi: (0, b * nq + qi)),
                  pl.BlockSpec((1, g, n_cmp_pad, LANES), lambda b, qi: (b, 0, 0, 0)),
                  pl.BlockSpec((1, g, V_ROWS, n_cmp_pad), lambda b, qi: (b, 0, 0, 0)),
                  key_spec, val_spec, key_spec, val_spec,
                  pl.BlockSpec(c2s_t.shape, lambda b, qi: (0, 0))],
        out_specs=tok_spec(hd),
        out_shape=jax.ShapeDtypeStruct((batch * seq, hd), BF16),
        scratch_shapes=[pltpu.VMEM((n_slc, g * ATT_TQ), F32), pltpu.VMEM((n_slc, g * ATT_TQ), F32),
                        pltpu.VMEM((2, g, V_ROWS, (NSA_HEADS // g) * ATT_TQ), F32),
                        pltpu.VMEM((2, LANES), F32)],
        compiler_params=_cparams(("parallel", "arbitrary")),
        name="nsa_attention",
    )(q, gt, kc, vc_t, ks, vs_t, kw, vw_t, jnp.asarray(c2s_t, BF16))


def _block_diag(w):
    nb, bi, bo = w.shape
    eye = jnp.eye(nb, dtype=w.dtype)
    return (eye[:, None, :, None] * w[:, :, None, :]).reshape(nb * bi, nb * bo)


def _nsa_weight(w_in):
    hd = NSA_HEADS * NSA_HEAD_DIM
    kvw = NSA_KV_GROUPS * NSA_HEAD_DIM
    nrep = NSA_HEADS // NSA_KV_GROUPS
    d = w_in.shape[0]
    gates = w_in[:, hd + 6 * kvw:].reshape(d, NSA_KV_GROUPS, 3 * nrep)
    gates = jnp.pad(gates, ((0, 0), (0, 0), (0, LANES - 3 * nrep))).reshape(d, NSA_KV_GROUPS * LANES)
    return jnp.concatenate([w_in[:, :hd + 6 * kvw], gates], axis=1).astype(BF16)


def _ab_mixer(h, g_attn, w_in, conv_w, conv_b, ga_w, ga_b, gx_w, gx_b, lam, batch, seq):
    proj = _proj_ab(h, g_attn, w_in.astype(BF16))
    gate_w = jnp.concatenate([_block_diag(ga_w), _block_diag(gx_w)], axis=1).astype(BF16)
    gate_b = jnp.concatenate([ga_b, gx_b])[None, :]
    lru_out = _lru(proj, batch, seq, conv_w, conv_b[None, :], gate_w, gate_b, lam[None, :])
    return [lru_out, _retention(proj, batch, seq)]


def _nsa_mixer(h, g_attn, w_in, pe_k, k_w1, k_w2, pe_v, v_w1, v_w2, batch, seq):
    q, cmp_in, ks, vs_t, kw, vw_t, gt = _proj_nsa(h, g_attn, _nsa_weight(w_in), seq)
    dh = NSA_HEAD_DIM
    n_blk = seq // CMP_STRIDE
    pe = jnp.broadcast_to(jnp.stack([pe_k, pe_v]).reshape(2, 1, CMP_LEN * dh), (2, SUBLANES, CMP_LEN * dh))
    w1 = jnp.stack([k_w1, v_w1]).astype(BF16)
    w1r = w1.reshape(2, CMP_LEN, dh, dh)
    first, second, zero = w1r[:, :CMP_STRIDE], w1r[:, CMP_STRIDE:], jnp.zeros_like(w1r[:, :CMP_STRIDE])
    wc = jnp.concatenate([jnp.concatenate([first, zero, second, zero], axis=-1),
                          jnp.concatenate([zero, first, zero, second], axis=-1)], axis=-2)
    w2k = k_w2.astype(BF16)
    z = jnp.zeros_like(w2k)
    w2k = jnp.concatenate([jnp.concatenate([w2k, z, z, z], axis=1), jnp.concatenate([z, z, w2k, z], axis=1)], axis=0)
    w2v_t = v_w2.T.astype(BF16)
    zt = jnp.zeros_like(w2v_t)
    w2v_t = jnp.concatenate([jnp.concatenate([w2v_t, zt], axis=1), jnp.concatenate([zt, w2v_t], axis=1)], axis=0)
    cos, sgn = _rot_tables(np.arange(n_blk) * CMP_STRIDE + CMP_LEN - 1, ROT_DIM, ROPE_THETA, 1.0)
    kc, vc_t = _compress(cmp_in, batch, seq, pe, w1, wc, w2k, w2v_t, cos, sgn)
    return [_nsa_attention(q, gt, kc, vc_t, ks, vs_t, kw, vw_t, batch, seq)]


def kernel(x, attn_norm, ab_w_in, conv_w, conv_b, gate_a_w, gate_a_b, gate_x_w, gate_x_b, lru_lambda,
           ab_w_out, nsa_w_in, cmp_pe_k, cmp_k_w1, cmp_k_w2, cmp_pe_v, cmp_v_w1, cmp_v_w2, nsa_w_out,
           ffn_norm, ffn_w1, ffn_w3, ffn_w2, final_norm):
    batch, seq, d = x.shape
    depth = attn_norm.shape[0]
    h = x.reshape(batch * seq, d)
    for layer in range(depth):
        i = layer // 2
        g_attn = attn_norm[layer][None, :]
        if layer % 2 == 0:
            mixes = _ab_mixer(h, g_attn, ab_w_in[i], conv_w[i], conv_b[i], gate_a_w[i], gate_a_b[i],
                              gate_x_w[i], gate_x_b[i], lru_lambda[i], batch, seq)
            w_out = ab_w_out[i]
        else:
            mixes = _nsa_mixer(h, g_attn, nsa_w_in[i], cmp_pe_k[i], cmp_k_w1[i], cmp_k_w2[i],
                               cmp_pe_v[i], cmp_v_w1[i], cmp_v_w2[i], batch, seq)
            w_out = nsa_w_out[i]
        last = layer == depth - 1
        h = _mix_ffn(h, mixes, w_out.astype(BF16), ffn_norm[layer][None, :],
                     ffn_w1[layer].astype(BF16), ffn_w3[layer].astype(BF16), ffn_w2[layer].astype(BF16),
                     final_g=final_norm[None, :] if last else None)
    return h.reshape(batch, seq, d)
```

```python
import functools

import numpy as np
import jax
import jax.numpy as jnp
from jax import lax
from jax.experimental import pallas as pl
from jax.experimental.pallas import tpu as pltpu

F32 = jnp.float32
BF16 = jnp.bfloat16

EPS = 1e-6
NEG = -1e30
BIG = 1e30
LOG2E = float(np.log2(np.e))

LRU_BLOCKS = 8
CONV_WIDTH = 4
LRU_C = 8.0
RET_HEADS = 8
RET_DIM = 64
RET_THETA = 10000.0
NSA_HEADS = 16
NSA_KV_GROUPS = 4
NSA_HEAD_DIM = 64
CMP_LEN = 32
CMP_STRIDE = 16
SLC_LEN = 64
SLC_TOPK = 16
WINDOW = 512
ROPE_THETA = 500000.0
ROT_DIM = NSA_HEAD_DIM // 4

LANES = 128
SUBLANES = 8
VMEM_LIMIT = 56 * 1024 * 1024

RET_CHUNK = 256
TOK_TILE = 512
LRU_TILE = 512
ATT_TQ = 256
ATT_TK = 512
V_ROWS = NSA_HEAD_DIM + 16
MAX_FIXED_SHIFT = 32.0
NORM_MARGIN = 1.01


def _cparams(sem):
    return pltpu.CompilerParams(dimension_semantics=sem, vmem_limit_bytes=VMEM_LIMIT)


def _dot(a, b):
    return jnp.dot(a, b, preferred_element_type=F32)


def _dot_nt(a, b):
    return lax.dot_general(a, b, (((1,), (1,)), ((), ())), preferred_element_type=F32)


def _rms(x, g):
    return x * lax.rsqrt(jnp.mean(x * x, axis=-1, keepdims=True) + EPS) * g


def _tile_lanes(t, reps):
    return t if reps == 1 else jnp.concatenate([t] * reps, axis=1)


def _rotate(x, cos, sgn, half):
    n = x.shape[1]
    lane = lax.broadcasted_iota(jnp.int32, x.shape, 1)
    first = (lane % NSA_HEAD_DIM) < half
    partner = jnp.where(first, pltpu.roll(x, n - half, axis=1), pltpu.roll(x, half, axis=1))
    return x * cos + partner * sgn


def _group_slabs(x, fill):
    lane = lax.broadcasted_iota(jnp.int32, (x.shape[0], LANES), 1)
    lo = lane < NSA_HEAD_DIM
    out = []
    for c in range(x.shape[1] // LANES):
        xc = x[:, c * LANES:(c + 1) * LANES]
        out.append(jnp.where(lo, xc, fill))
        out.append(jnp.where(lo, pltpu.roll(xc, NSA_HEAD_DIM, axis=1), fill))
    return jnp.concatenate(out, axis=1)


def _values_t(v_t, ncols):
    return jnp.concatenate([v_t, jnp.ones((V_ROWS - NSA_HEAD_DIM, ncols), F32)], axis=0).astype(BF16)


def _store_values_t(v, o_ref):
    rows = v.shape[0]
    v_t = jnp.transpose(v)
    for g in range(v.shape[1] // NSA_HEAD_DIM):
        slab = _values_t(v_t[g * NSA_HEAD_DIM:(g + 1) * NSA_HEAD_DIM, :], rows)
        for c in range(rows // LANES):
            o_ref[0, g, c] = slab[:, c * LANES:(c + 1) * LANES]


def _rot_tables(pos, rot_dim, theta, scale):
    half = rot_dim // 2
    inv = theta ** (-np.arange(0, rot_dim, 2, dtype=np.float64) / rot_dim)
    ang = np.asarray(pos, np.float64)[:, None] * inv
    cos = np.ones((len(pos), NSA_HEAD_DIM))
    sgn = np.zeros((len(pos), NSA_HEAD_DIM))
    cos[:, :half] = np.cos(ang)
    cos[:, half:rot_dim] = np.cos(ang)
    sgn[:, :half] = -np.sin(ang)
    sgn[:, half:rot_dim] = np.sin(ang)
    cos = np.tile(cos * scale, (1, 2))
    sgn = np.tile(sgn * scale, (1, 2))
    return jnp.asarray(cos, F32), jnp.asarray(sgn, F32)


def _ret_decay_tables(chunk):
    log_g = np.log1p(-(2.0 ** (-5.0 - np.arange(RET_HEADS, dtype=np.float64))))
    ci = np.arange(chunk, dtype=np.float64)
    diff = ci[:, None] - ci[None, :]
    inner = np.where(diff >= 0, np.exp(np.maximum(diff, 0.0) * log_g[:, None, None]), 0.0)
    lane_head = np.repeat(np.arange(RET_HEADS), RET_DIM)
    qdec = np.exp((ci[:, None] + 1.0) * log_g[lane_head][None, :])
    kdec = np.exp((chunk - 1.0 - ci[:, None]) * log_g[lane_head][None, :])
    cdec = np.exp(chunk * log_g[lane_head])[None, :]
    return (jnp.asarray(inner, F32), jnp.asarray(qdec, F32), jnp.asarray(kdec, F32),
            jnp.asarray(cdec, F32))


def _proj_ab_kernel(x_ref, g_ref, w_ref, o_ref):
    xn = _rms(x_ref[...], g_ref[...]).astype(BF16)
    o_ref[...] = _dot(xn, w_ref[...])


def _proj_ab(x2, g, w):
    t, d = x2.shape
    n = w.shape[1]
    return pl.pallas_call(
        _proj_ab_kernel,
        grid=(t // TOK_TILE,),
        in_specs=[pl.BlockSpec((TOK_TILE, d), lambda i: (i, 0)),
                  pl.BlockSpec((1, d), lambda i: (0, 0)),
                  pl.BlockSpec((d, n), lambda i: (0, 0), pipeline_mode=pl.Buffered(1))],
        out_specs=pl.BlockSpec((TOK_TILE, n), lambda i: (i, 0)),
        out_shape=jax.ShapeDtypeStruct((t, n), F32),
        compiler_params=_cparams(("parallel",)),
        name="proj_ab",
    )(x2, g, w)


def _lru_kernel(y_ref, x_ref, cw_ref, cb_ref, gw_ref, gb_ref, lam_ref, o_ref,
                xp_ref, a_ref, u_ref, h_ref, carry_ref):
    ts, w = x_ref.shape
    hist = SUBLANES

    @pl.when(pl.program_id(1) == 0)
    def _():
        xp_ref[0:hist, :] = jnp.zeros((hist, w), F32)
        carry_ref[...] = jnp.zeros_like(carry_ref)

    xp_ref[hist:hist + ts, :] = x_ref[...]
    xc = cb_ref[...]
    for k in range(CONV_WIDTH):
        off = hist - (CONV_WIDTH - 1) + k
        xc = xc + xp_ref[off:off + ts, :] * cw_ref[k:k + 1, :]
    xp_ref[0:hist, :] = x_ref[ts - hist:ts, :]

    gates = _dot(xc.astype(BF16), gw_ref[...]) + gb_ref[...]
    r = jax.nn.sigmoid(gates[:, :w])
    i = jax.nn.sigmoid(gates[:, w:])
    log_a = -LRU_C * r * jax.nn.softplus(-lam_ref[...])
    a = jnp.exp(log_a)
    a_ref[...] = a
    u_ref[...] = jnp.sqrt(jnp.tanh(-log_a) * (a * a + 1.0)) * (i * xc)

    row = lax.broadcasted_iota(jnp.int32, (SUBLANES, w), 0)

    def body(gi, carry):
        r0 = pl.multiple_of(gi * SUBLANES, SUBLANES)
        a = a_ref[pl.ds(r0, SUBLANES), :]
        u = u_ref[pl.ds(r0, SUBLANES), :]
        for d in (1, 2, 4):
            a_s = pltpu.roll(a, d, axis=0)
            u_s = pltpu.roll(u, d, axis=0)
            m = row >= d
            u = jnp.where(m, a * u_s + u, u)
            a = jnp.where(m, a * a_s, a)
        h = a * carry + u
        h_ref[pl.ds(r0, SUBLANES), :] = h
        return h[SUBLANES - 1:SUBLANES, :]

    carry_ref[...] = lax.fori_loop(0, ts // SUBLANES, body, carry_ref[...])
    o_ref[...] = (h_ref[...] * jax.nn.gelu(y_ref[...])).astype(o_ref.dtype)


def _lru(proj, batch, seq, conv_w, conv_b, gate_w, gate_b, lam):
    w = conv_w.shape[1]
    ns = seq // LRU_TILE
    return pl.pallas_call(
        _lru_kernel,
        grid=(batch, ns),
        in_specs=[pl.BlockSpec((LRU_TILE, w), lambda b, s: (b * ns + s, 0)),
                  pl.BlockSpec((LRU_TILE, w), lambda b, s: (b * ns + s, 1)),
                  pl.BlockSpec((CONV_WIDTH, w), lambda b, s: (0, 0)),
                  pl.BlockSpec((1, w), lambda b, s: (0, 0)),
                  pl.BlockSpec((w, 2 * w), lambda b, s: (0, 0)),
                  pl.BlockSpec((1, 2 * w), lambda b, s: (0, 0)),
                  pl.BlockSpec((1, w), lambda b, s: (0, 0))],
        out_specs=pl.BlockSpec((LRU_TILE, w), lambda b, s: (b * ns + s, 0)),
        out_shape=jax.ShapeDtypeStruct((batch * seq, w), BF16),
        scratch_shapes=[pltpu.VMEM((LRU_TILE + SUBLANES, w), F32),
                        pltpu.VMEM((LRU_TILE, w), F32),
                        pltpu.VMEM((LRU_TILE, w), F32),
                        pltpu.VMEM((LRU_TILE, w), F32),
                        pltpu.VMEM((1, w), F32)],
        compiler_params=_cparams(("parallel", "arbitrary")),
        name="rg_lru",
    )(proj, proj, conv_w, conv_b, gate_w, gate_b, lam)


def _split_bf16(x):
    hi = x.astype(BF16)
    lo = (x - hi.astype(F32)).astype(BF16)
    return hi, lo


def _ret_kernel(q_ref, k_ref, v_ref, g_ref, cq_ref, sq_ref, ck_ref, sk_ref,
                inner_ref, qdec_ref, kdec_ref, cdec_ref, avg_ref, o_ref, st_ref):
    c, w = q_ref.shape
    npair = w // LANES
    b = pl.program_id(1)

    @pl.when(pl.program_id(0) == 0)
    def _():
        st_ref[b] = jnp.zeros(st_ref.shape[1:], F32)

    half = RET_DIM // 2
    q = _rotate(q_ref[...], _tile_lanes(cq_ref[...], npair), _tile_lanes(sq_ref[...], npair), half)
    k = _rotate(k_ref[...], _tile_lanes(ck_ref[...], npair), _tile_lanes(sk_ref[...], npair), half)
    v = v_ref[...]
    qd = (q * qdec_ref[...]).astype(BF16)
    kd = k * kdec_ref[...]
    q = q.astype(BF16)
    kb = k.astype(BF16)
    vb = v.astype(BF16)

    lane = lax.broadcasted_iota(jnp.int32, (c, LANES), 1)
    lo = lane < RET_DIM
    rr = lax.broadcasted_iota(jnp.int32, (LANES, LANES), 0) < RET_DIM
    cc = lax.broadcasted_iota(jnp.int32, (LANES, LANES), 1) < RET_DIM
    same_head = rr == cc
    zero = jnp.zeros((c, LANES), BF16)

    outs = []
    for p in range(npair):
        sl = slice(p * LANES, (p + 1) * LANES)
        qp, kp, vp = q[:, sl], kb[:, sl], vb[:, sl]
        s_lo = _dot_nt(jnp.where(lo, qp, zero), kp) * inner_ref[2 * p]
        s_hi = _dot_nt(jnp.where(lo, zero, qp), kp) * inner_ref[2 * p + 1]
        o = jnp.where(lo, _dot(s_lo.astype(BF16), vp), _dot(s_hi.astype(BF16), vp))
        st = st_ref[b, p]
        o = o + _dot(qd[:, sl], st.astype(BF16))
        kv = _dot(jnp.transpose(kd[:, sl]).astype(BF16), vp)
        st_ref[b, p] = st * cdec_ref[:, sl] + jnp.where(same_head, kv, 0.0)
        outs.append(o)

    o_all = jnp.concatenate(outs, axis=0)
    avg = avg_ref[...]
    o_hi, o_lo = _split_bf16(o_all)
    mu = _dot(o_hi, avg) + _dot(o_lo, avg)
    dlt = o_all - mu
    d_hi, d_lo = _split_bf16(dlt * dlt)
    var = _dot(d_hi, avg) + _dot(d_lo, avg)
    y = dlt * lax.rsqrt(var + EPS)
    y = jnp.concatenate([y[p * c:(p + 1) * c, :] for p in range(npair)], axis=1)
    o_ref[...] = (y * jax.nn.silu(g_ref[...])).astype(o_ref.dtype)


def _retention(proj, batch, seq):
    w = RET_HEADS * RET_DIM
    c = RET_CHUNK
    nc = seq // c
    npair = w // LANES
    pos = np.arange(seq)
    cq, sq = _rot_tables(pos, RET_DIM, RET_THETA, 1.0)
    ck, sk = _rot_tables(pos, RET_DIM, RET_THETA, RET_DIM ** -0.5)
    inner, qdec, kdec, cdec = _ret_decay_tables(c)
    blk = np.arange(LANES) // RET_DIM
    avg = jnp.asarray((blk[:, None] == blk[None, :]) / RET_DIM, BF16)
    col0 = 2
    tok = lambda j: pl.BlockSpec((c, w), lambda ci, b, j=j: (b * nc + ci, j))
    tab = pl.BlockSpec((c, LANES), lambda ci, b: (ci, 0))
    const = lambda shape: pl.BlockSpec(shape, lambda ci, b: (0,) * len(shape))
    return pl.pallas_call(
        _ret_kernel,
        grid=(nc, batch),
        in_specs=[tok(col0), tok(col0 + 1), tok(col0 + 2), tok(col0 + 3),
                  tab, tab, tab, tab,
                  const((RET_HEADS, c, c)), const((c, w)), const((c, w)), const((1, w)),
                  const((LANES, LANES))],
        out_specs=pl.BlockSpec((c, w), lambda ci, b: (b * nc + ci, 0)),
        out_shape=jax.ShapeDtypeStruct((batch * seq, w), BF16),
        scratch_shapes=[pltpu.VMEM((batch, npair, LANES, LANES), F32)],
        compiler_params=_cparams(("arbitrary", "arbitrary")),
        name="retention",
    )(proj, proj, proj, proj, cq, sq, ck, sk, inner, qdec, kdec, cdec, avg)


def _mix_ffn_kernel(*refs, n_mix, final):
    h_ref = refs[0]
    mix_refs = refs[1:1 + n_mix]
    wo_ref, g_ref, w1_ref, w3_ref, w2_ref = refs[1 + n_mix:6 + n_mix]
    rest = refs[6 + n_mix:]
    fg_ref = rest[0] if final else None
    o_ref = rest[-1]

    mixed = mix_refs[0][...] if n_mix == 1 else jnp.concatenate([r[...] for r in mix_refs], axis=1)
    h1 = h_ref[...] + _dot(mixed, wo_ref[...])
    xn = _rms(h1, g_ref[...]).astype(BF16)
    act = (jax.nn.silu(_dot(xn, w1_ref[...])) * _dot(xn, w3_ref[...])).astype(BF16)
    h2 = h1 + _dot(act, w2_ref[...])
    if final:
        h2 = _rms(h2, fg_ref[...])
    o_ref[...] = h2


def _mix_ffn(h2d, mixes, wo, g, w1, w3, w2, final_g=None):
    t, d = h2d.shape
    hid = w1.shape[1]
    final = final_g is not None
    row = lambda width: pl.BlockSpec((TOK_TILE, width), lambda i: (i, 0))
    const = lambda shape: pl.BlockSpec(shape, lambda i: (0, 0), pipeline_mode=pl.Buffered(1))
    in_specs = [row(d)] + [row(m.shape[1]) for m in mixes] + [
        const(wo.shape), const((1, d)), const((d, hid)), const((d, hid)), const((hid, d))]
    args = [h2d, *mixes, wo, g, w1, w3, w2]
    if final:
        in_specs.append(const((1, d)))
        args.append(final_g)
    return pl.pallas_call(
        functools.partial(_mix_ffn_kernel, n_mix=len(mixes), final=final),
        grid=(t // TOK_TILE,),
        in_specs=in_specs,
        out_specs=row(d),
        out_shape=jax.ShapeDtypeStruct((t, d), F32),
        compiler_params=_cparams(("parallel",)),
        name="mix_ffn_final" if final else "mix_ffn",
    )(*args)


def _proj_nsa_kernel(x_ref, g_ref, w_ref, cq_ref, sq_ref, ck_ref, sk_ref,
                     q_ref, cmp_ref, ks_ref, vs_ref, kw_ref, vw_ref, gt_ref, *, seq_tiles):
    hd = q_ref.shape[0]
    kvw = NSA_KV_GROUPS * NSA_HEAD_DIM
    xn = _rms(x_ref[...], g_ref[...]).astype(BF16)
    proj = _dot(xn, w_ref[...])
    half = ROT_DIM // 2
    q = proj[:, :hd]
    q_ref[...] = jnp.transpose(_rotate(q, _tile_lanes(cq_ref[...], hd // LANES),
                                       _tile_lanes(sq_ref[...], hd // LANES), half))
    kv = proj[:, hd:hd + 6 * kvw]
    for c in range(cmp_ref.shape[0]):
        cmp_ref[c] = kv[:, c * LANES:(c + 1) * LANES]
    ck = _tile_lanes(ck_ref[...], kvw // LANES)
    sk = _tile_lanes(sk_ref[...], kvw // LANES)
    rows = x_ref.shape[0]
    tpos = (pl.program_id(0) % seq_tiles) * rows + lax.broadcasted_iota(jnp.int32, (rows, LANES), 0)
    lane = lax.broadcasted_iota(jnp.int32, (rows, LANES), 1)
    block_onehot = jnp.where(lane - NSA_HEAD_DIM == tpos // SLC_LEN, 1.0, 0.0)
    ks_ref[...] = _group_slabs(_rotate(kv[:, 2 * kvw:3 * kvw], ck, sk, half), block_onehot).astype(ks_ref.dtype)
    one_lane = jnp.where(lane == NSA_HEAD_DIM, 1.0, 0.0)
    kw_ref[...] = _group_slabs(_rotate(kv[:, 4 * kvw:5 * kvw], ck, sk, half), one_lane).astype(kw_ref.dtype)
    _store_values_t(kv[:, 3 * kvw:4 * kvw], vs_ref)
    _store_values_t(kv[:, 5 * kvw:6 * kvw], vw_ref)
    gt_ref[...] = jnp.transpose(jax.nn.sigmoid(proj[:, hd + 6 * kvw:]))


def _proj_nsa(x2, g, w, seq):
    t, d = x2.shape
    hd = NSA_HEADS * NSA_HEAD_DIM
    kvw = NSA_KV_GROUPS * NSA_HEAD_DIM
    gw = NSA_KV_GROUPS * LANES
    n = w.shape[1]
    ns = seq // TOK_TILE
    pos = np.arange(seq)
    cq, sq = _rot_tables(pos, ROT_DIM, ROPE_THETA, NSA_HEAD_DIM ** -0.5 * LOG2E)
    ck, sk = _rot_tables(pos, ROT_DIM, ROPE_THETA, 1.0)
    row = lambda width: pl.BlockSpec((TOK_TILE, width), lambda i: (i, 0))
    tab = pl.BlockSpec((TOK_TILE, LANES), lambda i: (i % ns, 0))
    chunks = TOK_TILE // LANES
    val_t = (pl.BlockSpec((1, NSA_KV_GROUPS, chunks, V_ROWS, LANES), lambda i: (i // ns, 0, i % ns, 0, 0)),
             jax.ShapeDtypeStruct((t // seq, NSA_KV_GROUPS, seq // LANES, V_ROWS, LANES), BF16))
    tok = lambda wd, dt: (row(wd), jax.ShapeDtypeStruct((t, wd), dt))
    cmp_planes = (pl.BlockSpec((2 * kvw // LANES, TOK_TILE, LANES), lambda i: (0, i, 0)),
                  jax.ShapeDtypeStruct((2 * kvw // LANES, t, LANES), F32))
    tok_t = lambda wd: (pl.BlockSpec((wd, TOK_TILE), lambda i: (0, i)), jax.ShapeDtypeStruct((wd, t), F32))
    outs = [tok_t(hd), cmp_planes, tok(gw, BF16), val_t, tok(gw, BF16), val_t, tok_t(gw)]
    return pl.pallas_call(
        functools.partial(_proj_nsa_kernel, seq_tiles=ns),
        grid=(t // TOK_TILE,),
        in_specs=[row(d), pl.BlockSpec((1, d), lambda i: (0, 0)),
                  pl.BlockSpec((d, n), lambda i: (0, 0), pipeline_mode=pl.Buffered(1)),
                  tab, tab, tab, tab],
        out_specs=[spec for spec, _ in outs],
        out_shape=[shape for _, shape in outs],
        compiler_params=_cparams(("parallel",)),
        name="proj_nsa",
    )(x2, g, w, cq, sq, ck, sk)


def _compress_kernel(x_ref, pe_ref, w1_ref, wc_ref, w2k_ref, w2vt_ref, cos_ref, sgn_ref, kc_ref, vct_ref):
    n = kc_ref.shape[2]
    dh = NSA_HEAD_DIM
    for kv in range(2):
        pe_term = _dot(pe_ref[kv].astype(BF16), w1_ref[kv])[0:1, :]
        pe_term = jnp.concatenate([pe_term, pe_term], axis=1)
        for cp in range(NSA_KV_GROUPS // 2):
            col = kv * (NSA_KV_GROUPS // 2) + cp
            acc = jnp.zeros((n, 2 * LANES), F32)
            for j in range(CMP_STRIDE):
                rows = x_ref[col, pl.ds(j, n, stride=CMP_STRIDE), :]
                acc = acc + _dot(rows.astype(BF16), wc_ref[kv, j])
            hid = jax.nn.gelu(acc[:, :LANES] + pltpu.roll(acc[:, LANES:], n - 1, axis=0) + pe_term).astype(BF16)
            if kv == 0:
                kc = _dot(hid, w2k_ref[...])
                kc = _rotate(kc, _tile_lanes(cos_ref[...], 2), _tile_lanes(sgn_ref[...], 2), ROT_DIM // 2)
                kc_ref[0, 2 * cp] = kc[:, :LANES].astype(kc_ref.dtype)
                kc_ref[0, 2 * cp + 1] = kc[:, LANES:].astype(kc_ref.dtype)
            else:
                vc_t = _dot_nt(w2vt_ref[...], hid)
                vct_ref[0, 2 * cp] = _values_t(vc_t[:dh, :], n)
                vct_ref[0, 2 * cp + 1] = _values_t(vc_t[dh:, :], n)


def _compress(cmp_in, batch, seq, pe, w1, wc, w2k, w2v_t, cos, sgn):
    g = NSA_KV_GROUPS
    n = seq // CMP_STRIDE
    full = lambda a: pl.BlockSpec(a.shape, lambda bi: (0,) * a.ndim)
    return pl.pallas_call(
        _compress_kernel,
        grid=(batch,),
        in_specs=[pl.BlockSpec((cmp_in.shape[0], seq, LANES), lambda bi: (0, bi, 0)),
                  full(pe), full(w1), full(wc), full(w2k), full(w2v_t), full(cos), full(sgn)],
        out_specs=[pl.BlockSpec((1, g, n, LANES), lambda bi: (bi, 0, 0, 0)),
                   pl.BlockSpec((1, g, V_ROWS, n), lambda bi: (bi, 0, 0, 0))],
        out_shape=[jax.ShapeDtypeStruct((batch, g, n, LANES), BF16),
                   jax.ShapeDtypeStruct((batch, g, V_ROWS, n), BF16)],
        compiler_params=_cparams(("parallel",)),
        name="compress",
    )(cmp_in, pe, w1, wc, w2k, w2v_t, cos, sgn)


def _topk_bias_t(imp_t, tpos, last_block, val_ref, cnt_ref):
    nb, tq = imp_t.shape
    j = lax.broadcasted_iota(jnp.int32, (nb, tq), 0)
    cur = tpos // SLC_LEN
    forced = (j == 0) | (j == cur) | (j == cur - 1)
    causal = j <= cur
    val_ref[...] = jnp.where(forced, BIG, jnp.where(causal, imp_t, NEG))
    cnt_ref[...] = jnp.zeros((nb, tq), F32)
    nchunk = nb // SUBLANES
    rows = lax.broadcasted_iota(jnp.int32, (SUBLANES, tq), 0)
    for c in range(nchunk):
        for ic in range(nchunk):
            @pl.when(max(c, ic) * SUBLANES <= last_block)
            def _():
                vc = val_ref[c * SUBLANES:(c + 1) * SUBLANES, :]
                acc = cnt_ref[c * SUBLANES:(c + 1) * SUBLANES, :]
                for ii in range(SUBLANES):
                    vi = val_ref[ic * SUBLANES + ii:ic * SUBLANES + ii + 1, :]
                    if c > ic:
                        ahead = jnp.where(vi >= vc, 1.0, 0.0)
                    elif c < ic:
                        ahead = jnp.where(vi > vc, 1.0, 0.0)
                    else:
                        ahead = jnp.where(rows > ii, jnp.where(vi >= vc, 1.0, 0.0),
                                          jnp.where(vi > vc, 1.0, 0.0))
                    acc = acc + ahead
                cnt_ref[c * SUBLANES:(c + 1) * SUBLANES, :] = acc
    return jnp.where((cnt_ref[...] < float(SLC_TOPK)) & causal, 0.0, NEG)


def _value_chunks_t(vt_ref, g, first, count):
    return jnp.concatenate([vt_ref[0, g, first + i] for i in range(count)], axis=1)


def _max_key_norms(k_ref, chunk):
    seq, width = k_ref.shape
    row = lax.broadcasted_iota(jnp.int32, (width, LANES), 0)
    lane = lax.broadcasted_iota(jnp.int32, (width, LANES), 1)
    dims_of_group = jnp.where((row // LANES == lane) & (row % LANES < NSA_HEAD_DIM), 1.0, 0.0).astype(BF16)
    best = jnp.zeros((1, LANES), F32)
    for i in range(seq // chunk):
        k = k_ref[i * chunk:(i + 1) * chunk, :].astype(F32)
        sq_hi, sq_lo = _split_bf16(k * k)
        norm2 = _dot(sq_hi, dims_of_group) + _dot(sq_lo, dims_of_group)
        best = jnp.maximum(best, jnp.max(norm2, axis=0, keepdims=True))
    return jnp.sqrt(best) * NORM_MARGIN


def _nsa_attn_kernel(q_ref, gt_ref, kc_ref, vct_ref, ks_ref, vst_ref, kw_ref, vwt_ref, c2s_ref, o_ref,
                     val_ref, cnt_ref, acc_ref, kmax_ref, *, n_slc, tk):
    tq = q_ref.shape[1]
    ncmp = kc_ref.shape[2]
    ngrp = NSA_KV_GROUPS
    nrep = NSA_HEADS // ngrp
    dh = NSA_HEAD_DIM
    groups = range(ngrp)
    t0 = pl.program_id(1) * tq
    head = lambda x, r: x[:, r * tq:(r + 1) * tq]
    grp_lanes = lambda g: slice(g * LANES, (g + 1) * LANES)

    @pl.when(pl.program_id(1) == 0)
    def _():
        kmax_ref[0:1, :] = _max_key_norms(ks_ref, tk)
        kmax_ref[1:2, :] = _max_key_norms(kw_ref, tk)

    tpos = t0 + lax.broadcasted_iota(jnp.int32, (1, tq), 1)
    q_t = []
    for g in groups:
        cols = [q_ref[(g * nrep + r) * dh:(g * nrep + r + 1) * dh, :] for r in range(nrep)]
        q_t.append(jnp.concatenate([jnp.concatenate(cols, axis=1), jnp.zeros((LANES - dh, nrep * tq), F32)], axis=0))

    cmp_end = lax.broadcasted_iota(jnp.int32, (ncmp, tq), 0) * CMP_STRIDE + (CMP_LEN - 1)
    vis_bias = jnp.where(cmp_end <= tpos, 0.0, NEG)
    any_vis = jnp.where(tpos >= CMP_LEN - 1, 1.0, 0.0)
    c2s = c2s_ref[...]
    o_cmp, imp_t = [], []
    for g in groups:
        s = _dot(kc_ref[0, g], q_t[g].astype(BF16))
        p_sum = jnp.zeros((ncmp, tq), F32)
        p_cmp = []
        for r in range(nrep):
            sr = head(s, r) + vis_bias
            e = jnp.exp2(sr - jnp.max(sr, axis=0, keepdims=True))
            pr = e * (any_vis / jnp.sum(e, axis=0, keepdims=True))
            p_sum = p_sum + pr
            p_cmp.append(pr.astype(BF16))
        o_cmp.append(_dot(vct_ref[0, g], jnp.concatenate(p_cmp, axis=1)))
        p_hi, p_lo = _split_bf16(p_sum)
        imp_t.append((_dot(c2s, p_hi) + _dot(c2s, p_lo))[:n_slc, :])

    bias_t = _topk_bias_t(jnp.concatenate(imp_t, axis=1), jnp.concatenate([tpos] * ngrp, axis=1),
                          (t0 + tq - 1) // SLC_LEN, val_ref, cnt_ref)
    if n_slc < dh:
        bias_t = jnp.concatenate([bias_t, jnp.full((dh - n_slc, ngrp * tq), NEG, F32)], axis=0)
    bias_rows = [jnp.concatenate([head(bias_t, g)] * nrep, axis=1) for g in groups]

    q_norm = [jnp.sqrt(jnp.sum(q_t[g] * q_t[g], axis=0, keepdims=True)) for g in groups]
    shift_s = [q_norm[g] * kmax_ref[0:1, g:g + 1] for g in groups]
    shift_w = [q_norm[g] * kmax_ref[1:2, g:g + 1] for g in groups]
    worst = functools.reduce(jnp.maximum, shift_s + shift_w)
    fixed_shift_ok = jnp.max(worst) < MAX_FIXED_SHIFT

    last = (t0 + tq - 1) // tk
    k_last = last * tk + lax.broadcasted_iota(jnp.int32, (tk, tq), 0)
    causal_bias = jnp.where(k_last <= tpos, 0.0, NEG)
    span = WINDOW + tq
    start = pl.multiple_of(jnp.maximum(t0 - WINDOW, 0), tq)
    dist = tpos - (start + lax.broadcasted_iota(jnp.int32, (span, tq), 0))
    band_bias = jnp.where((dist >= 0) & (dist < WINDOW), 0.0, NEG)
    key_tile = lambda c, g: ks_ref[pl.ds(pl.multiple_of(c * tk, tk), tk), grp_lanes(g)]
    val_tile = lambda c, g: _value_chunks_t(vst_ref, g, c * (tk // LANES), tk // LANES)
    win_keys = lambda g: kw_ref[pl.ds(start, span), grp_lanes(g)]
    win_vals = lambda g: _value_chunks_t(vwt_ref, g, start // LANES, span // LANES)
    all_heads = lambda x: jnp.concatenate([x] * nrep, axis=1)

    @pl.when(fixed_shift_ok)
    def _():
        qt_sel = [jnp.concatenate([q_t[g][:dh, :], bias_rows[g] - shift_s[g]], axis=0).astype(BF16) for g in groups]

        def partial_pv(c, g, token_bias=None):
            sc = _dot(key_tile(c, g), qt_sel[g])
            if token_bias is not None:
                sc = sc + all_heads(token_bias)
            return _dot(val_tile(c, g), jnp.exp2(sc).astype(BF16))

        def body(i, accs):
            return tuple(accs[g] + partial_pv(2 * i, g) + partial_pv(2 * i + 1, g) for g in groups)

        accs = lax.fori_loop(0, last // 2, body, tuple(jnp.zeros((V_ROWS, nrep * tq), F32) for _ in groups))
        for g in groups:
            acc_ref[0, g] = accs[g] + partial_pv(last, g, causal_bias)

        @pl.when(last % 2 == 1)
        def _():
            for g in groups:
                acc_ref[0, g] += partial_pv(last - 1, g)

        zeros_hi = jnp.zeros((dh - 1, nrep * tq), F32)
        for g in groups:
            qt_win = jnp.concatenate([q_t[g][:dh, :], -shift_w[g], zeros_hi], axis=0).astype(BF16)
            p = jnp.exp2(_dot(win_keys(g), qt_win) + all_heads(band_bias)).astype(BF16)
            acc_ref[1, g] = _dot(win_vals(g), p)

    @pl.when(jnp.logical_not(fixed_shift_ok))
    def _():
        qt_sel = [jnp.concatenate([q_t[g][:dh, :], bias_rows[g]], axis=0).astype(BF16) for g in groups]

        def tile(c, carry, token_bias):
            out = []
            for g in groups:
                m, acc = carry[g]
                sc = _dot(key_tile(c, g), qt_sel[g])
                ps, ms, alphas = [], [], []
                for r in range(nrep):
                    sr = head(sc, r)
                    if token_bias is not None:
                        sr = sr + token_bias
                    m_new = jnp.maximum(head(m, r), jnp.max(sr, axis=0, keepdims=True))
                    alphas.append(jnp.exp2(head(m, r) - m_new))
                    ms.append(m_new)
                    ps.append(jnp.exp2(sr - m_new).astype(BF16))
                pv = _dot(val_tile(c, g), jnp.concatenate(ps, axis=1))
                out.append((jnp.concatenate(ms, axis=1), jnp.concatenate(alphas, axis=1) * acc + pv))
            return tuple(out)

        init = tuple((jnp.full((1, nrep * tq), NEG, F32), jnp.zeros((V_ROWS, nrep * tq), F32)) for _ in groups)
        carry = lax.fori_loop(0, last, lambda c, cr: tile(c, cr, None), init)
        carry = tile(last, carry, causal_bias)
        for g in groups:
            acc_ref[0, g] = carry[g][1]
            sw = _dot(win_keys(g), q_t[g].astype(BF16))
            p_win = []
            for r in range(nrep):
                sr = head(sw, r) + band_bias
                p_win.append(jnp.exp2(sr - jnp.max(sr, axis=0, keepdims=True)).astype(BF16))
            acc_ref[1, g] = _dot(win_vals(g), jnp.concatenate(p_win, axis=1))

    gt_t = gt_ref[...]
    heads = []
    for g in groups:
        acc_s = acc_ref[0, g]
        acc_w = acc_ref[1, g]
        o_slc = acc_s[:dh, :] * (1.0 / acc_s[dh:dh + 1, :])
        o_win = acc_w[:dh, :] * (1.0 / acc_w[dh:dh + 1, :])
        for r in range(nrep):
            row = g * LANES + 3 * r
            heads.append(gt_t[row:row + 1, :] * head(o_cmp[g], r)[:dh, :]
                         + gt_t[row + 1:row + 2, :] * head(o_slc, r) + gt_t[row + 2:row + 3, :] * head(o_win, r))
    o_ref[...] = jnp.transpose(jnp.concatenate(heads, axis=0)).astype(o_ref.dtype)


def _nsa_attention(q, gt, kc, vc_t, ks, vs_t, kw, vw_t, batch, seq):
    g = NSA_KV_GROUPS
    nq = seq // ATT_TQ
    qw = (NSA_HEADS // g) * NSA_HEAD_DIM
    n_slc = seq // SLC_LEN
    n_cmp = (seq - CMP_LEN) // CMP_STRIDE + 1
    n_cmp_pad = kc.shape[2]
    c_start = np.arange(n_cmp)[:, None] * CMP_STRIDE
    s_start = np.arange(n_slc)[None, :] * SLC_LEN
    overlap = np.clip(np.minimum(c_start + CMP_LEN, s_start + SLC_LEN)
                      - np.maximum(c_start, s_start), 0, None) / CMP_LEN
    assert n_slc <= NSA_HEAD_DIM and n_slc % SUBLANES == 0 and seq % ATT_TK == 0
    c2s_t = np.zeros((NSA_HEAD_DIM, n_cmp_pad))
    c2s_t[:n_slc, :n_cmp] = overlap.T
    hd = NSA_HEADS * NSA_HEAD_DIM
    tok_spec = lambda width: pl.BlockSpec((ATT_TQ, width), lambda b, qi: (b * nq + qi, 0))
    key_spec = pl.BlockSpec((seq, g * LANES), lambda b, qi: (b, 0))
    val_spec = pl.BlockSpec((1, g, seq // LANES, V_ROWS, LANES), lambda b, qi: (b, 0, 0, 0, 0))
    return pl.pallas_call(
        functools.partial(_nsa_attn_kernel, n_slc=n_slc, tk=ATT_TK),
        grid=(batch, nq),
        in_specs=[pl.BlockSpec((hd, ATT_TQ), lambda b, qi: (0, b * nq + qi)),
                  pl.BlockSpec((g * LANES, ATT_TQ), lambda b, qi: (0, b * nq + qi)),
                  pl.BlockSpec((1, g, n_cmp_pad, LANES), lambda b, qi: (b, 0, 0, 0)),
                  pl.BlockSpec((1, g, V_ROWS, n_cmp_pad), lambda b, qi: (b, 0, 0, 0)),
                  key_spec, val_spec, key_spec, val_spec,
                  pl.BlockSpec(c2s_t.shape, lambda b, qi: (0, 0))],
        out_specs=tok_spec(hd),
        out_shape=jax.ShapeDtypeStruct((batch * seq, hd), BF16),
        scratch_shapes=[pltpu.VMEM((n_slc, g * ATT_TQ), F32), pltpu.VMEM((n_slc, g * ATT_TQ), F32),
                        pltpu.VMEM((2, g, V_ROWS, (NSA_HEADS // g) * ATT_TQ), F32),
                        pltpu.VMEM((2, LANES), F32)],
        compiler_params=_cparams(("parallel", "arbitrary")),
        name="nsa_attention",
    )(q, gt, kc, vc_t, ks, vs_t, kw, vw_t, jnp.asarray(c2s_t, BF16))


def _block_diag(w):
    nb, bi, bo = w.shape
    eye = jnp.eye(nb, dtype=w.dtype)
    return (eye[:, None, :, None] * w[:, :, None, :]).reshape(nb * bi, nb * bo)


def _nsa_weight(w_in):
    hd = NSA_HEADS * NSA_HEAD_DIM
    kvw = NSA_KV_GROUPS * NSA_HEAD_DIM
    nrep = NSA_HEADS // NSA_KV_GROUPS
    d = w_in.shape[0]
    gates = w_in[:, hd + 6 * kvw:].reshape(d, NSA_KV_GROUPS, 3 * nrep)
    gates = jnp.pad(gates, ((0, 0), (0, 0), (0, LANES - 3 * nrep))).reshape(d, NSA_KV_GROUPS * LANES)
    return jnp.concatenate([w_in[:, :hd + 6 * kvw], gates], axis=1).astype(BF16)


def _ab_mixer(h, g_attn, w_in, conv_w, conv_b, ga_w, ga_b, gx_w, gx_b, lam, batch, seq):
    proj = _proj_ab(h, g_attn, w_in.astype(BF16))
    gate_w = jnp.concatenate([_block_diag(ga_w), _block_diag(gx_w)], axis=1).astype(BF16)
    gate_b = jnp.concatenate([ga_b, gx_b])[None, :]
    lru_out = _lru(proj, batch, seq, conv_w, conv_b[None, :], gate_w, gate_b, lam[None, :])
    return [lru_out, _retention(proj, batch, seq)]


def _nsa_mixer(h, g_attn, w_in, pe_k, k_w1, k_w2, pe_v, v_w1, v_w2, batch, seq):
    q, cmp_in, ks, vs_t, kw, vw_t, gt = _proj_nsa(h, g_attn, _nsa_weight(w_in), seq)
    dh = NSA_HEAD_DIM
    n_blk = seq // CMP_STRIDE
    pe = jnp.broadcast_to(jnp.stack([pe_k, pe_v]).reshape(2, 1, CMP_LEN * dh), (2, SUBLANES, CMP_LEN * dh))
    w1 = jnp.stack([k_w1, v_w1]).astype(BF16)
    w1r = w1.reshape(2, CMP_LEN, dh, dh)
    first, second, zero = w1r[:, :CMP_STRIDE], w1r[:, CMP_STRIDE:], jnp.zeros_like(w1r[:, :CMP_STRIDE])
    wc = jnp.concatenate([jnp.concatenate([first, zero, second, zero], axis=-1),
                          jnp.concatenate([zero, first, zero, second], axis=-1)], axis=-2)
    w2k = k_w2.astype(BF16)
    z = jnp.zeros_like(w2k)
    w2k = jnp.concatenate([jnp.concatenate([w2k, z, z, z], axis=1), jnp.concatenate([z, z, w2k, z], axis=1)], axis=0)
    w2v_t = v_w2.T.astype(BF16)
    zt = jnp.zeros_like(w2v_t)
    w2v_t = jnp.concatenate([jnp.concatenate([w2v_t, zt], axis=1), jnp.concatenate([zt, w2v_t], axis=1)], axis=0)
    cos, sgn = _rot_tables(np.arange(n_blk) * CMP_STRIDE + CMP_LEN - 1, ROT_DIM, ROPE_THETA, 1.0)
    kc, vc_t = _compress(cmp_in, batch, seq, pe, w1, wc, w2k, w2v_t, cos, sgn)
    return [_nsa_attention(q, gt, kc, vc_t, ks, vs_t, kw, vw_t, batch, seq)]


def kernel(x, attn_norm, ab_w_in, conv_w, conv_b, gate_a_w, gate_a_b, gate_x_w, gate_x_b, lru_lambda,
           ab_w_out, nsa_w_in, cmp_pe_k, cmp_k_w1, cmp_k_w2, cmp_pe_v, cmp_v_w1, cmp_v_w2, nsa_w_out,
           ffn_norm, ffn_w1, ffn_w3, ffn_w2, final_norm):
    batch, seq, d = x.shape
    depth = attn_norm.shape[0]
    h = x.reshape(batch * seq, d)
    for layer in range(depth):
        i = layer // 2
        g_attn = attn_norm[layer][None, :]
        if layer % 2 == 0:
            mixes = _ab_mixer(h, g_attn, ab_w_in[i], conv_w[i], conv_b[i], gate_a_w[i], gate_a_b[i],
                              gate_x_w[i], gate_x_b[i], lru_lambda[i], batch, seq)
            w_out = ab_w_out[i]
        else:
            mixes = _nsa_mixer(h, g_attn, nsa_w_in[i], cmp_pe_k[i], cmp_k_w1[i], cmp_k_w2[i],
                               cmp_pe_v[i], cmp_v_w1[i], cmp_v_w2[i], batch, seq)
            w_out = nsa_w_out[i]
        last = layer == depth - 1
        h = _mix_ffn(h, mixes, w_out.astype(BF16), ffn_norm[layer][None, :],
                     ffn_w1[layer].astype(BF16), ffn_w3[layer].astype(BF16), ffn_w2[layer].astype(BF16),
                     final_g=final_norm[None, :] if last else None)
    return h.reshape(batch, seq, d)
```

```python
import functools

import numpy as np
import jax
import jax.numpy as jnp
from jax import lax
from jax.experimental import pallas as pl
from jax.experimental.pallas import tpu as pltpu

F32 = jnp.float32
BF16 = jnp.bfloat16

EPS = 1e-6
NEG = -1e30
BIG = 1e30
LOG2E = float(np.log2(np.e))

LRU_BLOCKS = 8
CONV_WIDTH = 4
LRU_C = 8.0
RET_HEADS = 8
RET_DIM = 64
RET_THETA = 10000.0
NSA_HEADS = 16
NSA_KV_GROUPS = 4
NSA_HEAD_DIM = 64
CMP_LEN = 32
CMP_STRIDE = 16
SLC_LEN = 64
SLC_TOPK = 16
WINDOW = 512
ROPE_THETA = 500000.0
ROT_DIM = NSA_HEAD_DIM // 4

LANES = 128
SUBLANES = 8
VMEM_LIMIT = 56 * 1024 * 1024

RET_CHUNK = 256
TOK_TILE = 512
LRU_TILE = 512
ATT_TQ = 256
ATT_TK = 512
V_ROWS = NSA_HEAD_DIM + 16
MAX_FIXED_SHIFT = 32.0
NORM_MARGIN = 1.01


def _cparams(sem):
    return pltpu.CompilerParams(dimension_semantics=sem, vmem_limit_bytes=VMEM_LIMIT)


def _dot(a, b):
    return jnp.dot(a, b, preferred_element_type=F32)


def _dot_nt(a, b):
    return lax.dot_general(a, b, (((1,), (1,)), ((), ())), preferred_element_type=F32)


def _rms(x, g):
    return x * lax.rsqrt(jnp.mean(x * x, axis=-1, keepdims=True) + EPS) * g


def _tile_lanes(t, reps):
    return t if reps == 1 else jnp.concatenate([t] * reps, axis=1)


def _rotate(x, cos, sgn, half):
    n = x.shape[1]
    lane = lax.broadcasted_iota(jnp.int32, x.shape, 1)
    first = (lane % NSA_HEAD_DIM) < half
    partner = jnp.where(first, pltpu.roll(x, n - half, axis=1), pltpu.roll(x, half, axis=1))
    return x * cos + partner * sgn


def _group_slabs(x, fill):
    lane = lax.broadcasted_iota(jnp.int32, (x.shape[0], LANES), 1)
    lo = lane < NSA_HEAD_DIM
    out = []
    for c in range(x.shape[1] // LANES):
        xc = x[:, c * LANES:(c + 1) * LANES]
        out.append(jnp.where(lo, xc, fill))
        out.append(jnp.where(lo, pltpu.roll(xc, NSA_HEAD_DIM, axis=1), fill))
    return jnp.concatenate(out, axis=1)


def _values_t(v_t, ncols):
    return jnp.concatenate([v_t, jnp.ones((V_ROWS - NSA_HEAD_DIM, ncols), F32)], axis=0).astype(BF16)


def _store_values_t(v, o_ref):
    rows = v.shape[0]
    v_t = jnp.transpose(v)
    for g in range(v.shape[1] // NSA_HEAD_DIM):
        slab = _values_t(v_t[g * NSA_HEAD_DIM:(g + 1) * NSA_HEAD_DIM, :], rows)
        for c in range(rows // LANES):
            o_ref[0, g, c] = slab[:, c * LANES:(c + 1) * LANES]


def _rot_tables(pos, rot_dim, theta, scale):
    half = rot_dim // 2
    inv = theta ** (-np.arange(0, rot_dim, 2, dtype=np.float64) / rot_dim)
    ang = np.asarray(pos, np.float64)[:, None] * inv
    cos = np.ones((len(pos), NSA_HEAD_DIM))
    sgn = np.zeros((len(pos), NSA_HEAD_DIM))
    cos[:, :half] = np.cos(ang)
    cos[:, half:rot_dim] = np.cos(ang)
    sgn[:, :half] = -np.sin(ang)
    sgn[:, half:rot_dim] = np.sin(ang)
    cos = np.tile(cos * scale, (1, 2))
    sgn = np.tile(sgn * scale, (1, 2))
    return jnp.asarray(cos, F32), jnp.asarray(sgn, F32)


def _ret_decay_tables(chunk):
    log_g = np.log1p(-(2.0 ** (-5.0 - np.arange(RET_HEADS, dtype=np.float64))))
    ci = np.arange(chunk, dtype=np.float64)
    diff = ci[:, None] - ci[None, :]
    inner = np.where(diff >= 0, np.exp(np.maximum(diff, 0.0) * log_g[:, None, None]), 0.0)
    lane_head = np.repeat(np.arange(RET_HEADS), RET_DIM)
    qdec = np.exp((ci[:, None] + 1.0) * log_g[lane_head][None, :])
    kdec = np.exp((chunk - 1.0 - ci[:, None]) * log_g[lane_head][None, :])
    cdec = np.exp(chunk * log_g[lane_head])[None, :]
    return (jnp.asarray(inner, F32), jnp.asarray(qdec, F32), jnp.asarray(kdec, F32),
            jnp.asarray(cdec, F32))


def _proj_ab_kernel(x_ref, g_ref, w_ref, o_ref):
    xn = _rms(x_ref[...], g_ref[...]).astype(BF16)
    o_ref[...] = _dot(xn, w_ref[...])


def _proj_ab(x2, g, w):
    t, d = x2.shape
    n = w.shape[1]
    return pl.pallas_call(
        _proj_ab_kernel,
        grid=(t // TOK_TILE,),
        in_specs=[pl.BlockSpec((TOK_TILE, d), lambda i: (i, 0)),
                  pl.BlockSpec((1, d), lambda i: (0, 0)),
                  pl.BlockSpec((d, n), lambda i: (0, 0), pipeline_mode=pl.Buffered(1))],
        out_specs=pl.BlockSpec((TOK_TILE, n), lambda i: (i, 0)),
        out_shape=jax.ShapeDtypeStruct((t, n), F32),
        compiler_params=_cparams(("parallel",)),
        name="proj_ab",
    )(x2, g, w)


def _lru_kernel(y_ref, x_ref, cw_ref, cb_ref, gw_ref, gb_ref, lam_ref, o_ref,
                xp_ref, a_ref, u_ref, h_ref, carry_ref):
    ts, w = x_ref.shape
    row = lax.broadcasted_iota(jnp.int32, (SUBLANES, w), 0)

    @pl.when(pl.program_id(1) == 0)
    def _():
        xp_ref[...] = jnp.zeros_like(xp_ref)
        carry_ref[...] = jnp.zeros_like(carry_ref)

    x = x_ref[...]
    prev = xp_ref[...]
    xc = cb_ref[...] + x * cw_ref[CONV_WIDTH - 1:CONV_WIDTH, :]
    for s in range(1, CONV_WIDTH):
        rolled = pltpu.roll(x, s, axis=0)
        head_rows = jnp.where(row < s, pltpu.roll(prev, s, axis=0), rolled[:SUBLANES, :])
        shifted = jnp.concatenate([head_rows, rolled[SUBLANES:, :]], axis=0)
        xc = xc + shifted * cw_ref[CONV_WIDTH - 1 - s:CONV_WIDTH - s, :]
    xp_ref[...] = x[ts - SUBLANES:, :]

    gates = _dot(xc.astype(BF16), gw_ref[...]) + gb_ref[...]
    r = jax.nn.sigmoid(gates[:, :w])
    i = jax.nn.sigmoid(gates[:, w:])
    log_a = -LRU_C * r * jax.nn.softplus(-lam_ref[...])
    a = jnp.exp(log_a)
    z = jnp.tanh(-log_a) * (a * a + 1.0)
    a_ref[...] = a
    u_ref[...] = jnp.where(z > 0.0, z * lax.rsqrt(z), 0.0) * (i * xc)

    def body(gi, carry):
        r0 = pl.multiple_of(gi * SUBLANES, SUBLANES)
        a = a_ref[pl.ds(r0, SUBLANES), :]
        u = u_ref[pl.ds(r0, SUBLANES), :]
        for d in (1, 2, 4):
            a_s = pltpu.roll(a, d, axis=0)
            u_s = pltpu.roll(u, d, axis=0)
            m = row >= d
            u = jnp.where(m, a * u_s + u, u)
            a = jnp.where(m, a * a_s, a)
        h = a * carry + u
        h_ref[pl.ds(r0, SUBLANES), :] = h
        return h[SUBLANES - 1:SUBLANES, :]

    carry_ref[...] = lax.fori_loop(0, ts // SUBLANES, body, carry_ref[...])
    o_ref[...] = (h_ref[...] * jax.nn.gelu(y_ref[...])).astype(o_ref.dtype)


def _lru(proj, batch, seq, conv_w, conv_b, gate_w, gate_b, lam):
    w = conv_w.shape[1]
    ns = seq // LRU_TILE
    return pl.pallas_call(
        _lru_kernel,
        grid=(batch, ns),
        in_specs=[pl.BlockSpec((LRU_TILE, w), lambda b, s: (b * ns + s, 0)),
                  pl.BlockSpec((LRU_TILE, w), lambda b, s: (b * ns + s, 1)),
                  pl.BlockSpec((CONV_WIDTH, w), lambda b, s: (0, 0)),
                  pl.BlockSpec((1, w), lambda b, s: (0, 0)),
                  pl.BlockSpec((w, 2 * w), lambda b, s: (0, 0)),
                  pl.BlockSpec((1, 2 * w), lambda b, s: (0, 0)),
                  pl.BlockSpec((1, w), lambda b, s: (0, 0))],
        out_specs=pl.BlockSpec((LRU_TILE, w), lambda b, s: (b * ns + s, 0)),
        out_shape=jax.ShapeDtypeStruct((batch * seq, w), BF16),
        scratch_shapes=[pltpu.VMEM((SUBLANES, w), F32),
                        pltpu.VMEM((LRU_TILE, w), F32),
                        pltpu.VMEM((LRU_TILE, w), F32),
                        pltpu.VMEM((LRU_TILE, w), F32),
                        pltpu.VMEM((1, w), F32)],
        compiler_params=_cparams(("parallel", "arbitrary")),
        name="rg_lru",
    )(proj, proj, conv_w, conv_b, gate_w, gate_b, lam)


def _split_bf16(x):
    hi = x.astype(BF16)
    lo = (x - hi.astype(F32)).astype(BF16)
    return hi, lo


def _ret_kernel(q_ref, k_ref, v_ref, g_ref, cq_ref, sq_ref, ck_ref, sk_ref,
                inner_ref, qdec_ref, kdec_ref, cdec_ref, avg_ref, o_ref, st_ref):
    c, w = q_ref.shape
    npair = w // LANES
    b = pl.program_id(1)

    @pl.when(pl.program_id(0) == 0)
    def _():
        st_ref[b] = jnp.zeros(st_ref.shape[1:], F32)

    half = RET_DIM // 2
    q = _rotate(q_ref[...], _tile_lanes(cq_ref[...], npair), _tile_lanes(sq_ref[...], npair), half)
    k = _rotate(k_ref[...], _tile_lanes(ck_ref[...], npair), _tile_lanes(sk_ref[...], npair), half)
    v = v_ref[...]
    qd = (q * qdec_ref[...]).astype(BF16)
    kd = k * kdec_ref[...]
    q = q.astype(BF16)
    kb = k.astype(BF16)
    vb = v.astype(BF16)

    lane = lax.broadcasted_iota(jnp.int32, (c, LANES), 1)
    lo = lane < RET_DIM
    rr = lax.broadcasted_iota(jnp.int32, (LANES, LANES), 0) < RET_DIM
    cc = lax.broadcasted_iota(jnp.int32, (LANES, LANES), 1) < RET_DIM
    same_head = rr == cc
    zero = jnp.zeros((c, LANES), BF16)

    outs = []
    for p in range(npair):
        sl = slice(p * LANES, (p + 1) * LANES)
        qp, kp, vp = q[:, sl], kb[:, sl], vb[:, sl]
        s_lo = _dot_nt(jnp.where(lo, qp, zero), kp) * inner_ref[2 * p]
        s_hi = _dot_nt(jnp.where(lo, zero, qp), kp) * inner_ref[2 * p + 1]
        o = jnp.where(lo, _dot(s_lo.astype(BF16), vp), _dot(s_hi.astype(BF16), vp))
        st = st_ref[b, p]
        o = o + _dot(qd[:, sl], st.astype(BF16))
        kv = _dot(jnp.transpose(kd[:, sl]).astype(BF16), vp)
        st_ref[b, p] = st * cdec_ref[:, sl] + jnp.where(same_head, kv, 0.0)
        outs.append(o)

    o_all = jnp.concatenate(outs, axis=0)
    avg = avg_ref[...]
    o_hi, o_lo = _split_bf16(o_all)
    mu = _dot(o_hi, avg) + _dot(o_lo, avg)
    dlt = o_all - mu
    d_hi, d_lo = _split_bf16(dlt * dlt)
    var = _dot(d_hi, avg) + _dot(d_lo, avg)
    y = dlt * lax.rsqrt(var + EPS)
    y = jnp.concatenate([y[p * c:(p + 1) * c, :] for p in range(npair)], axis=1)
    o_ref[...] = (y * jax.nn.silu(g_ref[...])).astype(o_ref.dtype)


def _retention(proj, batch, seq):
    w = RET_HEADS * RET_DIM
    c = RET_CHUNK
    nc = seq // c
    npair = w // LANES
    pos = np.arange(seq)
    cq, sq = _rot_tables(pos, RET_DIM, RET_THETA, 1.0)
    ck, sk = _rot_tables(pos, RET_DIM, RET_THETA, RET_DIM ** -0.5)
    inner, qdec, kdec, cdec = _ret_decay_tables(c)
    blk = np.arange(LANES) // RET_DIM
    avg = jnp.asarray((blk[:, None] == blk[None, :]) / RET_DIM, BF16)
    col0 = 2
    tok = lambda j: pl.BlockSpec((c, w), lambda ci, b, j=j: (b * nc + ci, j))
    tab = pl.BlockSpec((c, LANES), lambda ci, b: (ci, 0))
    const = lambda shape: pl.BlockSpec(shape, lambda ci, b: (0,) * len(shape))
    return pl.pallas_call(
        _ret_kernel,
        grid=(nc, batch),
        in_specs=[tok(col0), tok(col0 + 1), tok(col0 + 2), tok(col0 + 3),
                  tab, tab, tab, tab,
                  const((RET_HEADS, c, c)), const((c, w)), const((c, w)), const((1, w)),
                  const((LANES, LANES))],
        out_specs=pl.BlockSpec((c, w), lambda ci, b: (b * nc + ci, 0)),
        out_shape=jax.ShapeDtypeStruct((batch * seq, w), BF16),
        scratch_shapes=[pltpu.VMEM((batch, npair, LANES, LANES), F32)],
        compiler_params=_cparams(("arbitrary", "arbitrary")),
        name="retention",
    )(proj, proj, proj, proj, cq, sq, ck, sk, inner, qdec, kdec, cdec, avg)


def _mix_ffn_kernel(*refs, n_mix, final):
    h_ref = refs[0]
    mix_refs = refs[1:1 + n_mix]
    wo_ref, g_ref, w1_ref, w3_ref, w2_ref = refs[1 + n_mix:6 + n_mix]
    rest = refs[6 + n_mix:]
    fg_ref = rest[0] if final else None
    o_ref = rest[-1]

    mixed = mix_refs[0][...] if n_mix == 1 else jnp.concatenate([r[...] for r in mix_refs], axis=1)
    h1 = h_ref[...] + _dot(mixed, wo_ref[...])
    xn = _rms(h1, g_ref[...]).astype(BF16)
    act = (jax.nn.silu(_dot(xn, w1_ref[...])) * _dot(xn, w3_ref[...])).astype(BF16)
    h2 = h1 + _dot(act, w2_ref[...])
    if final:
        h2 = _rms(h2, fg_ref[...])
    o_ref[...] = h2


def _mix_ffn(h2d, mixes, wo, g, w1, w3, w2, final_g=None):
    t, d = h2d.shape
    hid = w1.shape[1]
    final = final_g is not None
    row = lambda width: pl.BlockSpec((TOK_TILE, width), lambda i: (i, 0))
    const = lambda shape: pl.BlockSpec(shape, lambda i: (0, 0), pipeline_mode=pl.Buffered(1))
    in_specs = [row(d)] + [row(m.shape[1]) for m in mixes] + [
        const(wo.shape), const((1, d)), const((d, hid)), const((d, hid)), const((hid, d))]
    args = [h2d, *mixes, wo, g, w1, w3, w2]
    if final:
        in_specs.append(const((1, d)))
        args.append(final_g)
    return pl.pallas_call(
        functools.partial(_mix_ffn_kernel, n_mix=len(mixes), final=final),
        grid=(t // TOK_TILE,),
        in_specs=in_specs,
        out_specs=row(d),
        out_shape=jax.ShapeDtypeStruct((t, d), F32),
        compiler_params=_cparams(("parallel",)),
        name="mix_ffn_final" if final else "mix_ffn",
    )(*args)


def _proj_nsa_kernel(x_ref, g_ref, w_ref, cq_ref, sq_ref, ck_ref, sk_ref,
                     q_ref, cmp_ref, ks_ref, vs_ref, kw_ref, vw_ref, gt_ref, *, seq_tiles):
    hd = q_ref.shape[0]
    kvw = NSA_KV_GROUPS * NSA_HEAD_DIM
    xn = _rms(x_ref[...], g_ref[...]).astype(BF16)
    proj = _dot(xn, w_ref[...])
    half = ROT_DIM // 2
    q = proj[:, :hd]
    q_ref[...] = jnp.transpose(_rotate(q, _tile_lanes(cq_ref[...], hd // LANES),
                                       _tile_lanes(sq_ref[...], hd // LANES), half))
    kv = proj[:, hd:hd + 6 * kvw]
    for c in range(cmp_ref.shape[0]):
        cmp_ref[c] = kv[:, c * LANES:(c + 1) * LANES]
    ck = _tile_lanes(ck_ref[...], kvw // LANES)
    sk = _tile_lanes(sk_ref[...], kvw // LANES)
    rows = x_ref.shape[0]
    tpos = (pl.program_id(0) % seq_tiles) * rows + lax.broadcasted_iota(jnp.int32, (rows, LANES), 0)
    lane = lax.broadcasted_iota(jnp.int32, (rows, LANES), 1)
    block_onehot = jnp.where(lane - NSA_HEAD_DIM == tpos // SLC_LEN, 1.0, 0.0)
    ks_ref[...] = _group_slabs(_rotate(kv[:, 2 * kvw:3 * kvw], ck, sk, half), block_onehot).astype(ks_ref.dtype)
    one_lane = jnp.where(lane == NSA_HEAD_DIM, 1.0, 0.0)
    kw_ref[...] = _group_slabs(_rotate(kv[:, 4 * kvw:5 * kvw], ck, sk, half), one_lane).astype(kw_ref.dtype)
    _store_values_t(kv[:, 3 * kvw:4 * kvw], vs_ref)
    _store_values_t(kv[:, 5 * kvw:6 * kvw], vw_ref)
    gt_ref[...] = jnp.transpose(jax.nn.sigmoid(proj[:, hd + 6 * kvw:]))


def _proj_nsa(x2, g, w, seq):
    t, d = x2.shape
    hd = NSA_HEADS * NSA_HEAD_DIM
    kvw = NSA_KV_GROUPS * NSA_HEAD_DIM
    gw = NSA_KV_GROUPS * LANES
    n = w.shape[1]
    ns = seq // TOK_TILE
    pos = np.arange(seq)
    cq, sq = _rot_tables(pos, ROT_DIM, ROPE_THETA, NSA_HEAD_DIM ** -0.5 * LOG2E)
    ck, sk = _rot_tables(pos, ROT_DIM, ROPE_THETA, 1.0)
    row = lambda width: pl.BlockSpec((TOK_TILE, width), lambda i: (i, 0))
    tab = pl.BlockSpec((TOK_TILE, LANES), lambda i: (i % ns, 0))
    chunks = TOK_TILE // LANES
    val_t = (pl.BlockSpec((1, NSA_KV_GROUPS, chunks, V_ROWS, LANES), lambda i: (i // ns, 0, i % ns, 0, 0)),
             jax.ShapeDtypeStruct((t // seq, NSA_KV_GROUPS, seq // LANES, V_ROWS, LANES), BF16))
    tok = lambda wd, dt: (row(wd), jax.ShapeDtypeStruct((t, wd), dt))
    cmp_planes = (pl.BlockSpec((2 * kvw // LANES, TOK_TILE, LANES), lambda i: (0, i, 0)),
                  jax.ShapeDtypeStruct((2 * kvw // LANES, t, LANES), F32))
    tok_t = lambda wd: (pl.BlockSpec((wd, TOK_TILE), lambda i: (0, i)), jax.ShapeDtypeStruct((wd, t), F32))
    outs = [tok_t(hd), cmp_planes, tok(gw, BF16), val_t, tok(gw, BF16), val_t, tok_t(gw)]
    return pl.pallas_call(
        functools.partial(_proj_nsa_kernel, seq_tiles=ns),
        grid=(t // TOK_TILE,),
        in_specs=[row(d), pl.BlockSpec((1, d), lambda i: (0, 0)),
                  pl.BlockSpec((d, n), lambda i: (0, 0), pipeline_mode=pl.Buffered(1)),
                  tab, tab, tab, tab],
        out_specs=[spec for spec, _ in outs],
        out_shape=[shape for _, shape in outs],
        compiler_params=_cparams(("parallel",)),
        name="proj_nsa",
    )(x2, g, w, cq, sq, ck, sk)


def _compress_kernel(x_ref, pe_ref, w1_ref, wc_ref, w2k_ref, w2vt_ref, cos_ref, sgn_ref, kc_ref, vct_ref):
    n = kc_ref.shape[2]
    dh = NSA_HEAD_DIM
    for kv in range(2):
        pe_term = _dot(pe_ref[kv].astype(BF16), w1_ref[kv])[0:1, :]
        pe_term = jnp.concatenate([pe_term, pe_term], axis=1)
        for cp in range(NSA_KV_GROUPS // 2):
            col = kv * (NSA_KV_GROUPS // 2) + cp
            acc = jnp.zeros((n, 2 * LANES), F32)
            for j in range(CMP_STRIDE):
                rows = x_ref[col, pl.ds(j, n, stride=CMP_STRIDE), :]
                acc = acc + _dot(rows.astype(BF16), wc_ref[kv, j])
            hid = jax.nn.gelu(acc[:, :LANES] + pltpu.roll(acc[:, LANES:], n - 1, axis=0) + pe_term).astype(BF16)
            if kv == 0:
                kc = _dot(hid, w2k_ref[...])
                kc = _rotate(kc, _tile_lanes(cos_ref[...], 2), _tile_lanes(sgn_ref[...], 2), ROT_DIM // 2)
                kc_ref[0, 2 * cp] = kc[:, :LANES].astype(kc_ref.dtype)
                kc_ref[0, 2 * cp + 1] = kc[:, LANES:].astype(kc_ref.dtype)
            else:
                vc_t = _dot_nt(w2vt_ref[...], hid)
                vct_ref[0, 2 * cp] = _values_t(vc_t[:dh, :], n)
                vct_ref[0, 2 * cp + 1] = _values_t(vc_t[dh:, :], n)


def _compress(cmp_in, batch, seq, pe, w1, wc, w2k, w2v_t, cos, sgn):
    g = NSA_KV_GROUPS
    n = seq // CMP_STRIDE
    full = lambda a: pl.BlockSpec(a.shape, lambda bi: (0,) * a.ndim)
    return pl.pallas_call(
        _compress_kernel,
        grid=(batch,),
        in_specs=[pl.BlockSpec((cmp_in.shape[0], seq, LANES), lambda bi: (0, bi, 0)),
                  full(pe), full(w1), full(wc), full(w2k), full(w2v_t), full(cos), full(sgn)],
        out_specs=[pl.BlockSpec((1, g, n, LANES), lambda bi: (bi, 0, 0, 0)),
                   pl.BlockSpec((1, g, V_ROWS, n), lambda bi: (bi, 0, 0, 0))],
        out_shape=[jax.ShapeDtypeStruct((batch, g, n, LANES), BF16),
                   jax.ShapeDtypeStruct((batch, g, V_ROWS, n), BF16)],
        compiler_params=_cparams(("parallel",)),
        name="compress",
    )(cmp_in, pe, w1, wc, w2k, w2v_t, cos, sgn)


def _topk_bias_t(imp_t, tpos, last_block, val_ref, cnt_ref):
    nb, tq = imp_t.shape
    j = lax.broadcasted_iota(jnp.int32, (nb, tq), 0)
    cur = tpos // SLC_LEN
    forced = (j == 0) | (j == cur) | (j == cur - 1)
    causal = j <= cur
    val_ref[...] = jnp.where(forced, BIG, jnp.where(causal, imp_t, NEG))
    cnt_ref[...] = jnp.zeros((nb, tq), F32)
    nchunk = nb // SUBLANES
    rows = lax.broadcasted_iota(jnp.int32, (SUBLANES, tq), 0)
    for c in range(nchunk):
        for ic in range(nchunk):
            @pl.when(max(c, ic) * SUBLANES <= last_block)
            def _():
                vc = val_ref[c * SUBLANES:(c + 1) * SUBLANES, :]
                vic = val_ref[ic * SUBLANES:(ic + 1) * SUBLANES, :]
                acc = cnt_ref[c * SUBLANES:(c + 1) * SUBLANES, :]
                for r in range(SUBLANES):
                    if c == ic and r == 0:
                        continue
                    vi = vic if r == 0 else pltpu.roll(vic, SUBLANES - r, axis=0)
                    if c > ic:
                        ahead = jnp.where(vi >= vc, 1.0, 0.0)
                    elif c < ic:
                        ahead = jnp.where(vi > vc, 1.0, 0.0)
                    else:
                        ahead = jnp.where(rows >= SUBLANES - r, jnp.where(vi >= vc, 1.0, 0.0),
                                          jnp.where(vi > vc, 1.0, 0.0))
                    acc = acc + ahead
                cnt_ref[c * SUBLANES:(c + 1) * SUBLANES, :] = acc
    return jnp.where((cnt_ref[...] < float(SLC_TOPK)) & causal, 0.0, NEG)


def _value_chunks_t(vt_ref, g, first, count):
    return jnp.concatenate([vt_ref[0, g, first + i] for i in range(count)], axis=1)


def _max_key_norms(k_ref, chunk):
    seq, width = k_ref.shape
    row = lax.broadcasted_iota(jnp.int32, (width, LANES), 0)
    lane = lax.broadcasted_iota(jnp.int32, (width, LANES), 1)
    dims_of_group = jnp.where((row // LANES == lane) & (row % LANES < NSA_HEAD_DIM), 1.0, 0.0).astype(BF16)
    best = jnp.zeros((1, LANES), F32)
    for i in range(seq // chunk):
        k = k_ref[i * chunk:(i + 1) * chunk, :].astype(F32)
        sq_hi, sq_lo = _split_bf16(k * k)
        norm2 = _dot(sq_hi, dims_of_group) + _dot(sq_lo, dims_of_group)
        best = jnp.maximum(best, jnp.max(norm2, axis=0, keepdims=True))
    return jnp.sqrt(best) * NORM_MARGIN


def _nsa_attn_kernel(q_ref, gt_ref, kc_ref, vct_ref, ks_ref, vst_ref, kw_ref, vwt_ref, c2s_ref, o_ref,
                     val_ref, cnt_ref, acc_ref, kmax_ref, *, n_slc, tk):
    tq = q_ref.shape[1]
    ncmp = kc_ref.shape[2]
    ngrp = NSA_KV_GROUPS
    nrep = NSA_HEADS // ngrp
    dh = NSA_HEAD_DIM
    groups = range(ngrp)
    t0 = pl.program_id(1) * tq
    head = lambda x, r: x[:, r * tq:(r + 1) * tq]
    grp_lanes = lambda g: slice(g * LANES, (g + 1) * LANES)

    @pl.when(pl.program_id(1) == 0)
    def _():
        kmax_ref[0:1, :] = _max_key_norms(ks_ref, tk)
        kmax_ref[1:2, :] = _max_key_norms(kw_ref, tk)

    tpos = t0 + lax.broadcasted_iota(jnp.int32, (1, tq), 1)
    q_t = []
    for g in groups:
        cols = [q_ref[(g * nrep + r) * dh:(g * nrep + r + 1) * dh, :] for r in range(nrep)]
        q_t.append(jnp.concatenate([jnp.concatenate(cols, axis=1), jnp.zeros((LANES - dh, nrep * tq), F32)], axis=0))

    cmp_end = lax.broadcasted_iota(jnp.int32, (ncmp, tq), 0) * CMP_STRIDE + (CMP_LEN - 1)
    vis_bias = jnp.where(cmp_end <= tpos, 0.0, NEG)
    any_vis = jnp.where(tpos >= CMP_LEN - 1, 1.0, 0.0)
    c2s = c2s_ref[...]
    o_cmp, imp_t = [], []
    for g in groups:
        s = _dot(kc_ref[0, g], q_t[g].astype(BF16))
        p_sum = jnp.zeros((ncmp, tq), F32)
        p_cmp = []
        for r in range(nrep):
            sr = head(s, r) + vis_bias
            e = jnp.exp2(sr - jnp.max(sr, axis=0, keepdims=True))
            pr = e * (any_vis / jnp.sum(e, axis=0, keepdims=True))
            p_sum = p_sum + pr
            p_cmp.append(pr.astype(BF16))
        o_cmp.append(_dot(vct_ref[0, g], jnp.concatenate(p_cmp, axis=1)))
        p_hi, p_lo = _split_bf16(p_sum)
        imp_t.append((_dot(c2s, p_hi) + _dot(c2s, p_lo))[:n_slc, :])

    bias_t = _topk_bias_t(jnp.concatenate(imp_t, axis=1), jnp.concatenate([tpos] * ngrp, axis=1),
                          (t0 + tq - 1) // SLC_LEN, val_ref, cnt_ref)
    if n_slc < dh:
        bias_t = jnp.concatenate([bias_t, jnp.full((dh - n_slc, ngrp * tq), NEG, F32)], axis=0)
    bias_rows = [jnp.concatenate([head(bias_t, g)] * nrep, axis=1) for g in groups]

    q_norm = [jnp.sqrt(jnp.sum(q_t[g] * q_t[g], axis=0, keepdims=True)) for g in groups]
    shift_s = [q_norm[g] * kmax_ref[0:1, g:g + 1] for g in groups]
    shift_w = [q_norm[g] * kmax_ref[1:2, g:g + 1] for g in groups]
    worst = functools.reduce(jnp.maximum, shift_s + shift_w)
    fixed_shift_ok = jnp.max(worst) < MAX_FIXED_SHIFT

    last = (t0 + tq - 1) // tk
    k_last = last * tk + lax.broadcasted_iota(jnp.int32, (tk, tq), 0)
    causal_bias = jnp.where(k_last <= tpos, 0.0, NEG)
    span = WINDOW + tq
    start = pl.multiple_of(jnp.maximum(t0 - WINDOW, 0), tq)
    dist = tpos - (start + lax.broadcasted_iota(jnp.int32, (span, tq), 0))
    band_bias = jnp.where((dist >= 0) & (dist < WINDOW), 0.0, NEG)
    key_tile = lambda c, g: ks_ref[pl.ds(pl.multiple_of(c * tk, tk), tk), grp_lanes(g)]
    val_tile = lambda c, g: _value_chunks_t(vst_ref, g, c * (tk // LANES), tk // LANES)
    win_keys = lambda g: kw_ref[pl.ds(start, span), grp_lanes(g)]
    win_vals = lambda g: _value_chunks_t(vwt_ref, g, start // LANES, span // LANES)
    all_heads = lambda x: jnp.concatenate([x] * nrep, axis=1)

    @pl.when(fixed_shift_ok)
    def _():
        qt_sel = [jnp.concatenate([q_t[g][:dh, :], bias_rows[g] - shift_s[g]], axis=0).astype(BF16) for g in groups]

        def partial_pv(c, g, token_bias=None, nkeys=tk):
            k0 = pl.multiple_of(c * tk, tk)
            sc = _dot(ks_ref[pl.ds(k0, nkeys), grp_lanes(g)], qt_sel[g])
            if token_bias is not None:
                sc = sc + all_heads(token_bias[:nkeys, :])
            vals = _value_chunks_t(vst_ref, g, c * (tk // LANES), nkeys // LANES)
            return _dot(vals, jnp.exp2(sc).astype(BF16))

        def body(i, accs):
            return tuple(accs[g] + partial_pv(2 * i, g) + partial_pv(2 * i + 1, g) for g in groups)

        accs = lax.fori_loop(0, last // 2, body, tuple(jnp.zeros((V_ROWS, nrep * tq), F32) for _ in groups))
        for g in groups:
            acc_ref[0, g] = accs[g]

        @pl.when(last % 2 == 1)
        def _():
            for g in groups:
                acc_ref[0, g] += partial_pv(last - 1, g)

        needs_half = (t0 % tk) + tq <= tk // 2

        @pl.when(needs_half)
        def _():
            for g in groups:
                acc_ref[0, g] += partial_pv(last, g, causal_bias, tk // 2)

        @pl.when(jnp.logical_not(needs_half))
        def _():
            for g in groups:
                acc_ref[0, g] += partial_pv(last, g, causal_bias)

        zeros_hi = jnp.zeros((dh - 1, nrep * tq), F32)
        for g in groups:
            qt_win = jnp.concatenate([q_t[g][:dh, :], -shift_w[g], zeros_hi], axis=0).astype(BF16)
            p = jnp.exp2(_dot(win_keys(g), qt_win) + all_heads(band_bias)).astype(BF16)
            acc_ref[1, g] = _dot(win_vals(g), p)

    @pl.when(jnp.logical_not(fixed_shift_ok))
    def _():
        qt_sel = [jnp.concatenate([q_t[g][:dh, :], bias_rows[g]], axis=0).astype(BF16) for g in groups]

        def tile(c, carry, token_bias):
            out = []
            for g in groups:
                m, acc = carry[g]
                sc = _dot(key_tile(c, g), qt_sel[g])
                ps, ms, alphas = [], [], []
                for r in range(nrep):
                    sr = head(sc, r)
                    if token_bias is not None:
                        sr = sr + token_bias
                    m_new = jnp.maximum(head(m, r), jnp.max(sr, axis=0, keepdims=True))
                    alphas.append(jnp.exp2(head(m, r) - m_new))
                    ms.append(m_new)
                    ps.append(jnp.exp2(sr - m_new).astype(BF16))
                pv = _dot(val_tile(c, g), jnp.concatenate(ps, axis=1))
                out.append((jnp.concatenate(ms, axis=1), jnp.concatenate(alphas, axis=1) * acc + pv))
            return tuple(out)

        init = tuple((jnp.full((1, nrep * tq), NEG, F32), jnp.zeros((V_ROWS, nrep * tq), F32)) for _ in groups)
        carry = lax.fori_loop(0, last, lambda c, cr: tile(c, cr, None), init)
        carry = tile(last, carry, causal_bias)
        for g in groups:
            acc_ref[0, g] = carry[g][1]
            sw = _dot(win_keys(g), q_t[g].astype(BF16))
            p_win = []
            for r in range(nrep):
                sr = head(sw, r) + band_bias
                p_win.append(jnp.exp2(sr - jnp.max(sr, axis=0, keepdims=True)).astype(BF16))
            acc_ref[1, g] = _dot(win_vals(g), jnp.concatenate(p_win, axis=1))

    gt_t = gt_ref[...]
    heads = []
    for g in groups:
        acc_s = acc_ref[0, g]
        acc_w = acc_ref[1, g]
        o_slc = acc_s[:dh, :] * (1.0 / acc_s[dh:dh + 1, :])
        o_win = acc_w[:dh, :] * (1.0 / acc_w[dh:dh + 1, :])
        for r in range(nrep):
            row = g * LANES + 3 * r
            heads.append(gt_t[row:row + 1, :] * head(o_cmp[g], r)[:dh, :]
                         + gt_t[row + 1:row + 2, :] * head(o_slc, r) + gt_t[row + 2:row + 3, :] * head(o_win, r))
    o_ref[...] = jnp.transpose(jnp.concatenate(heads, axis=0)).astype(o_ref.dtype)


def _nsa_attention(q, gt, kc, vc_t, ks, vs_t, kw, vw_t, batch, seq):
    g = NSA_KV_GROUPS
    nq = seq // ATT_TQ
    qw = (NSA_HEADS // g) * NSA_HEAD_DIM
    n_slc = seq // SLC_LEN
    n_cmp = (seq - CMP_LEN) // CMP_STRIDE + 1
    n_cmp_pad = kc.shape[2]
    c_start = np.arange(n_cmp)[:, None] * CMP_STRIDE
    s_start = np.arange(n_slc)[None, :] * SLC_LEN
    overlap = np.clip(np.minimum(c_start + CMP_LEN, s_start + SLC_LEN)
                      - np.maximum(c_start, s_start), 0, None) / CMP_LEN
    assert n_slc <= NSA_HEAD_DIM and n_slc % SUBLANES == 0 and seq % ATT_TK == 0
    c2s_t = np.zeros((NSA_HEAD_DIM, n_cmp_pad))
    c2s_t[:n_slc, :n_cmp] = overlap.T
    hd = NSA_HEADS * NSA_HEAD_DIM
    tok_spec = lambda width: pl.BlockSpec((ATT_TQ, width), lambda b, qi: (b * nq + qi, 0))
    key_spec = pl.BlockSpec((seq, g * LANES), lambda b, qi: (b, 0))
    val_spec = pl.BlockSpec((1, g, seq // LANES, V_ROWS, LANES), lambda b, qi: (b, 0, 0, 0, 0))
    return pl.pallas_call(
        functools.partial(_nsa_attn_kernel, n_slc=n_slc, tk=ATT_TK),
        grid=(batch, nq),
        in_specs=[pl.BlockSpec((hd, ATT_TQ), lambda b, qi: (0, b * nq + qi)),
                  pl.BlockSpec((g * LANES, ATT_TQ), lambda b, qi: (0, b * nq + qi)),
                  pl.BlockSpec((1, g, n_cmp_pad, LANES), lambda b, qi: (b, 0, 0, 0)),
                  pl.BlockSpec((1, g, V_ROWS, n_cmp_pad), lambda b, qi: (b, 0, 0, 0)),
                  key_spec, val_spec, key_spec, val_spec,
                  pl.BlockSpec(c2s_t.shape, lambda b, qi: (0, 0))],
        out_specs=tok_spec(hd),
        out_shape=jax.ShapeDtypeStruct((batch * seq, hd), BF16),
        scratch_shapes=[pltpu.VMEM((n_slc, g * ATT_TQ), F32), pltpu.VMEM((n_slc, g * ATT_TQ), F32),
                        pltpu.VMEM((2, g, V_ROWS, (NSA_HEADS // g) * ATT_TQ), F32),
                        pltpu.VMEM((2, LANES), F32)],
        compiler_params=_cparams(("parallel", "arbitrary")),
        name="nsa_attention",
    )(q, gt, kc, vc_t, ks, vs_t, kw, vw_t, jnp.asarray(c2s_t, BF16))


def _block_diag(w):
    nb, bi, bo = w.shape
    eye = jnp.eye(nb, dtype=w.dtype)
    return (eye[:, None, :, None] * w[:, :, None, :]).reshape(nb * bi, nb * bo)


def _nsa_weight(w_in):
    hd = NSA_HEADS * NSA_HEAD_DIM
    kvw = NSA_KV_GROUPS * NSA_HEAD_DIM
    nrep = NSA_HEADS // NSA_KV_GROUPS
    d = w_in.shape[0]
    gates = w_in[:, hd + 6 * kvw:].reshape(d, NSA_KV_GROUPS, 3 * nrep)
    gates = jnp.pad(gates, ((0, 0), (0, 0), (0, LANES - 3 * nrep))).reshape(d, NSA_KV_GROUPS * LANES)
    return jnp.concatenate([w_in[:, :hd + 6 * kvw], gates], axis=1).astype(BF16)


def _ab_mixer(h, g_attn, w_in, conv_w, conv_b, ga_w, ga_b, gx_w, gx_b, lam, batch, seq):
    proj = _proj_ab(h, g_attn, w_in.astype(BF16))
    gate_w = jnp.concatenate([_block_diag(ga_w), _block_diag(gx_w)], axis=1).astype(BF16)
    gate_b = jnp.concatenate([ga_b, gx_b])[None, :]
    lru_out = _lru(proj, batch, seq, conv_w, conv_b[None, :], gate_w, gate_b, lam[None, :])
    return [lru_out, _retention(proj, batch, seq)]


def _nsa_mixer(h, g_attn, w_in, pe_k, k_w1, k_w2, pe_v, v_w1, v_w2, batch, seq):
    q, cmp_in, ks, vs_t, kw, vw_t, gt = _proj_nsa(h, g_attn, _nsa_weight(w_in), seq)
    dh = NSA_HEAD_DIM
    n_blk = seq // CMP_STRIDE
    pe = jnp.broadcast_to(jnp.stack([pe_k, pe_v]).reshape(2, 1, CMP_LEN * dh), (2, SUBLANES, CMP_LEN * dh))
    w1 = jnp.stack([k_w1, v_w1]).astype(BF16)
    w1r = w1.reshape(2, CMP_LEN, dh, dh)
    first, second, zero = w1r[:, :CMP_STRIDE], w1r[:, CMP_STRIDE:], jnp.zeros_like(w1r[:, :CMP_STRIDE])
    wc = jnp.concatenate([jnp.concatenate([first, zero, second, zero], axis=-1),
                          jnp.concatenate([zero, first, zero, second], axis=-1)], axis=-2)
    w2k = k_w2.astype(BF16)
    z = jnp.zeros_like(w2k)
    w2k = jnp.concatenate([jnp.concatenate([w2k, z, z, z], axis=1), jnp.concatenate([z, z, w2k, z], axis=1)], axis=0)
    w2v_t = v_w2.T.astype(BF16)
    zt = jnp.zeros_like(w2v_t)
    w2v_t = jnp.concatenate([jnp.concatenate([w2v_t, zt], axis=1), jnp.concatenate([zt, w2v_t], axis=1)], axis=0)
    cos, sgn = _rot_tables(np.arange(n_blk) * CMP_STRIDE + CMP_LEN - 1, ROT_DIM, ROPE_THETA, 1.0)
    kc, vc_t = _compress(cmp_in, batch, seq, pe, w1, wc, w2k, w2v_t, cos, sgn)
    return [_nsa_attention(q, gt, kc, vc_t, ks, vs_t, kw, vw_t, batch, seq)]


def kernel(x, attn_norm, ab_w_in, conv_w, conv_b, gate_a_w, gate_a_b, gate_x_w, gate_x_b, lru_lambda,
           ab_w_out, nsa_w_in, cmp_pe_k, cmp_k_w1, cmp_k_w2, cmp_pe_v, cmp_v_w1, cmp_v_w2, nsa_w_out,
           ffn_norm, ffn_w1, ffn_w3, ffn_w2, final_norm):
    batch, seq, d = x.shape
    depth = attn_norm.shape[0]
    h = x.reshape(batch * seq, d)
    for layer in range(depth):
        i = layer // 2
        g_attn = attn_norm[layer][None, :]
        if layer % 2 == 0:
            mixes = _ab_mixer(h, g_attn, ab_w_in[i], conv_w[i], conv_b[i], gate_a_w[i], gate_a_b[i],
                              gate_x_w[i], gate_x_b[i], lru_lambda[i], batch, seq)
            w_out = ab_w_out[i]
        else:
            mixes = _nsa_mixer(h, g_attn, nsa_w_in[i], cmp_pe_k[i], cmp_k_w1[i], cmp_k_w2[i],
                               cmp_pe_v[i], cmp_v_w1[i], cmp_v_w2[i], batch, seq)
            w_out = nsa_w_out[i]
        last = layer == depth - 1
        h = _mix_ffn(h, mixes, w_out.astype(BF16), ffn_norm[layer][None, :],
                     ffn_w1[layer].astype(BF16), ffn_w3[layer].astype(BF16), ffn_w2[layer].astype(BF16),
                     final_g=final_norm[None, :] if last else None)
    return h.reshape(batch, seq, d)
```

```python
import functools

import numpy as np
import jax
import jax.numpy as jnp
from jax import lax
from jax.experimental import pallas as pl
from jax.experimental.pallas import tpu as pltpu

F32 = jnp.float32
BF16 = jnp.bfloat16

EPS = 1e-6
NEG = -1e30
BIG = 1e30
LOG2E = float(np.log2(np.e))

LRU_BLOCKS = 8
CONV_WIDTH = 4
LRU_C = 8.0
RET_HEADS = 8
RET_DIM = 64
RET_THETA = 10000.0
NSA_HEADS = 16
NSA_KV_GROUPS = 4
NSA_HEAD_DIM = 64
CMP_LEN = 32
CMP_STRIDE = 16
SLC_LEN = 64
SLC_TOPK = 16
WINDOW = 512
ROPE_THETA = 500000.0
ROT_DIM = NSA_HEAD_DIM // 4

LANES = 128
SUBLANES = 8
VMEM_LIMIT = 56 * 1024 * 1024

RET_CHUNK = 256
TOK_TILE = 512
LRU_TILE = 512
ATT_TQ = 256
ATT_TK = 512
V_ROWS = NSA_HEAD_DIM + 16
MAX_FIXED_SHIFT = 32.0
NORM_MARGIN = 1.01


def _cparams(sem):
    return pltpu.CompilerParams(dimension_semantics=sem, vmem_limit_bytes=VMEM_LIMIT)


def _dot(a, b):
    return jnp.dot(a, b, preferred_element_type=F32)


def _dot_nt(a, b):
    return lax.dot_general(a, b, (((1,), (1,)), ((), ())), preferred_element_type=F32)


def _rms(x, g):
    return x * lax.rsqrt(jnp.mean(x * x, axis=-1, keepdims=True) + EPS) * g


def _tile_lanes(t, reps):
    return t if reps == 1 else jnp.concatenate([t] * reps, axis=1)


def _rotate(x, cos, sgn, half):
    n = x.shape[1]
    lane = lax.broadcasted_iota(jnp.int32, x.shape, 1)
    first = (lane % NSA_HEAD_DIM) < half
    partner = jnp.where(first, pltpu.roll(x, n - half, axis=1), pltpu.roll(x, half, axis=1))
    return x * cos + partner * sgn


def _group_slabs(x, fill):
    lane = lax.broadcasted_iota(jnp.int32, (x.shape[0], LANES), 1)
    lo = lane < NSA_HEAD_DIM
    out = []
    for c in range(x.shape[1] // LANES):
        xc = x[:, c * LANES:(c + 1) * LANES]
        out.append(jnp.where(lo, xc, fill))
        out.append(jnp.where(lo, pltpu.roll(xc, NSA_HEAD_DIM, axis=1), fill))
    return jnp.concatenate(out, axis=1)


def _values_t(v_t, ncols):
    return jnp.concatenate([v_t, jnp.ones((V_ROWS - NSA_HEAD_DIM, ncols), F32)], axis=0).astype(BF16)


def _store_values_t(v, o_ref):
    rows = v.shape[0]
    v_t = jnp.transpose(v)
    for g in range(v.shape[1] // NSA_HEAD_DIM):
        slab = _values_t(v_t[g * NSA_HEAD_DIM:(g + 1) * NSA_HEAD_DIM, :], rows)
        for c in range(rows // LANES):
            o_ref[0, g, c] = slab[:, c * LANES:(c + 1) * LANES]


def _rot_tables(pos, rot_dim, theta, scale):
    half = rot_dim // 2
    inv = theta ** (-np.arange(0, rot_dim, 2, dtype=np.float64) / rot_dim)
    ang = np.asarray(pos, np.float64)[:, None] * inv
    cos = np.ones((len(pos), NSA_HEAD_DIM))
    sgn = np.zeros((len(pos), NSA_HEAD_DIM))
    cos[:, :half] = np.cos(ang)
    cos[:, half:rot_dim] = np.cos(ang)
    sgn[:, :half] = -np.sin(ang)
    sgn[:, half:rot_dim] = np.sin(ang)
    cos = np.tile(cos * scale, (1, 2))
    sgn = np.tile(sgn * scale, (1, 2))
    return jnp.asarray(cos, F32), jnp.asarray(sgn, F32)


def _ret_decay_tables(chunk):
    log_g = np.log1p(-(2.0 ** (-5.0 - np.arange(RET_HEADS, dtype=np.float64))))
    ci = np.arange(chunk, dtype=np.float64)
    diff = ci[:, None] - ci[None, :]
    inner = np.where(diff >= 0, np.exp(np.maximum(diff, 0.0) * log_g[:, None, None]), 0.0)
    lane_head = np.repeat(np.arange(RET_HEADS), RET_DIM)
    qdec = np.exp((ci[:, None] + 1.0) * log_g[lane_head][None, :])
    kdec = np.exp((chunk - 1.0 - ci[:, None]) * log_g[lane_head][None, :])
    cdec = np.exp(chunk * log_g[lane_head])[None, :]
    return (jnp.asarray(inner, F32), jnp.asarray(qdec, F32), jnp.asarray(kdec, F32),
            jnp.asarray(cdec, F32))


def _proj_ab_kernel(x_ref, g_ref, w_ref, o_ref):
    xn = _rms(x_ref[...], g_ref[...]).astype(BF16)
    o_ref[...] = _dot(xn, w_ref[...])


def _proj_ab(x2, g, w):
    t, d = x2.shape
    n = w.shape[1]
    return pl.pallas_call(
        _proj_ab_kernel,
        grid=(t // TOK_TILE,),
        in_specs=[pl.BlockSpec((TOK_TILE, d), lambda i: (i, 0)),
                  pl.BlockSpec((1, d), lambda i: (0, 0)),
                  pl.BlockSpec((d, n), lambda i: (0, 0), pipeline_mode=pl.Buffered(1))],
        out_specs=pl.BlockSpec((TOK_TILE, n), lambda i: (i, 0)),
        out_shape=jax.ShapeDtypeStruct((t, n), F32),
        compiler_params=_cparams(("parallel",)),
        name="proj_ab",
    )(x2, g, w)


def _lru_kernel(y_ref, x_ref, cw_ref, cb_ref, gw_ref, gb_ref, lam_ref, o_ref,
                xp_ref, a_ref, u_ref, h_ref, carry_ref):
    ts, w = x_ref.shape
    row = lax.broadcasted_iota(jnp.int32, (SUBLANES, w), 0)

    @pl.when(pl.program_id(1) == 0)
    def _():
        xp_ref[...] = jnp.zeros_like(xp_ref)
        carry_ref[...] = jnp.zeros_like(carry_ref)

    x = x_ref[...]
    prev = xp_ref[...]
    xc = cb_ref[...] + x * cw_ref[CONV_WIDTH - 1:CONV_WIDTH, :]
    for s in range(1, CONV_WIDTH):
        rolled = pltpu.roll(x, s, axis=0)
        head_rows = jnp.where(row < s, pltpu.roll(prev, s, axis=0), rolled[:SUBLANES, :])
        shifted = jnp.concatenate([head_rows, rolled[SUBLANES:, :]], axis=0)
        xc = xc + shifted * cw_ref[CONV_WIDTH - 1 - s:CONV_WIDTH - s, :]
    xp_ref[...] = x[ts - SUBLANES:, :]

    gates = _dot(xc.astype(BF16), gw_ref[...]) + gb_ref[...]
    r = jax.nn.sigmoid(gates[:, :w])
    i = jax.nn.sigmoid(gates[:, w:])
    log_a = -LRU_C * r * jax.nn.softplus(-lam_ref[...])
    a = jnp.exp(log_a)
    z = jnp.tanh(-log_a) * (a * a + 1.0)
    a_ref[...] = a
    u_ref[...] = jnp.where(z > 0.0, z * lax.rsqrt(z), 0.0) * (i * xc)

    def body(gi, carry):
        r0 = pl.multiple_of(gi * SUBLANES, SUBLANES)
        a = a_ref[pl.ds(r0, SUBLANES), :]
        u = u_ref[pl.ds(r0, SUBLANES), :]
        for d in (1, 2, 4):
            a_s = pltpu.roll(a, d, axis=0)
            u_s = pltpu.roll(u, d, axis=0)
            m = row >= d
            u = jnp.where(m, a * u_s + u, u)
            a = jnp.where(m, a * a_s, a)
        h = a * carry + u
        h_ref[pl.ds(r0, SUBLANES), :] = h
        return h[SUBLANES - 1:SUBLANES, :]

    carry_ref[...] = lax.fori_loop(0, ts // SUBLANES, body, carry_ref[...])
    o_ref[...] = (h_ref[...] * jax.nn.gelu(y_ref[...])).astype(o_ref.dtype)


def _lru(proj, batch, seq, conv_w, conv_b, gate_w, gate_b, lam):
    w = conv_w.shape[1]
    ns = seq // LRU_TILE
    return pl.pallas_call(
        _lru_kernel,
        grid=(batch, ns),
        in_specs=[pl.BlockSpec((LRU_TILE, w), lambda b, s: (b * ns + s, 0)),
                  pl.BlockSpec((LRU_TILE, w), lambda b, s: (b * ns + s, 1)),
                  pl.BlockSpec((CONV_WIDTH, w), lambda b, s: (0, 0)),
                  pl.BlockSpec((1, w), lambda b, s: (0, 0)),
                  pl.BlockSpec((w, 2 * w), lambda b, s: (0, 0)),
                  pl.BlockSpec((1, 2 * w), lambda b, s: (0, 0)),
                  pl.BlockSpec((1, w), lambda b, s: (0, 0))],
        out_specs=pl.BlockSpec((LRU_TILE, w), lambda b, s: (b * ns + s, 0)),
        out_shape=jax.ShapeDtypeStruct((batch * seq, w), BF16),
        scratch_shapes=[pltpu.VMEM((SUBLANES, w), F32),
                        pltpu.VMEM((LRU_TILE, w), F32),
                        pltpu.VMEM((LRU_TILE, w), F32),
                        pltpu.VMEM((LRU_TILE, w), F32),
                        pltpu.VMEM((1, w), F32)],
        compiler_params=_cparams(("parallel", "arbitrary")),
        name="rg_lru",
    )(proj, proj, conv_w, conv_b, gate_w, gate_b, lam)


def _split_bf16(x):
    hi = x.astype(BF16)
    lo = (x - hi.astype(F32)).astype(BF16)
    return hi, lo


def _ret_kernel(q_ref, k_ref, v_ref, g_ref, cq_ref, sq_ref, ck_ref, sk_ref,
                inner_ref, qdec_ref, kdec_ref, cdec_ref, avg_ref, o_ref, st_ref):
    c, w = q_ref.shape
    npair = w // LANES
    b = pl.program_id(1)

    @pl.when(pl.program_id(0) == 0)
    def _():
        st_ref[b] = jnp.zeros(st_ref.shape[1:], F32)

    half = RET_DIM // 2
    q = _rotate(q_ref[...], _tile_lanes(cq_ref[...], npair), _tile_lanes(sq_ref[...], npair), half)
    k = _rotate(k_ref[...], _tile_lanes(ck_ref[...], npair), _tile_lanes(sk_ref[...], npair), half)
    v = v_ref[...]
    qd = (q * qdec_ref[...]).astype(BF16)
    kd = k * kdec_ref[...]
    q = q.astype(BF16)
    kb = k.astype(BF16)
    vb = v.astype(BF16)

    lane = lax.broadcasted_iota(jnp.int32, (c, LANES), 1)
    lo = lane < RET_DIM
    rr = lax.broadcasted_iota(jnp.int32, (LANES, LANES), 0) < RET_DIM
    cc = lax.broadcasted_iota(jnp.int32, (LANES, LANES), 1) < RET_DIM
    same_head = rr == cc
    zero = jnp.zeros((c, LANES), BF16)

    outs = []
    for p in range(npair):
        sl = slice(p * LANES, (p + 1) * LANES)
        qp, kp, vp = q[:, sl], kb[:, sl], vb[:, sl]
        s_lo = _dot_nt(jnp.where(lo, qp, zero), kp) * inner_ref[2 * p]
        s_hi = _dot_nt(jnp.where(lo, zero, qp), kp) * inner_ref[2 * p + 1]
        o = jnp.where(lo, _dot(s_lo.astype(BF16), vp), _dot(s_hi.astype(BF16), vp))
        st = st_ref[b, p]
        o = o + _dot(qd[:, sl], st.astype(BF16))
        kv = _dot(jnp.transpose(kd[:, sl]).astype(BF16), vp)
        st_ref[b, p] = st * cdec_ref[:, sl] + jnp.where(same_head, kv, 0.0)
        outs.append(o)

    o_all = jnp.concatenate(outs, axis=0)
    avg = avg_ref[...]
    o_hi, o_lo = _split_bf16(o_all)
    mu = _dot(o_hi, avg) + _dot(o_lo, avg)
    dlt = o_all - mu
    d_hi, d_lo = _split_bf16(dlt * dlt)
    var = _dot(d_hi, avg) + _dot(d_lo, avg)
    y = dlt * lax.rsqrt(var + EPS)
    y = jnp.concatenate([y[p * c:(p + 1) * c, :] for p in range(npair)], axis=1)
    o_ref[...] = (y * jax.nn.silu(g_ref[...])).astype(o_ref.dtype)


def _retention(proj, batch, seq):
    w = RET_HEADS * RET_DIM
    c = RET_CHUNK
    nc = seq // c
    npair = w // LANES
    pos = np.arange(seq)
    cq, sq = _rot_tables(pos, RET_DIM, RET_THETA, 1.0)
    ck, sk = _rot_tables(pos, RET_DIM, RET_THETA, RET_DIM ** -0.5)
    inner, qdec, kdec, cdec = _ret_decay_tables(c)
    blk = np.arange(LANES) // RET_DIM
    avg = jnp.asarray((blk[:, None] == blk[None, :]) / RET_DIM, BF16)
    col0 = 2
    tok = lambda j: pl.BlockSpec((c, w), lambda ci, b, j=j: (b * nc + ci, j))
    tab = pl.BlockSpec((c, LANES), lambda ci, b: (ci, 0))
    const = lambda shape: pl.BlockSpec(shape, lambda ci, b: (0,) * len(shape))
    return pl.pallas_call(
        _ret_kernel,
        grid=(nc, batch),
        in_specs=[tok(col0), tok(col0 + 1), tok(col0 + 2), tok(col0 + 3),
                  tab, tab, tab, tab,
                  const((RET_HEADS, c, c)), const((c, w)), const((c, w)), const((1, w)),
                  const((LANES, LANES))],
        out_specs=pl.BlockSpec((c, w), lambda ci, b: (b * nc + ci, 0)),
        out_shape=jax.ShapeDtypeStruct((batch * seq, w), BF16),
        scratch_shapes=[pltpu.VMEM((batch, npair, LANES, LANES), F32)],
        compiler_params=_cparams(("arbitrary", "arbitrary")),
        name="retention",
    )(proj, proj, proj, proj, cq, sq, ck, sk, inner, qdec, kdec, cdec, avg)


def _mix_ffn_kernel(*refs, n_mix, final):
    h_ref = refs[0]
    mix_refs = refs[1:1 + n_mix]
    wo_ref, g_ref, w1_ref, w3_ref, w2_ref = refs[1 + n_mix:6 + n_mix]
    rest = refs[6 + n_mix:]
    fg_ref = rest[0] if final else None
    o_ref = rest[-1]

    mixed = mix_refs[0][...] if n_mix == 1 else jnp.concatenate([r[...] for r in mix_refs], axis=1)
    h1 = h_ref[...] + _dot(mixed, wo_ref[...])
    xn = _rms(h1, g_ref[...]).astype(BF16)
    act = (jax.nn.silu(_dot(xn, w1_ref[...])) * _dot(xn, w3_ref[...])).astype(BF16)
    h2 = h1 + _dot(act, w2_ref[...])
    if final:
        h2 = _rms(h2, fg_ref[...])
    o_ref[...] = h2


def _mix_ffn(h2d, mixes, wo, g, w1, w3, w2, final_g=None):
    t, d = h2d.shape
    hid = w1.shape[1]
    final = final_g is not None
    row = lambda width: pl.BlockSpec((TOK_TILE, width), lambda i: (i, 0))
    const = lambda shape: pl.BlockSpec(shape, lambda i: (0, 0), pipeline_mode=pl.Buffered(1))
    in_specs = [row(d)] + [row(m.shape[1]) for m in mixes] + [
        const(wo.shape), const((1, d)), const((d, hid)), const((d, hid)), const((hid, d))]
    args = [h2d, *mixes, wo, g, w1, w3, w2]
    if final:
        in_specs.append(const((1, d)))
        args.append(final_g)
    return pl.pallas_call(
        functools.partial(_mix_ffn_kernel, n_mix=len(mixes), final=final),
        grid=(t // TOK_TILE,),
        in_specs=in_specs,
        out_specs=row(d),
        out_shape=jax.ShapeDtypeStruct((t, d), F32),
        compiler_params=_cparams(("parallel",)),
        name="mix_ffn_final" if final else "mix_ffn",
    )(*args)


def _proj_nsa_kernel(x_ref, g_ref, w_ref, cq_ref, sq_ref, ck_ref, sk_ref,
                     q_ref, cmp_ref, ks_ref, vs_ref, kw_ref, vw_ref, gt_ref, kn_ref, *, seq_tiles):
    hd = q_ref.shape[0]
    kvw = NSA_KV_GROUPS * NSA_HEAD_DIM
    xn = _rms(x_ref[...], g_ref[...]).astype(BF16)
    proj = _dot(xn, w_ref[...])
    half = ROT_DIM // 2
    q = proj[:, :hd]
    q_ref[...] = jnp.transpose(_rotate(q, _tile_lanes(cq_ref[...], hd // LANES),
                                       _tile_lanes(sq_ref[...], hd // LANES), half))
    kv = proj[:, hd:hd + 6 * kvw]
    for c in range(cmp_ref.shape[0]):
        cmp_ref[c] = kv[:, c * LANES:(c + 1) * LANES]
    ck = _tile_lanes(ck_ref[...], kvw // LANES)
    sk = _tile_lanes(sk_ref[...], kvw // LANES)
    rows = x_ref.shape[0]
    tpos = (pl.program_id(0) % seq_tiles) * rows + lax.broadcasted_iota(jnp.int32, (rows, LANES), 0)
    lane = lax.broadcasted_iota(jnp.int32, (rows, LANES), 1)
    block_onehot = jnp.where(lane - NSA_HEAD_DIM == tpos // SLC_LEN, 1.0, 0.0)
    ks = _group_slabs(_rotate(kv[:, 2 * kvw:3 * kvw], ck, sk, half), block_onehot).astype(ks_ref.dtype)
    one_lane = jnp.where(lane == NSA_HEAD_DIM, 1.0, 0.0)
    kw = _group_slabs(_rotate(kv[:, 4 * kvw:5 * kvw], ck, sk, half), one_lane).astype(kw_ref.dtype)
    ks_ref[...] = ks
    kw_ref[...] = kw
    kn_ref[0, 0] = _max_key_norm2(ks)
    kn_ref[0, 1] = _max_key_norm2(kw)
    _store_values_t(kv[:, 3 * kvw:4 * kvw], vs_ref)
    _store_values_t(kv[:, 5 * kvw:6 * kvw], vw_ref)
    gt_ref[...] = jnp.transpose(jax.nn.sigmoid(proj[:, hd + 6 * kvw:]))


def _proj_nsa(x2, g, w, seq):
    t, d = x2.shape
    hd = NSA_HEADS * NSA_HEAD_DIM
    kvw = NSA_KV_GROUPS * NSA_HEAD_DIM
    gw = NSA_KV_GROUPS * LANES
    n = w.shape[1]
    ns = seq // TOK_TILE
    pos = np.arange(seq)
    cq, sq = _rot_tables(pos, ROT_DIM, ROPE_THETA, NSA_HEAD_DIM ** -0.5 * LOG2E)
    ck, sk = _rot_tables(pos, ROT_DIM, ROPE_THETA, 1.0)
    row = lambda width: pl.BlockSpec((TOK_TILE, width), lambda i: (i, 0))
    tab = pl.BlockSpec((TOK_TILE, LANES), lambda i: (i % ns, 0))
    chunks = TOK_TILE // LANES
    val_t = (pl.BlockSpec((1, NSA_KV_GROUPS, chunks, V_ROWS, LANES), lambda i: (i // ns, 0, i % ns, 0, 0)),
             jax.ShapeDtypeStruct((t // seq, NSA_KV_GROUPS, seq // LANES, V_ROWS, LANES), BF16))
    tok = lambda wd, dt: (row(wd), jax.ShapeDtypeStruct((t, wd), dt))
    cmp_planes = (pl.BlockSpec((2 * kvw // LANES, TOK_TILE, LANES), lambda i: (0, i, 0)),
                  jax.ShapeDtypeStruct((2 * kvw // LANES, t, LANES), F32))
    tok_t = lambda wd: (pl.BlockSpec((wd, TOK_TILE), lambda i: (0, i)), jax.ShapeDtypeStruct((wd, t), F32))
    key_norms = (pl.BlockSpec((1, 2, SUBLANES, LANES), lambda i: (i, 0, 0, 0)),
                 jax.ShapeDtypeStruct((t // TOK_TILE, 2, SUBLANES, LANES), F32))
    outs = [tok_t(hd), cmp_planes, tok(gw, BF16), val_t, tok(gw, BF16), val_t, tok_t(LANES), key_norms]
    return pl.pallas_call(
        functools.partial(_proj_nsa_kernel, seq_tiles=ns),
        grid=(t // TOK_TILE,),
        in_specs=[row(d), pl.BlockSpec((1, d), lambda i: (0, 0)),
                  pl.BlockSpec((d, n), lambda i: (0, 0), pipeline_mode=pl.Buffered(1)),
                  tab, tab, tab, tab],
        out_specs=[spec for spec, _ in outs],
        out_shape=[shape for _, shape in outs],
        compiler_params=_cparams(("parallel",)),
        name="proj_nsa",
    )(x2, g, w, cq, sq, ck, sk)


def _compress_kernel(x_ref, pe_ref, w1_ref, wc_ref, w2k_ref, w2vt_ref, cos_ref, sgn_ref, kc_ref, vct_ref):
    n = kc_ref.shape[2]
    dh = NSA_HEAD_DIM
    for kv in range(2):
        pe_term = _dot(pe_ref[kv].astype(BF16), w1_ref[kv])[0:1, :]
        pe_term = jnp.concatenate([pe_term, pe_term], axis=1)
        for cp in range(NSA_KV_GROUPS // 2):
            col = kv * (NSA_KV_GROUPS // 2) + cp
            acc = jnp.zeros((n, 2 * LANES), F32)
            for j in range(CMP_STRIDE):
                rows = x_ref[col, pl.ds(j, n, stride=CMP_STRIDE), :]
                acc = acc + _dot(rows.astype(BF16), wc_ref[kv, j])
            hid = jax.nn.gelu(acc[:, :LANES] + pltpu.roll(acc[:, LANES:], n - 1, axis=0) + pe_term).astype(BF16)
            if kv == 0:
                kc = _dot(hid, w2k_ref[...])
                kc = _rotate(kc, _tile_lanes(cos_ref[...], 2), _tile_lanes(sgn_ref[...], 2), ROT_DIM // 2)
                kc_ref[0, 2 * cp] = kc[:, :LANES].astype(kc_ref.dtype)
                kc_ref[0, 2 * cp + 1] = kc[:, LANES:].astype(kc_ref.dtype)
            else:
                vc_t = _dot_nt(w2vt_ref[...], hid)
                vct_ref[0, 2 * cp] = _values_t(vc_t[:dh, :], n)
                vct_ref[0, 2 * cp + 1] = _values_t(vc_t[dh:, :], n)


def _compress(cmp_in, batch, seq, pe, w1, wc, w2k, w2v_t, cos, sgn):
    g = NSA_KV_GROUPS
    n = seq // CMP_STRIDE
    full = lambda a: pl.BlockSpec(a.shape, lambda bi: (0,) * a.ndim)
    return pl.pallas_call(
        _compress_kernel,
        grid=(batch,),
        in_specs=[pl.BlockSpec((cmp_in.shape[0], seq, LANES), lambda bi: (0, bi, 0)),
                  full(pe), full(w1), full(wc), full(w2k), full(w2v_t), full(cos), full(sgn)],
        out_specs=[pl.BlockSpec((1, g, n, LANES), lambda bi: (bi, 0, 0, 0)),
                   pl.BlockSpec((1, g, V_ROWS, n), lambda bi: (bi, 0, 0, 0))],
        out_shape=[jax.ShapeDtypeStruct((batch, g, n, LANES), BF16),
                   jax.ShapeDtypeStruct((batch, g, V_ROWS, n), BF16)],
        compiler_params=_cparams(("parallel",)),
        name="compress",
    )(cmp_in, pe, w1, wc, w2k, w2v_t, cos, sgn)


def _topk_bias_t(imp_t, tpos, last_block, val_ref, cnt_ref):
    nb, tq = imp_t.shape
    j = lax.broadcasted_iota(jnp.int32, (nb, tq), 0)
    cur = tpos // SLC_LEN
    forced = (j == 0) | (j == cur) | (j == cur - 1)
    causal = j <= cur
    val_ref[...] = jnp.where(forced, BIG, jnp.where(causal, imp_t, NEG))
    cnt_ref[...] = jnp.zeros((nb, tq), F32)
    nchunk = nb // SUBLANES
    rows = lax.broadcasted_iota(jnp.int32, (SUBLANES, tq), 0)
    for c in range(nchunk):
        for ic in range(nchunk):
            @pl.when(max(c, ic) * SUBLANES <= last_block)
            def _():
                vc = val_ref[c * SUBLANES:(c + 1) * SUBLANES, :]
                vic = val_ref[ic * SUBLANES:(ic + 1) * SUBLANES, :]
                acc = cnt_ref[c * SUBLANES:(c + 1) * SUBLANES, :]
                for r in range(SUBLANES):
                    if c == ic and r == 0:
                        continue
                    vi = vic if r == 0 else pltpu.roll(vic, SUBLANES - r, axis=0)
                    if c > ic:
                        ahead = jnp.where(vi >= vc, 1.0, 0.0)
                    elif c < ic:
                        ahead = jnp.where(vi > vc, 1.0, 0.0)
                    else:
                        ahead = jnp.where(rows >= SUBLANES - r, jnp.where(vi >= vc, 1.0, 0.0),
                                          jnp.where(vi > vc, 1.0, 0.0))
                    acc = acc + ahead
                cnt_ref[c * SUBLANES:(c + 1) * SUBLANES, :] = acc
    return jnp.where((cnt_ref[...] < float(SLC_TOPK)) & causal, 0.0, NEG)


def _value_chunks_t(vt_ref, g, first, count):
    return jnp.concatenate([vt_ref[0, g, first + i] for i in range(count)], axis=1)


def _max_key_norm2(k_slabs):
    width = k_slabs.shape[1]
    row = lax.broadcasted_iota(jnp.int32, (width, LANES), 0)
    lane = lax.broadcasted_iota(jnp.int32, (width, LANES), 1)
    dims_of_group = jnp.where((row // LANES == lane) & (row % LANES < NSA_HEAD_DIM), 1.0, 0.0).astype(BF16)
    k = k_slabs.astype(F32)
    norm2 = _dot((k * k).astype(BF16), dims_of_group)
    return jnp.broadcast_to(jnp.max(norm2, axis=0, keepdims=True), (SUBLANES, LANES))


def _nsa_attn_kernel(q_ref, gt_ref, kc_ref, vct_ref, ks_ref, vst_ref, kw_ref, vwt_ref, kn_ref, c2s_ref, o_ref,
                     val_ref, cnt_ref, imp_ref, acc_ref, *, n_slc, tk):
    tq = q_ref.shape[1]
    ncmp = kc_ref.shape[2]
    ngrp = NSA_KV_GROUPS
    nrep = NSA_HEADS // ngrp
    dh = NSA_HEAD_DIM
    groups = range(ngrp)
    t0 = pl.program_id(1) * tq
    head = lambda x, r: x[:, r * tq:(r + 1) * tq]
    grp_lanes = lambda g: slice(g * LANES, (g + 1) * LANES)

    kn = jnp.max(kn_ref[...], axis=0)
    kmax = [jnp.sqrt(kn[j, 0:1, :]) * NORM_MARGIN for j in range(2)]

    tpos = t0 + lax.broadcasted_iota(jnp.int32, (1, tq), 1)
    q_t = []
    for g in groups:
        cols = [q_ref[(g * nrep + r) * dh:(g * nrep + r + 1) * dh, :] for r in range(nrep)]
        q_t.append(jnp.concatenate([jnp.concatenate(cols, axis=1), jnp.zeros((LANES - dh, nrep * tq), F32)], axis=0))

    q_norm = [jnp.sqrt(jnp.sum(q_t[g] * q_t[g], axis=0, keepdims=True)) for g in groups]
    shift_s = [q_norm[g] * kmax[0][:, g:g + 1] for g in groups]
    shift_w = [q_norm[g] * kmax[1][:, g:g + 1] for g in groups]
    worst = functools.reduce(jnp.maximum, shift_s + shift_w)
    fixed_shift_ok = jnp.max(worst) < MAX_FIXED_SHIFT

    last = (t0 + tq - 1) // tk
    k_last = last * tk + lax.broadcasted_iota(jnp.int32, (tk, tq), 0)
    causal_bias = jnp.where(k_last <= tpos, 0.0, NEG)
    span = WINDOW + tq
    start = pl.multiple_of(jnp.maximum(t0 - WINDOW, 0), tq)
    dist = tpos - (start + lax.broadcasted_iota(jnp.int32, (span, tq), 0))
    band_bias = jnp.where((dist >= 0) & (dist < WINDOW), 0.0, NEG)
    key_tile = lambda c, g: ks_ref[pl.ds(pl.multiple_of(c * tk, tk), tk), grp_lanes(g)]
    val_tile = lambda c, g: _value_chunks_t(vst_ref, g, c * (tk // LANES), tk // LANES)
    win_keys = lambda g: kw_ref[pl.ds(start, span), grp_lanes(g)]
    win_vals = lambda g: _value_chunks_t(vwt_ref, g, start // LANES, span // LANES)
    all_heads = lambda x: jnp.concatenate([x] * nrep, axis=1)

    cmp_end = lax.broadcasted_iota(jnp.int32, (ncmp, tq), 0) * CMP_STRIDE + (CMP_LEN - 1)
    vis_bias = jnp.where(cmp_end <= tpos, 0.0, NEG)
    any_vis = jnp.where(tpos >= CMP_LEN - 1, 1.0, 0.0)

    def compressed_branch(g):
        c2s = c2s_ref[...]
        s = _dot(kc_ref[0, g], q_t[g].astype(BF16))
        p_sum = jnp.zeros((ncmp, tq), F32)
        p_cmp = []
        for r in range(nrep):
            sr = head(s, r) + vis_bias
            e = jnp.exp2(sr - jnp.max(sr, axis=0, keepdims=True))
            pr = e * (any_vis / jnp.sum(e, axis=0, keepdims=True))
            p_sum = p_sum + pr
            p_cmp.append(pr.astype(BF16))
        acc_ref[2, g] = _dot(vct_ref[0, g], jnp.concatenate(p_cmp, axis=1))
        p_hi, p_lo = _split_bf16(p_sum)
        imp_ref[:, g * tq:(g + 1) * tq] = (_dot(c2s, p_hi) + _dot(c2s, p_lo))[:n_slc, :]

    @pl.when(fixed_shift_ok)
    def _():
        zeros_hi = jnp.zeros((dh - 1, nrep * tq), F32)
        for g in groups:
            qt_win = jnp.concatenate([q_t[g][:dh, :], -shift_w[g], zeros_hi], axis=0).astype(BF16)
            p = jnp.exp2(_dot(win_keys(g), qt_win) + all_heads(band_bias)).astype(BF16)
            acc_ref[1, g] = _dot(win_vals(g), p)
            compressed_branch(g)

    @pl.when(jnp.logical_not(fixed_shift_ok))
    def _():
        for g in groups:
            compressed_branch(g)
            sw = _dot(win_keys(g), q_t[g].astype(BF16))
            p_win = []
            for r in range(nrep):
                sr = head(sw, r) + band_bias
                p_win.append(jnp.exp2(sr - jnp.max(sr, axis=0, keepdims=True)).astype(BF16))
            acc_ref[1, g] = _dot(win_vals(g), jnp.concatenate(p_win, axis=1))

    bias_t = _topk_bias_t(imp_ref[...], jnp.concatenate([tpos] * ngrp, axis=1),
                          (t0 + tq - 1) // SLC_LEN, val_ref, cnt_ref)
    if n_slc < dh:
        bias_t = jnp.concatenate([bias_t, jnp.full((dh - n_slc, ngrp * tq), NEG, F32)], axis=0)
    bias_rows = [jnp.concatenate([head(bias_t, g)] * nrep, axis=1) for g in groups]

    @pl.when(fixed_shift_ok)
    def _():
        qt_sel = [jnp.concatenate([q_t[g][:dh, :], bias_rows[g] - shift_s[g]], axis=0).astype(BF16) for g in groups]

        def partial_pv(c, g, token_bias=None, nkeys=tk):
            k0 = pl.multiple_of(c * tk, tk)
            sc = _dot(ks_ref[pl.ds(k0, nkeys), grp_lanes(g)], qt_sel[g])
            if token_bias is not None:
                sc = sc + all_heads(token_bias[:nkeys, :])
            vals = _value_chunks_t(vst_ref, g, c * (tk // LANES), nkeys // LANES)
            return _dot(vals, jnp.exp2(sc).astype(BF16))

        def body(i, accs):
            return tuple(accs[g] + partial_pv(2 * i, g) + partial_pv(2 * i + 1, g) for g in groups)

        accs = lax.fori_loop(0, last // 2, body, tuple(jnp.zeros((V_ROWS, nrep * tq), F32) for _ in groups))
        for g in groups:
            acc_ref[0, g] = accs[g]

        @pl.when(last % 2 == 1)
        def _():
            for g in groups:
                acc_ref[0, g] += partial_pv(last - 1, g)

        needs_half = (t0 % tk) + tq <= tk // 2

        @pl.when(needs_half)
        def _():
            for g in groups:
                acc_ref[0, g] += partial_pv(last, g, causal_bias, tk // 2)

        @pl.when(jnp.logical_not(needs_half))
        def _():
            for g in groups:
                acc_ref[0, g] += partial_pv(last, g, causal_bias)

    @pl.when(jnp.logical_not(fixed_shift_ok))
    def _():
        qt_sel = [jnp.concatenate([q_t[g][:dh, :], bias_rows[g]], axis=0).astype(BF16) for g in groups]

        def tile(c, carry, token_bias):
            out = []
            for g in groups:
                m, acc = carry[g]
                sc = _dot(key_tile(c, g), qt_sel[g])
                ps, ms, alphas = [], [], []
                for r in range(nrep):
                    sr = head(sc, r)
                    if token_bias is not None:
                        sr = sr + token_bias
                    m_new = jnp.maximum(head(m, r), jnp.max(sr, axis=0, keepdims=True))
                    alphas.append(jnp.exp2(head(m, r) - m_new))
                    ms.append(m_new)
                    ps.append(jnp.exp2(sr - m_new).astype(BF16))
                pv = _dot(val_tile(c, g), jnp.concatenate(ps, axis=1))
                out.append((jnp.concatenate(ms, axis=1), jnp.concatenate(alphas, axis=1) * acc + pv))
            return tuple(out)

        init = tuple((jnp.full((1, nrep * tq), NEG, F32), jnp.zeros((V_ROWS, nrep * tq), F32)) for _ in groups)
        carry = lax.fori_loop(0, last, lambda c, cr: tile(c, cr, None), init)
        carry = tile(last, carry, causal_bias)
        for g in groups:
            acc_ref[0, g] = carry[g][1]

    gt_t = gt_ref[...]
    heads = []
    for g in groups:
        acc_s = acc_ref[0, g]
        acc_w = acc_ref[1, g]
        o_cmp_g = acc_ref[2, g]
        o_slc = acc_s[:dh, :] * (1.0 / acc_s[dh:dh + 1, :])
        o_win = acc_w[:dh, :] * (1.0 / acc_w[dh:dh + 1, :])
        for r in range(nrep):
            row = 3 * (g * nrep + r)
            heads.append(gt_t[row:row + 1, :] * head(o_cmp_g, r)[:dh, :]
                         + gt_t[row + 1:row + 2, :] * head(o_slc, r) + gt_t[row + 2:row + 3, :] * head(o_win, r))
    o_ref[...] = jnp.transpose(jnp.concatenate(heads, axis=0)).astype(o_ref.dtype)


def _nsa_attention(q, gt, kc, vc_t, ks, vs_t, kw, vw_t, kn, batch, seq):
    g = NSA_KV_GROUPS
    tiles_per_row = kn.shape[0] // batch
    nq = seq // ATT_TQ
    qw = (NSA_HEADS // g) * NSA_HEAD_DIM
    n_slc = seq // SLC_LEN
    n_cmp = (seq - CMP_LEN) // CMP_STRIDE + 1
    n_cmp_pad = kc.shape[2]
    c_start = np.arange(n_cmp)[:, None] * CMP_STRIDE
    s_start = np.arange(n_slc)[None, :] * SLC_LEN
    overlap = np.clip(np.minimum(c_start + CMP_LEN, s_start + SLC_LEN)
                      - np.maximum(c_start, s_start), 0, None) / CMP_LEN
    assert n_slc <= NSA_HEAD_DIM and n_slc % SUBLANES == 0 and seq % ATT_TK == 0
    c2s_t = np.zeros((NSA_HEAD_DIM, n_cmp_pad))
    c2s_t[:n_slc, :n_cmp] = overlap.T
    hd = NSA_HEADS * NSA_HEAD_DIM
    tok_spec = lambda width: pl.BlockSpec((ATT_TQ, width), lambda b, qi: (b * nq + qi, 0))
    key_spec = pl.BlockSpec((seq, g * LANES), lambda b, qi: (b, 0))
    val_spec = pl.BlockSpec((1, g, seq // LANES, V_ROWS, LANES), lambda b, qi: (b, 0, 0, 0, 0))
    return pl.pallas_call(
        functools.partial(_nsa_attn_kernel, n_slc=n_slc, tk=ATT_TK),
        grid=(batch, nq),
        in_specs=[pl.BlockSpec((hd, ATT_TQ), lambda b, qi: (0, b * nq + qi)),
                  pl.BlockSpec((LANES, ATT_TQ), lambda b, qi: (0, b * nq + qi)),
                  pl.BlockSpec((1, g, n_cmp_pad, LANES), lambda b, qi: (b, 0, 0, 0)),
                  pl.BlockSpec((1, g, V_ROWS, n_cmp_pad), lambda b, qi: (b, 0, 0, 0)),
                  key_spec, val_spec, key_spec, val_spec,
                  pl.BlockSpec((tiles_per_row,) + kn.shape[1:], lambda b, qi: (b, 0, 0, 0)),
                  pl.BlockSpec(c2s_t.shape, lambda b, qi: (0, 0))],
        out_specs=tok_spec(hd),
        out_shape=jax.ShapeDtypeStruct((batch * seq, hd), BF16),
        scratch_shapes=[pltpu.VMEM((n_slc, g * ATT_TQ), F32)] * 3
        + [pltpu.VMEM((3, g, V_ROWS, (NSA_HEADS // g) * ATT_TQ), F32)],
        compiler_params=_cparams(("parallel", "arbitrary")),
        name="nsa_attention",
    )(q, gt, kc, vc_t, ks, vs_t, kw, vw_t, kn, jnp.asarray(c2s_t, BF16))


def _block_diag(w):
    nb, bi, bo = w.shape
    eye = jnp.eye(nb, dtype=w.dtype)
    return (eye[:, None, :, None] * w[:, :, None, :]).reshape(nb * bi, nb * bo)


def _nsa_weight(w_in):
    return jnp.pad(w_in, ((0, 0), (0, LANES - 3 * NSA_HEADS))).astype(BF16)


def _ab_mixer(h, g_attn, w_in, conv_w, conv_b, ga_w, ga_b, gx_w, gx_b, lam, batch, seq):
    proj = _proj_ab(h, g_attn, w_in.astype(BF16))
    gate_w = jnp.concatenate([_block_diag(ga_w), _block_diag(gx_w)], axis=1).astype(BF16)
    gate_b = jnp.concatenate([ga_b, gx_b])[None, :]
    lru_out = _lru(proj, batch, seq, conv_w, conv_b[None, :], gate_w, gate_b, lam[None, :])
    return [lru_out, _retention(proj, batch, seq)]


def _nsa_mixer(h, g_attn, w_in, pe_k, k_w1, k_w2, pe_v, v_w1, v_w2, batch, seq):
    q, cmp_in, ks, vs_t, kw, vw_t, gt, kn = _proj_nsa(h, g_attn, _nsa_weight(w_in), seq)
    dh = NSA_HEAD_DIM
    n_blk = seq // CMP_STRIDE
    pe = jnp.broadcast_to(jnp.stack([pe_k, pe_v]).reshape(2, 1, CMP_LEN * dh), (2, SUBLANES, CMP_LEN * dh))
    w1 = jnp.stack([k_w1, v_w1]).astype(BF16)
    w1r = w1.reshape(2, CMP_LEN, dh, dh)
    first, second, zero = w1r[:, :CMP_STRIDE], w1r[:, CMP_STRIDE:], jnp.zeros_like(w1r[:, :CMP_STRIDE])
    wc = jnp.concatenate([jnp.concatenate([first, zero, second, zero], axis=-1),
                          jnp.concatenate([zero, first, zero, second], axis=-1)], axis=-2)
    w2k = k_w2.astype(BF16)
    z = jnp.zeros_like(w2k)
    w2k = jnp.concatenate([jnp.concatenate([w2k, z, z, z], axis=1), jnp.concatenate([z, z, w2k, z], axis=1)], axis=0)
    w2v_t = v_w2.T.astype(BF16)
    zt = jnp.zeros_like(w2v_t)
    w2v_t = jnp.concatenate([jnp.concatenate([w2v_t, zt], axis=1), jnp.concatenate([zt, w2v_t], axis=1)], axis=0)
    cos, sgn = _rot_tables(np.arange(n_blk) * CMP_STRIDE + CMP_LEN - 1, ROT_DIM, ROPE_THETA, 1.0)
    kc, vc_t = _compress(cmp_in, batch, seq, pe, w1, wc, w2k, w2v_t, cos, sgn)
    return [_nsa_attention(q, gt, kc, vc_t, ks, vs_t, kw, vw_t, kn, batch, seq)]


def kernel(x, attn_norm, ab_w_in, conv_w, conv_b, gate_a_w, gate_a_b, gate_x_w, gate_x_b, lru_lambda,
           ab_w_out, nsa_w_in, cmp_pe_k, cmp_k_w1, cmp_k_w2, cmp_pe_v, cmp_v_w1, cmp_v_w2, nsa_w_out,
           ffn_norm, ffn_w1, ffn_w3, ffn_w2, final_norm):
    batch, seq, d = x.shape
    depth = attn_norm.shape[0]
    h = x.reshape(batch * seq, d)
    for layer in range(depth):
        i = layer // 2
        g_attn = attn_norm[layer][None, :]
        if layer % 2 == 0:
            mixes = _ab_mixer(h, g_attn, ab_w_in[i], conv_w[i], conv_b[i], gate_a_w[i], gate_a_b[i],
                              gate_x_w[i], gate_x_b[i], lru_lambda[i], batch, seq)
            w_out = ab_w_out[i]
        else:
            mixes = _nsa_mixer(h, g_attn, nsa_w_in[i], cmp_pe_k[i], cmp_k_w1[i], cmp_k_w2[i],
                               cmp_pe_v[i], cmp_v_w1[i], cmp_v_w2[i], batch, seq)
            w_out = nsa_w_out[i]
        last = layer == depth - 1
        h = _mix_ffn(h, mixes, w_out.astype(BF16), ffn_norm[layer][None, :],
                     ffn_w1[layer].astype(BF16), ffn_w3[layer].astype(BF16), ffn_w2[layer].astype(BF16),
                     final_g=final_norm[None, :] if last else None)
    return h.reshape(batch, seq, d)
```

```python
import functools

import numpy as np
import jax
import jax.numpy as jnp
from jax import lax
from jax.experimental import pallas as pl
from jax.experimental.pallas import tpu as pltpu

F32 = jnp.float32
BF16 = jnp.bfloat16

EPS = 1e-6
NEG = -1e30
BIG = 1e30
LOG2E = float(np.log2(np.e))

LRU_BLOCKS = 8
CONV_WIDTH = 4
LRU_C = 8.0
RET_HEADS = 8
RET_DIM = 64
RET_THETA = 10000.0
NSA_HEADS = 16
NSA_KV_GROUPS = 4
NSA_HEAD_DIM = 64
CMP_LEN = 32
CMP_STRIDE = 16
SLC_LEN = 64
SLC_TOPK = 16
WINDOW = 512
ROPE_THETA = 500000.0
ROT_DIM = NSA_HEAD_DIM // 4

LANES = 128
SUBLANES = 8
VMEM_LIMIT = 56 * 1024 * 1024

RET_CHUNK = 256
TOK_TILE = 512
PROJ_AB_TILE = 1024
LRU_TILE = 512
ATT_TQ = 256
ATT_TK = 512
V_ROWS = NSA_HEAD_DIM + 16
MAX_FIXED_SHIFT = 32.0
NORM_MARGIN = 1.01


def _cparams(sem):
    return pltpu.CompilerParams(dimension_semantics=sem, vmem_limit_bytes=VMEM_LIMIT)


def _dot(a, b):
    return jnp.dot(a, b, preferred_element_type=F32)


def _dot_nt(a, b):
    return lax.dot_general(a, b, (((1,), (1,)), ((), ())), preferred_element_type=F32)


def _rms(x, g):
    return x * lax.rsqrt(jnp.mean(x * x, axis=-1, keepdims=True) + EPS) * g


def _tile_lanes(t, reps):
    return t if reps == 1 else jnp.concatenate([t] * reps, axis=1)


def _rotate(x, cos, sgn, half):
    n = x.shape[1]
    lane = lax.broadcasted_iota(jnp.int32, x.shape, 1)
    first = (lane % NSA_HEAD_DIM) < half
    partner = jnp.where(first, pltpu.roll(x, n - half, axis=1), pltpu.roll(x, half, axis=1))
    return x * cos + partner * sgn


def _group_slabs(x, fill):
    lane = lax.broadcasted_iota(jnp.int32, (x.shape[0], LANES), 1)
    lo = lane < NSA_HEAD_DIM
    out = []
    for c in range(x.shape[1] // LANES):
        xc = x[:, c * LANES:(c + 1) * LANES]
        out.append(jnp.where(lo, xc, fill))
        out.append(jnp.where(lo, pltpu.roll(xc, NSA_HEAD_DIM, axis=1), fill))
    return jnp.concatenate(out, axis=1)


def _values_t(v_t, ncols):
    return jnp.concatenate([v_t, jnp.ones((V_ROWS - NSA_HEAD_DIM, ncols), F32)], axis=0).astype(BF16)


def _store_values_t(v, o_ref):
    rows = v.shape[0]
    v_t = jnp.transpose(v)
    for g in range(v.shape[1] // NSA_HEAD_DIM):
        slab = _values_t(v_t[g * NSA_HEAD_DIM:(g + 1) * NSA_HEAD_DIM, :], rows)
        for c in range(rows // LANES):
            o_ref[0, g, c] = slab[:, c * LANES:(c + 1) * LANES]


def _rot_tables(pos, rot_dim, theta, scale):
    half = rot_dim // 2
    inv = theta ** (-np.arange(0, rot_dim, 2, dtype=np.float64) / rot_dim)
    ang = np.asarray(pos, np.float64)[:, None] * inv
    cos = np.ones((len(pos), NSA_HEAD_DIM))
    sgn = np.zeros((len(pos), NSA_HEAD_DIM))
    cos[:, :half] = np.cos(ang)
    cos[:, half:rot_dim] = np.cos(ang)
    sgn[:, :half] = -np.sin(ang)
    sgn[:, half:rot_dim] = np.sin(ang)
    cos = np.tile(cos * scale, (1, 2))
    sgn = np.tile(sgn * scale, (1, 2))
    return jnp.asarray(cos, F32), jnp.asarray(sgn, F32)


def _ret_decay_tables(chunk):
    log_g = np.log1p(-(2.0 ** (-5.0 - np.arange(RET_HEADS, dtype=np.float64))))
    ci = np.arange(chunk, dtype=np.float64)
    diff = ci[:, None] - ci[None, :]
    inner = np.where(diff >= 0, np.exp(np.maximum(diff, 0.0) * log_g[:, None, None]), 0.0)
    lane_head = np.repeat(np.arange(RET_HEADS), RET_DIM)
    qdec = np.exp((ci[:, None] + 1.0) * log_g[lane_head][None, :])
    kdec = np.exp((chunk - 1.0 - ci[:, None]) * log_g[lane_head][None, :])
    cdec = np.exp(chunk * log_g[lane_head])[None, :]
    return (jnp.asarray(inner, F32), jnp.asarray(qdec, F32), jnp.asarray(kdec, F32),
            jnp.asarray(cdec, F32))


def _proj_ab_kernel(x_ref, g_ref, w_ref, o_ref):
    xn = _rms(x_ref[...], g_ref[...]).astype(BF16)
    o_ref[...] = _dot(xn, w_ref[...])


def _proj_ab(x2, g, w):
    t, d = x2.shape
    n = w.shape[1]
    return pl.pallas_call(
        _proj_ab_kernel,
        grid=(t // PROJ_AB_TILE,),
        in_specs=[pl.BlockSpec((PROJ_AB_TILE, d), lambda i: (i, 0)),
                  pl.BlockSpec((1, d), lambda i: (0, 0)),
                  pl.BlockSpec((d, n), lambda i: (0, 0), pipeline_mode=pl.Buffered(1))],
        out_specs=pl.BlockSpec((PROJ_AB_TILE, n), lambda i: (i, 0)),
        out_shape=jax.ShapeDtypeStruct((t, n), F32),
        compiler_params=_cparams(("parallel",)),
        name="proj_ab",
    )(x2, g, w)


def _lru_kernel(y_ref, x_ref, cw_ref, cb_ref, gw_ref, gb_ref, lam_ref, o_ref,
                xp_ref, a_ref, u_ref, h_ref, carry_ref):
    ts, w = x_ref.shape
    row = lax.broadcasted_iota(jnp.int32, (SUBLANES, w), 0)

    @pl.when(pl.program_id(1) == 0)
    def _():
        xp_ref[...] = jnp.zeros_like(xp_ref)
        carry_ref[...] = jnp.zeros_like(carry_ref)

    x = x_ref[...]
    prev = xp_ref[...]
    xc = cb_ref[...] + x * cw_ref[CONV_WIDTH - 1:CONV_WIDTH, :]
    for s in range(1, CONV_WIDTH):
        rolled = pltpu.roll(x, s, axis=0)
        head_rows = jnp.where(row < s, pltpu.roll(prev, s, axis=0), rolled[:SUBLANES, :])
        shifted = jnp.concatenate([head_rows, rolled[SUBLANES:, :]], axis=0)
        xc = xc + shifted * cw_ref[CONV_WIDTH - 1 - s:CONV_WIDTH - s, :]
    xp_ref[...] = x[ts - SUBLANES:, :]

    gates = _dot(xc.astype(BF16), gw_ref[...]) + gb_ref[...]
    r = jax.nn.sigmoid(gates[:, :w])
    i = jax.nn.sigmoid(gates[:, w:])
    log_a = -LRU_C * r * jax.nn.softplus(-lam_ref[...])
    a = jnp.exp(log_a)
    z = jnp.tanh(-log_a) * (a * a + 1.0)
    a_ref[...] = a
    u_ref[...] = jnp.where(z > 0.0, z * lax.rsqrt(z), 0.0) * (i * xc)

    def body(gi, carry):
        r0 = pl.multiple_of(gi * SUBLANES, SUBLANES)
        a = a_ref[pl.ds(r0, SUBLANES), :]
        u = u_ref[pl.ds(r0, SUBLANES), :]
        for d in (1, 2, 4):
            a_s = pltpu.roll(a, d, axis=0)
            u_s = pltpu.roll(u, d, axis=0)
            m = row >= d
            u = jnp.where(m, a * u_s + u, u)
            a = jnp.where(m, a * a_s, a)
        h = a * carry + u
        h_ref[pl.ds(r0, SUBLANES), :] = h
        return h[SUBLANES - 1:SUBLANES, :]

    carry_ref[...] = lax.fori_loop(0, ts // SUBLANES, body, carry_ref[...])
    o_ref[...] = (h_ref[...] * jax.nn.gelu(y_ref[...])).astype(o_ref.dtype)


def _lru(proj, batch, seq, conv_w, conv_b, gate_w, gate_b, lam):
    w = conv_w.shape[1]
    ns = seq // LRU_TILE
    return pl.pallas_call(
        _lru_kernel,
        grid=(batch, ns),
        in_specs=[pl.BlockSpec((LRU_TILE, w), lambda b, s: (b * ns + s, 0)),
                  pl.BlockSpec((LRU_TILE, w), lambda b, s: (b * ns + s, 1)),
                  pl.BlockSpec((CONV_WIDTH, w), lambda b, s: (0, 0)),
                  pl.BlockSpec((1, w), lambda b, s: (0, 0)),
                  pl.BlockSpec((w, 2 * w), lambda b, s: (0, 0)),
                  pl.BlockSpec((1, 2 * w), lambda b, s: (0, 0)),
                  pl.BlockSpec((1, w), lambda b, s: (0, 0))],
        out_specs=pl.BlockSpec((LRU_TILE, w), lambda b, s: (b * ns + s, 0)),
        out_shape=jax.ShapeDtypeStruct((batch * seq, w), BF16),
        scratch_shapes=[pltpu.VMEM((SUBLANES, w), F32),
                        pltpu.VMEM((LRU_TILE, w), F32),
                        pltpu.VMEM((LRU_TILE, w), F32),
                        pltpu.VMEM((LRU_TILE, w), F32),
                        pltpu.VMEM((1, w), F32)],
        compiler_params=_cparams(("parallel", "arbitrary")),
        name="rg_lru",
    )(proj, proj, conv_w, conv_b, gate_w, gate_b, lam)


def _split_bf16(x):
    hi = x.astype(BF16)
    lo = (x - hi.astype(F32)).astype(BF16)
    return hi, lo


def _ret_kernel(q_ref, k_ref, v_ref, g_ref, cq_ref, sq_ref, ck_ref, sk_ref,
                inner_ref, qdec_ref, kdec_ref, cdec_ref, avg_ref, o_ref, st_ref):
    c, w = q_ref.shape
    npair = w // LANES
    b = pl.program_id(1)

    @pl.when(pl.program_id(0) == 0)
    def _():
        st_ref[b] = jnp.zeros(st_ref.shape[1:], F32)

    half = RET_DIM // 2
    q = _rotate(q_ref[...], _tile_lanes(cq_ref[...], npair), _tile_lanes(sq_ref[...], npair), half)
    k = _rotate(k_ref[...], _tile_lanes(ck_ref[...], npair), _tile_lanes(sk_ref[...], npair), half)
    v = v_ref[...]
    qd = (q * qdec_ref[...]).astype(BF16)
    kd = k * kdec_ref[...]
    q = q.astype(BF16)
    kb = k.astype(BF16)
    vb = v.astype(BF16)

    lane = lax.broadcasted_iota(jnp.int32, (c, LANES), 1)
    lo = lane < RET_DIM
    rr = lax.broadcasted_iota(jnp.int32, (LANES, LANES), 0) < RET_DIM
    cc = lax.broadcasted_iota(jnp.int32, (LANES, LANES), 1) < RET_DIM
    same_head = rr == cc
    zero = jnp.zeros((c, LANES), BF16)

    outs = []
    for p in range(npair):
        sl = slice(p * LANES, (p + 1) * LANES)
        qp, kp, vp = q[:, sl], kb[:, sl], vb[:, sl]
        s_lo = _dot_nt(jnp.where(lo, qp, zero), kp) * inner_ref[2 * p]
        s_hi = _dot_nt(jnp.where(lo, zero, qp), kp) * inner_ref[2 * p + 1]
        o = jnp.where(lo, _dot(s_lo.astype(BF16), vp), _dot(s_hi.astype(BF16), vp))
        st = st_ref[b, p]
        o = o + _dot(qd[:, sl], st.astype(BF16))
        kv = _dot(jnp.transpose(kd[:, sl]).astype(BF16), vp)
        st_ref[b, p] = st * cdec_ref[:, sl] + jnp.where(same_head, kv, 0.0)
        outs.append(o)

    o_all = jnp.concatenate(outs, axis=0)
    avg = avg_ref[...]
    mu = _dot(o_all.astype(BF16), avg)
    dlt = o_all - mu
    var = _dot((dlt * dlt).astype(BF16), avg)
    y = dlt * lax.rsqrt(var + EPS)
    y = jnp.concatenate([y[p * c:(p + 1) * c, :] for p in range(npair)], axis=1)
    o_ref[...] = (y * jax.nn.silu(g_ref[...])).astype(o_ref.dtype)


def _retention(proj, batch, seq):
    w = RET_HEADS * RET_DIM
    c = RET_CHUNK
    nc = seq // c
    npair = w // LANES
    pos = np.arange(seq)
    cq, sq = _rot_tables(pos, RET_DIM, RET_THETA, 1.0)
    ck, sk = _rot_tables(pos, RET_DIM, RET_THETA, RET_DIM ** -0.5)
    inner, qdec, kdec, cdec = _ret_decay_tables(c)
    blk = np.arange(LANES) // RET_DIM
    avg = jnp.asarray((blk[:, None] == blk[None, :]) / RET_DIM, BF16)
    col0 = 2
    tok = lambda j: pl.BlockSpec((c, w), lambda ci, b, j=j: (b * nc + ci, j))
    tab = pl.BlockSpec((c, LANES), lambda ci, b: (ci, 0))
    const = lambda shape: pl.BlockSpec(shape, lambda ci, b: (0,) * len(shape))
    return pl.pallas_call(
        _ret_kernel,
        grid=(nc, batch),
        in_specs=[tok(col0), tok(col0 + 1), tok(col0 + 2), tok(col0 + 3),
                  tab, tab, tab, tab,
                  const((RET_HEADS, c, c)), const((c, w)), const((c, w)), const((1, w)),
                  const((LANES, LANES))],
        out_specs=pl.BlockSpec((c, w), lambda ci, b: (b * nc + ci, 0)),
        out_shape=jax.ShapeDtypeStruct((batch * seq, w), BF16),
        scratch_shapes=[pltpu.VMEM((batch, npair, LANES, LANES), F32)],
        compiler_params=_cparams(("arbitrary", "arbitrary")),
        name="retention",
    )(proj, proj, proj, proj, cq, sq, ck, sk, inner, qdec, kdec, cdec, avg)


def _mix_ffn_kernel(*refs, n_mix, final):
    h_ref = refs[0]
    mix_refs = refs[1:1 + n_mix]
    wo_ref, g_ref, w1_ref, w3_ref, w2_ref = refs[1 + n_mix:6 + n_mix]
    rest = refs[6 + n_mix:]
    fg_ref = rest[0] if final else None
    o_ref = rest[-1]

    mixed = mix_refs[0][...] if n_mix == 1 else jnp.concatenate([r[...] for r in mix_refs], axis=1)
    h1 = h_ref[...] + _dot(mixed, wo_ref[...])
    xn = _rms(h1, g_ref[...]).astype(BF16)
    act = (jax.nn.silu(_dot(xn, w1_ref[...])) * _dot(xn, w3_ref[...])).astype(BF16)
    h2 = h1 + _dot(act, w2_ref[...])
    if final:
        h2 = _rms(h2, fg_ref[...])
    o_ref[...] = h2


def _mix_ffn(h2d, mixes, wo, g, w1, w3, w2, final_g=None):
    t, d = h2d.shape
    hid = w1.shape[1]
    final = final_g is not None
    row = lambda width: pl.BlockSpec((TOK_TILE, width), lambda i: (i, 0))
    const = lambda shape: pl.BlockSpec(shape, lambda i: (0, 0), pipeline_mode=pl.Buffered(1))
    in_specs = [row(d)] + [row(m.shape[1]) for m in mixes] + [
        const(wo.shape), const((1, d)), const((d, hid)), const((d, hid)), const((hid, d))]
    args = [h2d, *mixes, wo, g, w1, w3, w2]
    if final:
        in_specs.append(const((1, d)))
        args.append(final_g)
    return pl.pallas_call(
        functools.partial(_mix_ffn_kernel, n_mix=len(mixes), final=final),
        grid=(t // TOK_TILE,),
        in_specs=in_specs,
        out_specs=row(d),
        out_shape=jax.ShapeDtypeStruct((t, d), F32),
        compiler_params=_cparams(("parallel",)),
        name="mix_ffn_final" if final else "mix_ffn",
    )(*args)


def _proj_nsa_kernel(x_ref, g_ref, w_ref, cq_ref, sq_ref, ck_ref, sk_ref,
                     q_ref, cmp_ref, ks_ref, vs_ref, kw_ref, vw_ref, gt_ref, kn_ref, *, seq_tiles):
    hd = q_ref.shape[0]
    kvw = NSA_KV_GROUPS * NSA_HEAD_DIM
    xn = _rms(x_ref[...], g_ref[...]).astype(BF16)
    proj = _dot(xn, w_ref[...])
    half = ROT_DIM // 2
    q = proj[:, :hd]
    q_ref[...] = jnp.transpose(_rotate(q, _tile_lanes(cq_ref[...], hd // LANES),
                                       _tile_lanes(sq_ref[...], hd // LANES), half))
    kv = proj[:, hd:hd + 6 * kvw]
    for c in range(cmp_ref.shape[0]):
        cmp_ref[c] = kv[:, c * LANES:(c + 1) * LANES]
    ck = _tile_lanes(ck_ref[...], kvw // LANES)
    sk = _tile_lanes(sk_ref[...], kvw // LANES)
    rows = x_ref.shape[0]
    tpos = (pl.program_id(0) % seq_tiles) * rows + lax.broadcasted_iota(jnp.int32, (rows, LANES), 0)
    lane = lax.broadcasted_iota(jnp.int32, (rows, LANES), 1)
    block_onehot = jnp.where(lane - NSA_HEAD_DIM == tpos // SLC_LEN, 1.0, 0.0)
    ks = _group_slabs(_rotate(kv[:, 2 * kvw:3 * kvw], ck, sk, half), block_onehot).astype(ks_ref.dtype)
    one_lane = jnp.where(lane == NSA_HEAD_DIM, 1.0, 0.0)
    kw = _group_slabs(_rotate(kv[:, 4 * kvw:5 * kvw], ck, sk, half), one_lane).astype(kw_ref.dtype)
    ks_ref[...] = ks
    kw_ref[...] = kw
    kn_ref[0, 0] = _max_key_norm2(ks)
    kn_ref[0, 1] = _max_key_norm2(kw)
    _store_values_t(kv[:, 3 * kvw:4 * kvw], vs_ref)
    _store_values_t(kv[:, 5 * kvw:6 * kvw], vw_ref)
    gt_ref[...] = jnp.transpose(jax.nn.sigmoid(proj[:, hd + 6 * kvw:]))


def _proj_nsa(x2, g, w, seq):
    t, d = x2.shape
    hd = NSA_HEADS * NSA_HEAD_DIM
    kvw = NSA_KV_GROUPS * NSA_HEAD_DIM
    gw = NSA_KV_GROUPS * LANES
    n = w.shape[1]
    ns = seq // TOK_TILE
    pos = np.arange(seq)
    cq, sq = _rot_tables(pos, ROT_DIM, ROPE_THETA, NSA_HEAD_DIM ** -0.5 * LOG2E)
    ck, sk = _rot_tables(pos, ROT_DIM, ROPE_THETA, 1.0)
    row = lambda width: pl.BlockSpec((TOK_TILE, width), lambda i: (i, 0))
    tab = pl.BlockSpec((TOK_TILE, LANES), lambda i: (i % ns, 0))
    chunks = TOK_TILE // LANES
    val_t = (pl.BlockSpec((1, NSA_KV_GROUPS, chunks, V_ROWS, LANES), lambda i: (i // ns, 0, i % ns, 0, 0)),
             jax.ShapeDtypeStruct((t // seq, NSA_KV_GROUPS, seq // LANES, V_ROWS, LANES), BF16))
    tok = lambda wd, dt: (row(wd), jax.ShapeDtypeStruct((t, wd), dt))
    cmp_planes = (pl.BlockSpec((2 * kvw // LANES, TOK_TILE, LANES), lambda i: (0, i, 0)),
                  jax.ShapeDtypeStruct((2 * kvw // LANES, t, LANES), F32))
    tok_t = lambda wd: (pl.BlockSpec((wd, TOK_TILE), lambda i: (0, i)), jax.ShapeDtypeStruct((wd, t), F32))
    key_norms = (pl.BlockSpec((1, 2, SUBLANES, LANES), lambda i: (i, 0, 0, 0)),
                 jax.ShapeDtypeStruct((t // TOK_TILE, 2, SUBLANES, LANES), F32))
    outs = [tok_t(hd), cmp_planes, tok(gw, BF16), val_t, tok(gw, BF16), val_t, tok_t(LANES), key_norms]
    return pl.pallas_call(
        functools.partial(_proj_nsa_kernel, seq_tiles=ns),
        grid=(t // TOK_TILE,),
        in_specs=[row(d), pl.BlockSpec((1, d), lambda i: (0, 0)),
                  pl.BlockSpec((d, n), lambda i: (0, 0), pipeline_mode=pl.Buffered(1)),
                  tab, tab, tab, tab],
        out_specs=[spec for spec, _ in outs],
        out_shape=[shape for _, shape in outs],
        compiler_params=_cparams(("parallel",)),
        name="proj_nsa",
    )(x2, g, w, cq, sq, ck, sk)


def _compress_kernel(x_ref, pe_ref, w1_ref, wc_ref, w2k_ref, w2vt_ref, cos_ref, sgn_ref, kc_ref, vct_ref):
    n = kc_ref.shape[2]
    dh = NSA_HEAD_DIM
    for kv in range(2):
        pe_term = _dot(pe_ref[kv].astype(BF16), w1_ref[kv])[0:1, :]
        pe_term = jnp.concatenate([pe_term, pe_term], axis=1)
        for cp in range(NSA_KV_GROUPS // 2):
            col = kv * (NSA_KV_GROUPS // 2) + cp
            acc = jnp.zeros((n, 2 * LANES), F32)
            for j in range(CMP_STRIDE):
                rows = x_ref[col, pl.ds(j, n, stride=CMP_STRIDE), :]
                acc = acc + _dot(rows.astype(BF16), wc_ref[kv, j])
            hid = jax.nn.gelu(acc[:, :LANES] + pltpu.roll(acc[:, LANES:], n - 1, axis=0) + pe_term).astype(BF16)
            if kv == 0:
                kc = _dot(hid, w2k_ref[...])
                kc = _rotate(kc, _tile_lanes(cos_ref[...], 2), _tile_lanes(sgn_ref[...], 2), ROT_DIM // 2)
                kc_ref[0, 2 * cp] = kc[:, :LANES].astype(kc_ref.dtype)
                kc_ref[0, 2 * cp + 1] = kc[:, LANES:].astype(kc_ref.dtype)
            else:
                vc_t = _dot_nt(w2vt_ref[...], hid)
                vct_ref[0, 2 * cp] = _values_t(vc_t[:dh, :], n)
                vct_ref[0, 2 * cp + 1] = _values_t(vc_t[dh:, :], n)


def _compress(cmp_in, batch, seq, pe, w1, wc, w2k, w2v_t, cos, sgn):
    g = NSA_KV_GROUPS
    n = seq // CMP_STRIDE
    full = lambda a: pl.BlockSpec(a.shape, lambda bi: (0,) * a.ndim)
    return pl.pallas_call(
        _compress_kernel,
        grid=(batch,),
        in_specs=[pl.BlockSpec((cmp_in.shape[0], seq, LANES), lambda bi: (0, bi, 0)),
                  full(pe), full(w1), full(wc), full(w2k), full(w2v_t), full(cos), full(sgn)],
        out_specs=[pl.BlockSpec((1, g, n, LANES), lambda bi: (bi, 0, 0, 0)),
                   pl.BlockSpec((1, g, V_ROWS, n), lambda bi: (bi, 0, 0, 0))],
        out_shape=[jax.ShapeDtypeStruct((batch, g, n, LANES), BF16),
                   jax.ShapeDtypeStruct((batch, g, V_ROWS, n), BF16)],
        compiler_params=_cparams(("parallel",)),
        name="compress",
    )(cmp_in, pe, w1, wc, w2k, w2v_t, cos, sgn)


def _topk_bias_t(imp_t, tpos, last_block, val_ref, cnt_ref):
    nb, tq = imp_t.shape
    j = lax.broadcasted_iota(jnp.int32, (nb, tq), 0)
    cur = tpos // SLC_LEN
    forced = (j == 0) | (j == cur) | (j == cur - 1)
    causal = j <= cur
    val_ref[...] = jnp.where(forced, BIG, jnp.where(causal, imp_t, NEG))
    cnt_ref[...] = jnp.zeros((nb, tq), F32)
    nchunk = nb // SUBLANES
    rows = lax.broadcasted_iota(jnp.int32, (SUBLANES, tq), 0)
    for c in range(nchunk):
        for ic in range(nchunk):
            @pl.when(max(c, ic) * SUBLANES <= last_block)
            def _():
                vc = val_ref[c * SUBLANES:(c + 1) * SUBLANES, :]
                vic = val_ref[ic * SUBLANES:(ic + 1) * SUBLANES, :]
                acc = cnt_ref[c * SUBLANES:(c + 1) * SUBLANES, :]
                for r in range(SUBLANES):
                    if c == ic and r == 0:
                        continue
                    vi = vic if r == 0 else pltpu.roll(vic, SUBLANES - r, axis=0)
                    if c > ic:
                        ahead = jnp.where(vi >= vc, 1.0, 0.0)
                    elif c < ic:
                        ahead = jnp.where(vi > vc, 1.0, 0.0)
                    else:
                        ahead = jnp.where(rows >= SUBLANES - r, jnp.where(vi >= vc, 1.0, 0.0),
                                          jnp.where(vi > vc, 1.0, 0.0))
                    acc = acc + ahead
                cnt_ref[c * SUBLANES:(c + 1) * SUBLANES, :] = acc
    return jnp.where((cnt_ref[...] < float(SLC_TOPK)) & causal, 0.0, NEG)


def _value_chunks_t(vt_ref, g, first, count):
    return jnp.concatenate([vt_ref[0, g, first + i] for i in range(count)], axis=1)


def _max_key_norm2(k_slabs):
    width = k_slabs.shape[1]
    row = lax.broadcasted_iota(jnp.int32, (width, LANES), 0)
    lane = lax.broadcasted_iota(jnp.int32, (width, LANES), 1)
    dims_of_group = jnp.where((row // LANES == lane) & (row % LANES < NSA_HEAD_DIM), 1.0, 0.0).astype(BF16)
    k = k_slabs.astype(F32)
    norm2 = _dot((k * k).astype(BF16), dims_of_group)
    return jnp.broadcast_to(jnp.max(norm2, axis=0, keepdims=True), (SUBLANES, LANES))


def _nsa_attn_kernel(q_ref, gt_ref, kc_ref, vct_ref, ks_ref, vst_ref, kw_ref, vwt_ref, kn_ref, c2s_ref, o_ref,
                     val_ref, cnt_ref, imp_ref, acc_ref, *, n_slc, tk):
    tq = q_ref.shape[1]
    ncmp = kc_ref.shape[2]
    ngrp = NSA_KV_GROUPS
    nrep = NSA_HEADS // ngrp
    dh = NSA_HEAD_DIM
    groups = range(ngrp)
    t0 = pl.program_id(1) * tq
    head = lambda x, r: x[:, r * tq:(r + 1) * tq]
    grp_lanes = lambda g: slice(g * LANES, (g + 1) * LANES)

    kn = jnp.max(kn_ref[...], axis=0)
    kmax = [jnp.sqrt(kn[j, 0:1, :]) * NORM_MARGIN for j in range(2)]

    tpos = t0 + lax.broadcasted_iota(jnp.int32, (1, tq), 1)
    q_t = []
    for g in groups:
        cols = [q_ref[(g * nrep + r) * dh:(g * nrep + r + 1) * dh, :] for r in range(nrep)]
        q_t.append(jnp.concatenate([jnp.concatenate(cols, axis=1), jnp.zeros((LANES - dh, nrep * tq), F32)], axis=0))

    q_norm = [jnp.sqrt(jnp.sum(q_t[g] * q_t[g], axis=0, keepdims=True)) for g in groups]
    shift_s = [q_norm[g] * kmax[0][:, g:g + 1] for g in groups]
    shift_w = [q_norm[g] * kmax[1][:, g:g + 1] for g in groups]
    worst = functools.reduce(jnp.maximum, shift_s + shift_w)
    fixed_shift_ok = jnp.max(worst) < MAX_FIXED_SHIFT

    last = (t0 + tq - 1) // tk
    k_last = last * tk + lax.broadcasted_iota(jnp.int32, (tk, tq), 0)
    causal_bias = jnp.where(k_last <= tpos, 0.0, NEG)
    span = WINDOW + tq
    start = pl.multiple_of(jnp.maximum(t0 - WINDOW, 0), tq)
    dist = tpos - (start + lax.broadcasted_iota(jnp.int32, (span, tq), 0))
    band_bias = jnp.where((dist >= 0) & (dist < WINDOW), 0.0, NEG)
    key_tile = lambda c, g: ks_ref[pl.ds(pl.multiple_of(c * tk, tk), tk), grp_lanes(g)]
    val_tile = lambda c, g: _value_chunks_t(vst_ref, g, c * (tk // LANES), tk // LANES)
    win_keys = lambda g: kw_ref[pl.ds(start, span), grp_lanes(g)]
    win_vals = lambda g: _value_chunks_t(vwt_ref, g, start // LANES, span // LANES)
    all_heads = lambda x: jnp.concatenate([x] * nrep, axis=1)

    cmp_end = lax.broadcasted_iota(jnp.int32, (ncmp, tq), 0) * CMP_STRIDE + (CMP_LEN - 1)
    vis_bias = jnp.where(cmp_end <= tpos, 0.0, NEG)
    any_vis = jnp.where(tpos >= CMP_LEN - 1, 1.0, 0.0)

    def compressed_branch(g):
        c2s = c2s_ref[...]
        s = _dot(kc_ref[0, g], q_t[g].astype(BF16))
        p_sum = jnp.zeros((ncmp, tq), F32)
        p_cmp = []
        for r in range(nrep):
            sr = head(s, r) + vis_bias
            e = jnp.exp2(sr - jnp.max(sr, axis=0, keepdims=True))
            pr = e * (any_vis / jnp.sum(e, axis=0, keepdims=True))
            p_sum = p_sum + pr
            p_cmp.append(pr.astype(BF16))
        acc_ref[2, g] = _dot(vct_ref[0, g], jnp.concatenate(p_cmp, axis=1))
        p_hi, p_lo = _split_bf16(p_sum)
        imp_ref[:, g * tq:(g + 1) * tq] = (_dot(c2s, p_hi) + _dot(c2s, p_lo))[:n_slc, :]

    @pl.when(fixed_shift_ok)
    def _():
        zeros_hi = jnp.zeros((dh - 1, nrep * tq), F32)
        for g in groups:
            qt_win = jnp.concatenate([q_t[g][:dh, :], -shift_w[g], zeros_hi], axis=0).astype(BF16)
            p = jnp.exp2(_dot(win_keys(g), qt_win) + all_heads(band_bias)).astype(BF16)
            acc_ref[1, g] = _dot(win_vals(g), p)
            compressed_branch(g)

    @pl.when(jnp.logical_not(fixed_shift_ok))
    def _():
        for g in groups:
            compressed_branch(g)
            sw = _dot(win_keys(g), q_t[g].astype(BF16))
            p_win = []
            for r in range(nrep):
                sr = head(sw, r) + band_bias
                p_win.append(jnp.exp2(sr - jnp.max(sr, axis=0, keepdims=True)).astype(BF16))
            acc_ref[1, g] = _dot(win_vals(g), jnp.concatenate(p_win, axis=1))

    bias_t = _topk_bias_t(imp_ref[...], jnp.concatenate([tpos] * ngrp, axis=1),
                          (t0 + tq - 1) // SLC_LEN, val_ref, cnt_ref)
    if n_slc < dh:
        bias_t = jnp.concatenate([bias_t, jnp.full((dh - n_slc, ngrp * tq), NEG, F32)], axis=0)
    bias_rows = [jnp.concatenate([head(bias_t, g)] * nrep, axis=1) for g in groups]

    @pl.when(fixed_shift_ok)
    def _():
        qt_sel = [jnp.concatenate([q_t[g][:dh, :], bias_rows[g] - shift_s[g]], axis=0).astype(BF16) for g in groups]

        def partial_pv(c, g, token_bias=None, nkeys=tk):
            k0 = pl.multiple_of(c * tk, tk)
            sc = _dot(ks_ref[pl.ds(k0, nkeys), grp_lanes(g)], qt_sel[g])
            if token_bias is not None:
                sc = sc + all_heads(token_bias[:nkeys, :])
            vals = _value_chunks_t(vst_ref, g, c * (tk // LANES), nkeys // LANES)
            return _dot(vals, jnp.exp2(sc).astype(BF16))

        def body(i, accs):
            return tuple(accs[g] + partial_pv(2 * i, g) + partial_pv(2 * i + 1, g) for g in groups)

        accs = lax.fori_loop(0, last // 2, body, tuple(jnp.zeros((V_ROWS, nrep * tq), F32) for _ in groups))
        for g in groups:
            acc_ref[0, g] = accs[g]

        @pl.when(last % 2 == 1)
        def _():
            for g in groups:
                acc_ref[0, g] += partial_pv(last - 1, g)

        needs_half = (t0 % tk) + tq <= tk // 2

        @pl.when(needs_half)
        def _():
            for g in groups:
                acc_ref[0, g] += partial_pv(last, g, causal_bias, tk // 2)

        @pl.when(jnp.logical_not(needs_half))
        def _():
            for g in groups:
                acc_ref[0, g] += partial_pv(last, g, causal_bias)

    @pl.when(jnp.logical_not(fixed_shift_ok))
    def _():
        qt_sel = [jnp.concatenate([q_t[g][:dh, :], bias_rows[g]], axis=0).astype(BF16) for g in groups]

        def tile(c, carry, token_bias):
            out = []
            for g in groups:
                m, acc = carry[g]
                sc = _dot(key_tile(c, g), qt_sel[g])
                ps, ms, alphas = [], [], []
                for r in range(nrep):
                    sr = head(sc, r)
                    if token_bias is not None:
                        sr = sr + token_bias
                    m_new = jnp.maximum(head(m, r), jnp.max(sr, axis=0, keepdims=True))
                    alphas.append(jnp.exp2(head(m, r) - m_new))
                    ms.append(m_new)
                    ps.append(jnp.exp2(sr - m_new).astype(BF16))
                pv = _dot(val_tile(c, g), jnp.concatenate(ps, axis=1))
                out.append((jnp.concatenate(ms, axis=1), jnp.concatenate(alphas, axis=1) * acc + pv))
            return tuple(out)

        init = tuple((jnp.full((1, nrep * tq), NEG, F32), jnp.zeros((V_ROWS, nrep * tq), F32)) for _ in groups)
        carry = lax.fori_loop(0, last, lambda c, cr: tile(c, cr, None), init)
        carry = tile(last, carry, causal_bias)
        for g in groups:
            acc_ref[0, g] = carry[g][1]

    gt_t = gt_ref[...]
    heads = []
    for g in groups:
        acc_s = acc_ref[0, g]
        acc_w = acc_ref[1, g]
        o_cmp_g = acc_ref[2, g]
        o_slc = acc_s[:dh, :] * (1.0 / acc_s[dh:dh + 1, :])
        o_win = acc_w[:dh, :] * (1.0 / acc_w[dh:dh + 1, :])
        for r in range(nrep):
            row = 3 * (g * nrep + r)
            heads.append(gt_t[row:row + 1, :] * head(o_cmp_g, r)[:dh, :]
                         + gt_t[row + 1:row + 2, :] * head(o_slc, r) + gt_t[row + 2:row + 3, :] * head(o_win, r))
    o_ref[...] = jnp.transpose(jnp.concatenate(heads, axis=0)).astype(o_ref.dtype)


def _nsa_attention(q, gt, kc, vc_t, ks, vs_t, kw, vw_t, kn, batch, seq):
    g = NSA_KV_GROUPS
    tiles_per_row = kn.shape[0] // batch
    nq = seq // ATT_TQ
    qw = (NSA_HEADS // g) * NSA_HEAD_DIM
    n_slc = seq // SLC_LEN
    n_cmp = (seq - CMP_LEN) // CMP_STRIDE + 1
    n_cmp_pad = kc.shape[2]
    c_start = np.arange(n_cmp)[:, None] * CMP_STRIDE
    s_start = np.arange(n_slc)[None, :] * SLC_LEN
    overlap = np.clip(np.minimum(c_start + CMP_LEN, s_start + SLC_LEN)
                      - np.maximum(c_start, s_start), 0, None) / CMP_LEN
    assert n_slc <= NSA_HEAD_DIM and n_slc % SUBLANES == 0 and seq % ATT_TK == 0
    c2s_t = np.zeros((NSA_HEAD_DIM, n_cmp_pad))
    c2s_t[:n_slc, :n_cmp] = overlap.T
    hd = NSA_HEADS * NSA_HEAD_DIM
    tok_spec = lambda width: pl.BlockSpec((ATT_TQ, width), lambda b, qi: (b * nq + qi, 0))
    key_spec = pl.BlockSpec((seq, g * LANES), lambda b, qi: (b, 0))
    val_spec = pl.BlockSpec((1, g, seq // LANES, V_ROWS, LANES), lambda b, qi: (b, 0, 0, 0, 0))
    return pl.pallas_call(
        functools.partial(_nsa_attn_kernel, n_slc=n_slc, tk=ATT_TK),
        grid=(batch, nq),
        in_specs=[pl.BlockSpec((hd, ATT_TQ), lambda b, qi: (0, b * nq + qi)),
                  pl.BlockSpec((LANES, ATT_TQ), lambda b, qi: (0, b * nq + qi)),
                  pl.BlockSpec((1, g, n_cmp_pad, LANES), lambda b, qi: (b, 0, 0, 0)),
                  pl.BlockSpec((1, g, V_ROWS, n_cmp_pad), lambda b, qi: (b, 0, 0, 0)),
                  key_spec, val_spec, key_spec, val_spec,
                  pl.BlockSpec((tiles_per_row,) + kn.shape[1:], lambda b, qi: (b, 0, 0, 0)),
                  pl.BlockSpec(c2s_t.shape, lambda b, qi: (0, 0))],
        out_specs=tok_spec(hd),
        out_shape=jax.ShapeDtypeStruct((batch * seq, hd), BF16),
        scratch_shapes=[pltpu.VMEM((n_slc, g * ATT_TQ), F32)] * 3
        + [pltpu.VMEM((3, g, V_ROWS, (NSA_HEADS // g) * ATT_TQ), F32)],
        compiler_params=_cparams(("parallel", "arbitrary")),
        name="nsa_attention",
    )(q, gt, kc, vc_t, ks, vs_t, kw, vw_t, kn, jnp.asarray(c2s_t, BF16))


def _block_diag(w):
    nb, bi, bo = w.shape
    eye = jnp.eye(nb, dtype=w.dtype)
    return (eye[:, None, :, None] * w[:, :, None, :]).reshape(nb * bi, nb * bo)


def _nsa_weight(w_in):
    return jnp.pad(w_in, ((0, 0), (0, LANES - 3 * NSA_HEADS))).astype(BF16)


def _ab_mixer(h, g_attn, w_in, conv_w, conv_b, ga_w, ga_b, gx_w, gx_b, lam, batch, seq):
    proj = _proj_ab(h, g_attn, w_in.astype(BF16))
    gate_w = jnp.concatenate([_block_diag(ga_w), _block_diag(gx_w)], axis=1).astype(BF16)
    gate_b = jnp.concatenate([ga_b, gx_b])[None, :]
    lru_out = _lru(proj, batch, seq, conv_w, conv_b[None, :], gate_w, gate_b, lam[None, :])
    return [lru_out, _retention(proj, batch, seq)]


def _nsa_mixer(h, g_attn, w_in, pe_k, k_w1, k_w2, pe_v, v_w1, v_w2, batch, seq):
    q, cmp_in, ks, vs_t, kw, vw_t, gt, kn = _proj_nsa(h, g_attn, _nsa_weight(w_in), seq)
    dh = NSA_HEAD_DIM
    n_blk = seq // CMP_STRIDE
    pe = jnp.broadcast_to(jnp.stack([pe_k, pe_v]).reshape(2, 1, CMP_LEN * dh), (2, SUBLANES, CMP_LEN * dh))
    w1 = jnp.stack([k_w1, v_w1]).astype(BF16)
    w1r = w1.reshape(2, CMP_LEN, dh, dh)
    first, second, zero = w1r[:, :CMP_STRIDE], w1r[:, CMP_STRIDE:], jnp.zeros_like(w1r[:, :CMP_STRIDE])
    wc = jnp.concatenate([jnp.concatenate([first, zero, second, zero], axis=-1),
                          jnp.concatenate([zero, first, zero, second], axis=-1)], axis=-2)
    w2k = k_w2.astype(BF16)
    z = jnp.zeros_like(w2k)
    w2k = jnp.concatenate([jnp.concatenate([w2k, z, z, z], axis=1), jnp.concatenate([z, z, w2k, z], axis=1)], axis=0)
    w2v_t = v_w2.T.astype(BF16)
    zt = jnp.zeros_like(w2v_t)
    w2v_t = jnp.concatenate([jnp.concatenate([w2v_t, zt], axis=1), jnp.concatenate([zt, w2v_t], axis=1)], axis=0)
    cos, sgn = _rot_tables(np.arange(n_blk) * CMP_STRIDE + CMP_LEN - 1, ROT_DIM, ROPE_THETA, 1.0)
    kc, vc_t = _compress(cmp_in, batch, seq, pe, w1, wc, w2k, w2v_t, cos, sgn)
    return [_nsa_attention(q, gt, kc, vc_t, ks, vs_t, kw, vw_t, kn, batch, seq)]


def kernel(x, attn_norm, ab_w_in, conv_w, conv_b, gate_a_w, gate_a_b, gate_x_w, gate_x_b, lru_lambda,
           ab_w_out, nsa_w_in, cmp_pe_k, cmp_k_w1, cmp_k_w2, cmp_pe_v, cmp_v_w1, cmp_v_w2, nsa_w_out,
           ffn_norm, ffn_w1, ffn_w3, ffn_w2, final_norm):
    batch, seq, d = x.shape
    depth = attn_norm.shape[0]
    h = x.reshape(batch * seq, d)
    for layer in range(depth):
        i = layer // 2
        g_attn = attn_norm[layer][None, :]
        if layer % 2 == 0:
            mixes = _ab_mixer(h, g_attn, ab_w_in[i], conv_w[i], conv_b[i], gate_a_w[i], gate_a_b[i],
                              gate_x_w[i], gate_x_b[i], lru_lambda[i], batch, seq)
            w_out = ab_w_out[i]
        else:
            mixes = _nsa_mixer(h, g_attn, nsa_w_in[i], cmp_pe_k[i], cmp_k_w1[i], cmp_k_w2[i],
                               cmp_pe_v[i], cmp_v_w1[i], cmp_v_w2[i], batch, seq)
            w_out = nsa_w_out[i]
        last = layer == depth - 1
        h = _mix_ffn(h, mixes, w_out.astype(BF16), ffn_norm[layer][None, :],
                     ffn_w1[layer].astype(BF16), ffn_w3[layer].astype(BF16), ffn_w2[layer].astype(BF16),
                     final_g=final_norm[None, :] if last else None)
    return h.reshape(batch, seq, d)
```

```python
import functools

import numpy as np
import jax
import jax.numpy as jnp
from jax import lax
from jax.experimental import pallas as pl
from jax.experimental.pallas import tpu as pltpu

F32 = jnp.float32
BF16 = jnp.bfloat16

EPS = 1e-6
NEG = -1e30
BIG = 1e30
LOG2E = float(np.log2(np.e))

LRU_BLOCKS = 8
CONV_WIDTH = 4
LRU_C = 8.0
RET_HEADS = 8
RET_DIM = 64
RET_THETA = 10000.0
NSA_HEADS = 16
NSA_KV_GROUPS = 4
NSA_HEAD_DIM = 64
CMP_LEN = 32
CMP_STRIDE = 16
SLC_LEN = 64
SLC_TOPK = 16
WINDOW = 512
ROPE_THETA = 500000.0
ROT_DIM = NSA_HEAD_DIM // 4

LANES = 128
SUBLANES = 8
VMEM_LIMIT = 56 * 1024 * 1024

RET_CHUNK = 256
TOK_TILE = 512
PROJ_AB_TILE = 1024
LRU_TILE = 1024
ATT_TQ = 256
ATT_TK = 512
V_ROWS = NSA_HEAD_DIM + 16
MAX_FIXED_SHIFT = 32.0
NORM_MARGIN = 1.01


def _cparams(sem):
    return pltpu.CompilerParams(dimension_semantics=sem, vmem_limit_bytes=VMEM_LIMIT)


def _dot(a, b):
    return jnp.dot(a, b, preferred_element_type=F32)


def _dot_nt(a, b):
    return lax.dot_general(a, b, (((1,), (1,)), ((), ())), preferred_element_type=F32)


def _rms(x, g):
    return x * lax.rsqrt(jnp.mean(x * x, axis=-1, keepdims=True) + EPS) * g


def _tile_lanes(t, reps):
    return t if reps == 1 else jnp.concatenate([t] * reps, axis=1)


def _rotate(x, cos, sgn, half):
    n = x.shape[1]
    lane = lax.broadcasted_iota(jnp.int32, x.shape, 1)
    first = (lane % NSA_HEAD_DIM) < half
    partner = jnp.where(first, pltpu.roll(x, n - half, axis=1), pltpu.roll(x, half, axis=1))
    return x * cos + partner * sgn


def _group_slabs(x, fill):
    lane = lax.broadcasted_iota(jnp.int32, (x.shape[0], LANES), 1)
    lo = lane < NSA_HEAD_DIM
    out = []
    for c in range(x.shape[1] // LANES):
        xc = x[:, c * LANES:(c + 1) * LANES]
        out.append(jnp.where(lo, xc, fill))
        out.append(jnp.where(lo, pltpu.roll(xc, NSA_HEAD_DIM, axis=1), fill))
    return jnp.concatenate(out, axis=1)


def _values_t(v_t, ncols):
    return jnp.concatenate([v_t, jnp.ones((V_ROWS - NSA_HEAD_DIM, ncols), F32)], axis=0).astype(BF16)


def _store_values_t(v, o_ref):
    rows = v.shape[0]
    v_t = jnp.transpose(v)
    for g in range(v.shape[1] // NSA_HEAD_DIM):
        slab = _values_t(v_t[g * NSA_HEAD_DIM:(g + 1) * NSA_HEAD_DIM, :], rows)
        for c in range(rows // LANES):
            o_ref[0, g, c] = slab[:, c * LANES:(c + 1) * LANES]


def _rot_tables(pos, rot_dim, theta, scale):
    half = rot_dim // 2
    inv = theta ** (-np.arange(0, rot_dim, 2, dtype=np.float64) / rot_dim)
    ang = np.asarray(pos, np.float64)[:, None] * inv
    cos = np.ones((len(pos), NSA_HEAD_DIM))
    sgn = np.zeros((len(pos), NSA_HEAD_DIM))
    cos[:, :half] = np.cos(ang)
    cos[:, half:rot_dim] = np.cos(ang)
    sgn[:, :half] = -np.sin(ang)
    sgn[:, half:rot_dim] = np.sin(ang)
    cos = np.tile(cos * scale, (1, 2))
    sgn = np.tile(sgn * scale, (1, 2))
    return jnp.asarray(cos, F32), jnp.asarray(sgn, F32)


def _ret_decay_tables(chunk):
    log_g = np.log1p(-(2.0 ** (-5.0 - np.arange(RET_HEADS, dtype=np.float64))))
    ci = np.arange(chunk, dtype=np.float64)
    diff = ci[:, None] - ci[None, :]
    inner = np.where(diff >= 0, np.exp(np.maximum(diff, 0.0) * log_g[:, None, None]), 0.0)
    lane_head = np.repeat(np.arange(RET_HEADS), RET_DIM)
    qdec = np.exp((ci[:, None] + 1.0) * log_g[lane_head][None, :])
    kdec = np.exp((chunk - 1.0 - ci[:, None]) * log_g[lane_head][None, :])
    cdec = np.exp(chunk * log_g[lane_head])[None, :]
    return (jnp.asarray(inner, F32), jnp.asarray(qdec, F32), jnp.asarray(kdec, F32),
            jnp.asarray(cdec, F32))


def _proj_ab_kernel(x_ref, g_ref, w_ref, o_ref):
    xn = _rms(x_ref[...], g_ref[...]).astype(BF16)
    o_ref[...] = _dot(xn, w_ref[...])


def _proj_ab(x2, g, w):
    t, d = x2.shape
    n = w.shape[1]
    return pl.pallas_call(
        _proj_ab_kernel,
        grid=(t // PROJ_AB_TILE,),
        in_specs=[pl.BlockSpec((PROJ_AB_TILE, d), lambda i: (i, 0)),
                  pl.BlockSpec((1, d), lambda i: (0, 0)),
                  pl.BlockSpec((d, n), lambda i: (0, 0), pipeline_mode=pl.Buffered(1))],
        out_specs=pl.BlockSpec((PROJ_AB_TILE, n), lambda i: (i, 0)),
        out_shape=jax.ShapeDtypeStruct((t, n), F32),
        compiler_params=_cparams(("parallel",)),
        name="proj_ab",
    )(x2, g, w)


def _lru_kernel(y_ref, x_ref, cw_ref, cb_ref, gw_ref, gb_ref, lam_ref, o_ref,
                xp_ref, a_ref, u_ref, h_ref, carry_ref):
    ts, w = x_ref.shape
    row = lax.broadcasted_iota(jnp.int32, (SUBLANES, w), 0)

    @pl.when(pl.program_id(1) == 0)
    def _():
        xp_ref[...] = jnp.zeros_like(xp_ref)
        carry_ref[...] = jnp.zeros_like(carry_ref)

    x = x_ref[...]
    prev = xp_ref[...]
    xc = cb_ref[...] + x * cw_ref[CONV_WIDTH - 1:CONV_WIDTH, :]
    for s in range(1, CONV_WIDTH):
        rolled = pltpu.roll(x, s, axis=0)
        head_rows = jnp.where(row < s, pltpu.roll(prev, s, axis=0), rolled[:SUBLANES, :])
        shifted = jnp.concatenate([head_rows, rolled[SUBLANES:, :]], axis=0)
        xc = xc + shifted * cw_ref[CONV_WIDTH - 1 - s:CONV_WIDTH - s, :]
    xp_ref[...] = x[ts - SUBLANES:, :]

    gates = _dot(xc.astype(BF16), gw_ref[...]) + gb_ref[...]
    r = jax.nn.sigmoid(gates[:, :w])
    i = jax.nn.sigmoid(gates[:, w:])
    log_a = -LRU_C * r * jax.nn.softplus(-lam_ref[...])
    a = jnp.exp(log_a)
    z = jnp.tanh(-log_a) * (a * a + 1.0)
    a_ref[...] = a
    u_ref[...] = jnp.where(z > 0.0, z * lax.rsqrt(z), 0.0) * (i * xc)

    def body(gi, carry):
        r0 = pl.multiple_of(gi * SUBLANES, SUBLANES)
        a = a_ref[pl.ds(r0, SUBLANES), :]
        u = u_ref[pl.ds(r0, SUBLANES), :]
        for d in (1, 2, 4):
            a_s = pltpu.roll(a, d, axis=0)
            u_s = pltpu.roll(u, d, axis=0)
            m = row >= d
            u = jnp.where(m, a * u_s + u, u)
            a = jnp.where(m, a * a_s, a)
        h = a * carry + u
        h_ref[pl.ds(r0, SUBLANES), :] = h
        return h[SUBLANES - 1:SUBLANES, :]

    carry_ref[...] = lax.fori_loop(0, ts // SUBLANES, body, carry_ref[...])
    o_ref[...] = (h_ref[...] * jax.nn.gelu(y_ref[...])).astype(o_ref.dtype)


def _lru(proj, batch, seq, conv_w, conv_b, gate_w, gate_b, lam):
    w = conv_w.shape[1]
    ns = seq // LRU_TILE
    return pl.pallas_call(
        _lru_kernel,
        grid=(batch, ns),
        in_specs=[pl.BlockSpec((LRU_TILE, w), lambda b, s: (b * ns + s, 0)),
                  pl.BlockSpec((LRU_TILE, w), lambda b, s: (b * ns + s, 1)),
                  pl.BlockSpec((CONV_WIDTH, w), lambda b, s: (0, 0)),
                  pl.BlockSpec((1, w), lambda b, s: (0, 0)),
                  pl.BlockSpec((w, 2 * w), lambda b, s: (0, 0)),
                  pl.BlockSpec((1, 2 * w), lambda b, s: (0, 0)),
                  pl.BlockSpec((1, w), lambda b, s: (0, 0))],
        out_specs=pl.BlockSpec((LRU_TILE, w), lambda b, s: (b * ns + s, 0)),
        out_shape=jax.ShapeDtypeStruct((batch * seq, w), BF16),
        scratch_shapes=[pltpu.VMEM((SUBLANES, w), F32),
                        pltpu.VMEM((LRU_TILE, w), F32),
                        pltpu.VMEM((LRU_TILE, w), F32),
                        pltpu.VMEM((LRU_TILE, w), F32),
                        pltpu.VMEM((1, w), F32)],
        compiler_params=_cparams(("parallel", "arbitrary")),
        name="rg_lru",
    )(proj, proj, conv_w, conv_b, gate_w, gate_b, lam)


def _split_bf16(x):
    hi = x.astype(BF16)
    lo = (x - hi.astype(F32)).astype(BF16)
    return hi, lo


def _ret_kernel(q_ref, k_ref, v_ref, g_ref, cq_ref, sq_ref, ck_ref, sk_ref,
                inner_ref, qdec_ref, kdec_ref, cdec_ref, avg_ref, o_ref, st_ref):
    c, w = q_ref.shape
    npair = w // LANES
    b = pl.program_id(1)

    @pl.when(pl.program_id(0) == 0)
    def _():
        st_ref[b] = jnp.zeros(st_ref.shape[1:], F32)

    half = RET_DIM // 2
    q = _rotate(q_ref[...], _tile_lanes(cq_ref[...], npair), _tile_lanes(sq_ref[...], npair), half)
    k = _rotate(k_ref[...], _tile_lanes(ck_ref[...], npair), _tile_lanes(sk_ref[...], npair), half)
    v = v_ref[...]
    qd = (q * qdec_ref[...]).astype(BF16)
    kd = k * kdec_ref[...]
    q = q.astype(BF16)
    kb = k.astype(BF16)
    vb = v.astype(BF16)

    lane = lax.broadcasted_iota(jnp.int32, (c, LANES), 1)
    lo = lane < RET_DIM
    rr = lax.broadcasted_iota(jnp.int32, (LANES, LANES), 0) < RET_DIM
    cc = lax.broadcasted_iota(jnp.int32, (LANES, LANES), 1) < RET_DIM
    same_head = rr == cc
    zero = jnp.zeros((c, LANES), BF16)

    outs = []
    for p in range(npair):
        sl = slice(p * LANES, (p + 1) * LANES)
        qp, kp, vp = q[:, sl], kb[:, sl], vb[:, sl]
        s_lo = _dot_nt(jnp.where(lo, qp, zero), kp) * inner_ref[2 * p]
        s_hi = _dot_nt(jnp.where(lo, zero, qp), kp) * inner_ref[2 * p + 1]
        o = jnp.where(lo, _dot(s_lo.astype(BF16), vp), _dot(s_hi.astype(BF16), vp))
        st = st_ref[b, p]
        o = o + _dot(qd[:, sl], st.astype(BF16))
        kv = _dot(jnp.transpose(kd[:, sl]).astype(BF16), vp)
        st_ref[b, p] = st * cdec_ref[:, sl] + jnp.where(same_head, kv, 0.0)
        outs.append(o)

    o_all = jnp.concatenate(outs, axis=0)
    avg = avg_ref[...]
    mu = _dot(o_all.astype(BF16), avg)
    dlt = o_all - mu
    var = _dot((dlt * dlt).astype(BF16), avg)
    y = dlt * lax.rsqrt(var + EPS)
    y = jnp.concatenate([y[p * c:(p + 1) * c, :] for p in range(npair)], axis=1)
    o_ref[...] = (y * jax.nn.silu(g_ref[...])).astype(o_ref.dtype)


def _retention(proj, batch, seq):
    w = RET_HEADS * RET_DIM
    c = RET_CHUNK
    nc = seq // c
    npair = w // LANES
    pos = np.arange(seq)
    cq, sq = _rot_tables(pos, RET_DIM, RET_THETA, 1.0)
    ck, sk = _rot_tables(pos, RET_DIM, RET_THETA, RET_DIM ** -0.5)
    inner, qdec, kdec, cdec = _ret_decay_tables(c)
    blk = np.arange(LANES) // RET_DIM
    avg = jnp.asarray((blk[:, None] == blk[None, :]) / RET_DIM, BF16)
    col0 = 2
    tok = lambda j: pl.BlockSpec((c, w), lambda ci, b, j=j: (b * nc + ci, j))
    tab = pl.BlockSpec((c, LANES), lambda ci, b: (ci, 0))
    const = lambda shape: pl.BlockSpec(shape, lambda ci, b: (0,) * len(shape))
    return pl.pallas_call(
        _ret_kernel,
        grid=(nc, batch),
        in_specs=[tok(col0), tok(col0 + 1), tok(col0 + 2), tok(col0 + 3),
                  tab, tab, tab, tab,
                  const((RET_HEADS, c, c)), const((c, w)), const((c, w)), const((1, w)),
                  const((LANES, LANES))],
        out_specs=pl.BlockSpec((c, w), lambda ci, b: (b * nc + ci, 0)),
        out_shape=jax.ShapeDtypeStruct((batch * seq, w), BF16),
        scratch_shapes=[pltpu.VMEM((batch, npair, LANES, LANES), F32)],
        compiler_params=_cparams(("arbitrary", "arbitrary")),
        name="retention",
    )(proj, proj, proj, proj, cq, sq, ck, sk, inner, qdec, kdec, cdec, avg)


def _mix_ffn_kernel(*refs, n_mix, final):
    h_ref = refs[0]
    mix_refs = refs[1:1 + n_mix]
    wo_ref, g_ref, w1_ref, w3_ref, w2_ref = refs[1 + n_mix:6 + n_mix]
    rest = refs[6 + n_mix:]
    fg_ref = rest[0] if final else None
    o_ref = rest[-1]

    mixed = mix_refs[0][...] if n_mix == 1 else jnp.concatenate([r[...] for r in mix_refs], axis=1)
    h1 = h_ref[...] + _dot(mixed, wo_ref[...])
    xn = _rms(h1, g_ref[...]).astype(BF16)
    act = (jax.nn.silu(_dot(xn, w1_ref[...])) * _dot(xn, w3_ref[...])).astype(BF16)
    h2 = h1 + _dot(act, w2_ref[...])
    if final:
        h2 = _rms(h2, fg_ref[...])
    o_ref[...] = h2


def _mix_ffn(h2d, mixes, wo, g, w1, w3, w2, final_g=None):
    t, d = h2d.shape
    hid = w1.shape[1]
    final = final_g is not None
    row = lambda width: pl.BlockSpec((TOK_TILE, width), lambda i: (i, 0))
    const = lambda shape: pl.BlockSpec(shape, lambda i: (0, 0), pipeline_mode=pl.Buffered(1))
    in_specs = [row(d)] + [row(m.shape[1]) for m in mixes] + [
        const(wo.shape), const((1, d)), const((d, hid)), const((d, hid)), const((hid, d))]
    args = [h2d, *mixes, wo, g, w1, w3, w2]
    if final:
        in_specs.append(const((1, d)))
        args.append(final_g)
    return pl.pallas_call(
        functools.partial(_mix_ffn_kernel, n_mix=len(mixes), final=final),
        grid=(t // TOK_TILE,),
        in_specs=in_specs,
        out_specs=row(d),
        out_shape=jax.ShapeDtypeStruct((t, d), F32),
        compiler_params=_cparams(("parallel",)),
        name="mix_ffn_final" if final else "mix_ffn",
    )(*args)


def _proj_nsa_kernel(x_ref, g_ref, w_ref, cq_ref, sq_ref, ck_ref, sk_ref,
                     q_ref, cmp_ref, ks_ref, vs_ref, kw_ref, vw_ref, gt_ref, kn_ref, *, seq_tiles):
    hd = q_ref.shape[0]
    kvw = NSA_KV_GROUPS * NSA_HEAD_DIM
    xn = _rms(x_ref[...], g_ref[...]).astype(BF16)
    half = ROT_DIM // 2
    proj = _dot(xn, w_ref[...])
    q = proj[:, :hd]
    kv_piece = lambda j: proj[:, hd + j * kvw:hd + (j + 1) * kvw]
    q_ref[...] = jnp.transpose(_rotate(q, _tile_lanes(cq_ref[...], hd // LANES),
                                       _tile_lanes(sq_ref[...], hd // LANES), half))
    for j in range(2):
        piece = kv_piece(j)
        for c in range(kvw // LANES):
            cmp_ref[j * (kvw // LANES) + c] = piece[:, c * LANES:(c + 1) * LANES]
    ck = _tile_lanes(ck_ref[...], kvw // LANES)
    sk = _tile_lanes(sk_ref[...], kvw // LANES)
    rows = x_ref.shape[0]
    tpos = (pl.program_id(0) % seq_tiles) * rows + lax.broadcasted_iota(jnp.int32, (rows, LANES), 0)
    lane = lax.broadcasted_iota(jnp.int32, (rows, LANES), 1)
    block_onehot = jnp.where(lane - NSA_HEAD_DIM == tpos // SLC_LEN, 1.0, 0.0)
    ks = _group_slabs(_rotate(kv_piece(2), ck, sk, half), block_onehot).astype(ks_ref.dtype)
    one_lane = jnp.where(lane == NSA_HEAD_DIM, 1.0, 0.0)
    kw = _group_slabs(_rotate(kv_piece(4), ck, sk, half), one_lane).astype(kw_ref.dtype)
    ks_ref[...] = ks
    kw_ref[...] = kw
    kn_ref[0, 0] = _max_key_norm2(ks)
    kn_ref[0, 1] = _max_key_norm2(kw)
    _store_values_t(kv_piece(3), vs_ref)
    _store_values_t(kv_piece(5), vw_ref)
    gt_ref[...] = jnp.transpose(jax.nn.sigmoid(proj[:, hd + 6 * kvw:]))


def _proj_nsa(x2, g, w, seq):
    t, d = x2.shape
    hd = NSA_HEADS * NSA_HEAD_DIM
    kvw = NSA_KV_GROUPS * NSA_HEAD_DIM
    gw = NSA_KV_GROUPS * LANES
    n = w.shape[1]
    ns = seq // TOK_TILE
    pos = np.arange(seq)
    cq, sq = _rot_tables(pos, ROT_DIM, ROPE_THETA, NSA_HEAD_DIM ** -0.5 * LOG2E)
    ck, sk = _rot_tables(pos, ROT_DIM, ROPE_THETA, 1.0)
    row = lambda width: pl.BlockSpec((TOK_TILE, width), lambda i: (i, 0))
    tab = pl.BlockSpec((TOK_TILE, LANES), lambda i: (i % ns, 0))
    chunks = TOK_TILE // LANES
    val_t = (pl.BlockSpec((1, NSA_KV_GROUPS, chunks, V_ROWS, LANES), lambda i: (i // ns, 0, i % ns, 0, 0)),
             jax.ShapeDtypeStruct((t // seq, NSA_KV_GROUPS, seq // LANES, V_ROWS, LANES), BF16))
    tok = lambda wd, dt: (row(wd), jax.ShapeDtypeStruct((t, wd), dt))
    cmp_planes = (pl.BlockSpec((2 * kvw // LANES, TOK_TILE, LANES), lambda i: (0, i, 0)),
                  jax.ShapeDtypeStruct((2 * kvw // LANES, t, LANES), F32))
    tok_t = lambda wd: (pl.BlockSpec((wd, TOK_TILE), lambda i: (0, i)), jax.ShapeDtypeStruct((wd, t), F32))
    key_norms = (pl.BlockSpec((1, 2, SUBLANES, LANES), lambda i: (i, 0, 0, 0)),
                 jax.ShapeDtypeStruct((t // TOK_TILE, 2, SUBLANES, LANES), F32))
    outs = [tok_t(hd), cmp_planes, tok(gw, BF16), val_t, tok(gw, BF16), val_t, tok_t(LANES), key_norms]
    return pl.pallas_call(
        functools.partial(_proj_nsa_kernel, seq_tiles=ns),
        grid=(t // TOK_TILE,),
        in_specs=[row(d), pl.BlockSpec((1, d), lambda i: (0, 0)),
                  pl.BlockSpec((d, n), lambda i: (0, 0), pipeline_mode=pl.Buffered(1)),
                  tab, tab, tab, tab],
        out_specs=[spec for spec, _ in outs],
        out_shape=[shape for _, shape in outs],
        compiler_params=_cparams(("parallel",)),
        name="proj_nsa",
    )(x2, g, w, cq, sq, ck, sk)


def _compress_kernel(x_ref, pe_ref, w1_ref, wc_ref, w2k_ref, w2vt_ref, cos_ref, sgn_ref, kc_ref, vct_ref):
    n = kc_ref.shape[2]
    dh = NSA_HEAD_DIM
    for kv in range(2):
        pe_term = _dot(pe_ref[kv].astype(BF16), w1_ref[kv])[0:1, :]
        pe_term = jnp.concatenate([pe_term, pe_term], axis=1)
        for cp in range(NSA_KV_GROUPS // 2):
            col = kv * (NSA_KV_GROUPS // 2) + cp
            acc = jnp.zeros((n, 2 * LANES), F32)
            for j in range(CMP_STRIDE):
                rows = x_ref[col, pl.ds(j, n, stride=CMP_STRIDE), :]
                acc = acc + _dot(rows.astype(BF16), wc_ref[kv, j])
            hid = jax.nn.gelu(acc[:, :LANES] + pltpu.roll(acc[:, LANES:], n - 1, axis=0) + pe_term).astype(BF16)
            if kv == 0:
                kc = _dot(hid, w2k_ref[...])
                kc = _rotate(kc, _tile_lanes(cos_ref[...], 2), _tile_lanes(sgn_ref[...], 2), ROT_DIM // 2)
                kc_ref[0, 2 * cp] = kc[:, :LANES].astype(kc_ref.dtype)
                kc_ref[0, 2 * cp + 1] = kc[:, LANES:].astype(kc_ref.dtype)
            else:
                vc_t = _dot_nt(w2vt_ref[...], hid)
                vct_ref[0, 2 * cp] = _values_t(vc_t[:dh, :], n)
                vct_ref[0, 2 * cp + 1] = _values_t(vc_t[dh:, :], n)


def _compress(cmp_in, batch, seq, pe, w1, wc, w2k, w2v_t, cos, sgn):
    g = NSA_KV_GROUPS
    n = seq // CMP_STRIDE
    full = lambda a: pl.BlockSpec(a.shape, lambda bi: (0,) * a.ndim)
    return pl.pallas_call(
        _compress_kernel,
        grid=(batch,),
        in_specs=[pl.BlockSpec((cmp_in.shape[0], seq, LANES), lambda bi: (0, bi, 0)),
                  full(pe), full(w1), full(wc), full(w2k), full(w2v_t), full(cos), full(sgn)],
        out_specs=[pl.BlockSpec((1, g, n, LANES), lambda bi: (bi, 0, 0, 0)),
                   pl.BlockSpec((1, g, V_ROWS, n), lambda bi: (bi, 0, 0, 0))],
        out_shape=[jax.ShapeDtypeStruct((batch, g, n, LANES), BF16),
                   jax.ShapeDtypeStruct((batch, g, V_ROWS, n), BF16)],
        compiler_params=_cparams(("parallel",)),
        name="compress",
    )(cmp_in, pe, w1, wc, w2k, w2v_t, cos, sgn)


def _topk_bias_t(imp_t, tpos, last_block, val_ref, cnt_ref):
    nb, tq = imp_t.shape
    j = lax.broadcasted_iota(jnp.int32, (nb, tq), 0)
    cur = tpos // SLC_LEN
    forced = (j == 0) | (j == cur) | (j == cur - 1)
    causal = j <= cur
    val_ref[...] = jnp.where(forced, BIG, jnp.where(causal, imp_t, NEG))
    cnt_ref[...] = jnp.zeros((nb, tq), F32)
    nchunk = nb // SUBLANES
    rows = lax.broadcasted_iota(jnp.int32, (SUBLANES, tq), 0)
    for c in range(nchunk):
        for ic in range(nchunk):
            @pl.when(max(c, ic) * SUBLANES <= last_block)
            def _():
                vc = val_ref[c * SUBLANES:(c + 1) * SUBLANES, :]
                vic = val_ref[ic * SUBLANES:(ic + 1) * SUBLANES, :]
                acc = cnt_ref[c * SUBLANES:(c + 1) * SUBLANES, :]
                for r in range(SUBLANES):
                    if c == ic and r == 0:
                        continue
                    vi = vic if r == 0 else pltpu.roll(vic, SUBLANES - r, axis=0)
                    if c > ic:
                        ahead = jnp.where(vi >= vc, 1.0, 0.0)
                    elif c < ic:
                        ahead = jnp.where(vi > vc, 1.0, 0.0)
                    else:
                        ahead = jnp.where(rows >= SUBLANES - r, jnp.where(vi >= vc, 1.0, 0.0),
                                          jnp.where(vi > vc, 1.0, 0.0))
                    acc = acc + ahead
                cnt_ref[c * SUBLANES:(c + 1) * SUBLANES, :] = acc
    return jnp.where((cnt_ref[...] < float(SLC_TOPK)) & causal, 0.0, NEG)


def _value_chunks_t(vt_ref, g, first, count):
    return jnp.concatenate([vt_ref[0, g, first + i] for i in range(count)], axis=1)


def _max_key_norm2(k_slabs):
    width = k_slabs.shape[1]
    row = lax.broadcasted_iota(jnp.int32, (width, LANES), 0)
    lane = lax.broadcasted_iota(jnp.int32, (width, LANES), 1)
    dims_of_group = jnp.where((row // LANES == lane) & (row % LANES < NSA_HEAD_DIM), 1.0, 0.0).astype(BF16)
    k = k_slabs.astype(F32)
    norm2 = _dot((k * k).astype(BF16), dims_of_group)
    return jnp.broadcast_to(jnp.max(norm2, axis=0, keepdims=True), (SUBLANES, LANES))


def _nsa_attn_kernel(q_ref, gt_ref, kc_ref, vct_ref, ks_ref, vst_ref, kw_ref, vwt_ref, kn_ref, c2s_ref, o_ref,
                     val_ref, cnt_ref, imp_ref, acc_ref, *, n_slc, tk):
    tq = q_ref.shape[1]
    ncmp = kc_ref.shape[2]
    ngrp = NSA_KV_GROUPS
    nrep = NSA_HEADS // ngrp
    dh = NSA_HEAD_DIM
    groups = range(ngrp)
    t0 = pl.program_id(1) * tq
    head = lambda x, r: x[:, r * tq:(r + 1) * tq]
    grp_lanes = lambda g: slice(g * LANES, (g + 1) * LANES)

    kn = jnp.max(kn_ref[...], axis=0)
    kmax = [jnp.sqrt(kn[j, 0:1, :]) * NORM_MARGIN for j in range(2)]

    tpos = t0 + lax.broadcasted_iota(jnp.int32, (1, tq), 1)
    q_t = []
    for g in groups:
        cols = [q_ref[(g * nrep + r) * dh:(g * nrep + r + 1) * dh, :] for r in range(nrep)]
        q_t.append(jnp.concatenate([jnp.concatenate(cols, axis=1), jnp.zeros((LANES - dh, nrep * tq), F32)], axis=0))

    q_norm = [jnp.sqrt(jnp.sum(q_t[g] * q_t[g], axis=0, keepdims=True)) for g in groups]
    shift_s = [q_norm[g] * kmax[0][:, g:g + 1] for g in groups]
    shift_w = [q_norm[g] * kmax[1][:, g:g + 1] for g in groups]
    worst = functools.reduce(jnp.maximum, shift_s + shift_w)
    fixed_shift_ok = jnp.max(worst) < MAX_FIXED_SHIFT

    last = (t0 + tq - 1) // tk
    k_last = last * tk + lax.broadcasted_iota(jnp.int32, (tk, tq), 0)
    causal_bias = jnp.where(k_last <= tpos, 0.0, NEG)
    span = WINDOW + tq
    start = pl.multiple_of(jnp.maximum(t0 - WINDOW, 0), tq)
    dist = tpos - (start + lax.broadcasted_iota(jnp.int32, (span, tq), 0))
    band_bias = jnp.where((dist >= 0) & (dist < WINDOW), 0.0, NEG)
    key_tile = lambda c, g: ks_ref[pl.ds(pl.multiple_of(c * tk, tk), tk), grp_lanes(g)]
    val_tile = lambda c, g: _value_chunks_t(vst_ref, g, c * (tk // LANES), tk // LANES)
    win_keys = lambda g: kw_ref[pl.ds(start, span), grp_lanes(g)]
    win_vals = lambda g: _value_chunks_t(vwt_ref, g, start // LANES, span // LANES)
    all_heads = lambda x: jnp.concatenate([x] * nrep, axis=1)

    cmp_end = lax.broadcasted_iota(jnp.int32, (ncmp, tq), 0) * CMP_STRIDE + (CMP_LEN - 1)
    vis_bias = jnp.where(cmp_end <= tpos, 0.0, NEG)
    any_vis = jnp.where(tpos >= CMP_LEN - 1, 1.0, 0.0)

    def compressed_branch(g):
        c2s = c2s_ref[...]
        s = _dot(kc_ref[0, g], q_t[g].astype(BF16))
        p_sum = jnp.zeros((ncmp, tq), F32)
        p_cmp = []
        for r in range(nrep):
            sr = head(s, r) + vis_bias
            e = jnp.exp2(sr - jnp.max(sr, axis=0, keepdims=True))
            pr = e * (any_vis / jnp.sum(e, axis=0, keepdims=True))
            p_sum = p_sum + pr
            p_cmp.append(pr.astype(BF16))
        acc_ref[2, g] = _dot(vct_ref[0, g], jnp.concatenate(p_cmp, axis=1))
        p_hi, p_lo = _split_bf16(p_sum)
        imp_ref[:, g * tq:(g + 1) * tq] = (_dot(c2s, p_hi) + _dot(c2s, p_lo))[:n_slc, :]

    @pl.when(fixed_shift_ok)
    def _():
        zeros_hi = jnp.zeros((dh - 1, nrep * tq), F32)
        for g in groups:
            qt_win = jnp.concatenate([q_t[g][:dh, :], -shift_w[g], zeros_hi], axis=0).astype(BF16)
            p = jnp.exp2(_dot(win_keys(g), qt_win) + all_heads(band_bias)).astype(BF16)
            acc_ref[1, g] = _dot(win_vals(g), p)
            compressed_branch(g)

    @pl.when(jnp.logical_not(fixed_shift_ok))
    def _():
        for g in groups:
            compressed_branch(g)
            sw = _dot(win_keys(g), q_t[g].astype(BF16))
            p_win = []
            for r in range(nrep):
                sr = head(sw, r) + band_bias
                p_win.append(jnp.exp2(sr - jnp.max(sr, axis=0, keepdims=True)).astype(BF16))
            acc_ref[1, g] = _dot(win_vals(g), jnp.concatenate(p_win, axis=1))

    bias_t = _topk_bias_t(imp_ref[...], jnp.concatenate([tpos] * ngrp, axis=1),
                          (t0 + tq - 1) // SLC_LEN, val_ref, cnt_ref)
    if n_slc < dh:
        bias_t = jnp.concatenate([bias_t, jnp.full((dh - n_slc, ngrp * tq), NEG, F32)], axis=0)
    bias_rows = [jnp.concatenate([head(bias_t, g)] * nrep, axis=1) for g in groups]

    @pl.when(fixed_shift_ok)
    def _():
        qt_sel = [jnp.concatenate([q_t[g][:dh, :], bias_rows[g] - shift_s[g]], axis=0).astype(BF16) for g in groups]

        def partial_pv(c, g, token_bias=None, nkeys=tk):
            k0 = pl.multiple_of(c * tk, tk)
            sc = _dot(ks_ref[pl.ds(k0, nkeys), grp_lanes(g)], qt_sel[g])
            if token_bias is not None:
                sc = sc + all_heads(token_bias[:nkeys, :])
            vals = _value_chunks_t(vst_ref, g, c * (tk // LANES), nkeys // LANES)
            return _dot(vals, jnp.exp2(sc).astype(BF16))

        def body(i, accs):
            return tuple(accs[g] + partial_pv(2 * i, g) + partial_pv(2 * i + 1, g) for g in groups)

        accs = lax.fori_loop(0, last // 2, body, tuple(jnp.zeros((V_ROWS, nrep * tq), F32) for _ in groups))
        for g in groups:
            acc_ref[0, g] = accs[g]

        @pl.when(last % 2 == 1)
        def _():
            for g in groups:
                acc_ref[0, g] += partial_pv(last - 1, g)

        needs_half = (t0 % tk) + tq <= tk // 2

        @pl.when(needs_half)
        def _():
            for g in groups:
                acc_ref[0, g] += partial_pv(last, g, causal_bias, tk // 2)

        @pl.when(jnp.logical_not(needs_half))
        def _():
            for g in groups:
                acc_ref[0, g] += partial_pv(last, g, causal_bias)

    @pl.when(jnp.logical_not(fixed_shift_ok))
    def _():
        qt_sel = [jnp.concatenate([q_t[g][:dh, :], bias_rows[g]], axis=0).astype(BF16) for g in groups]

        def tile(c, carry, token_bias):
            out = []
            for g in groups:
                m, acc = carry[g]
                sc = _dot(key_tile(c, g), qt_sel[g])
                ps, ms, alphas = [], [], []
                for r in range(nrep):
                    sr = head(sc, r)
                    if token_bias is not None:
                        sr = sr + token_bias
                    m_new = jnp.maximum(head(m, r), jnp.max(sr, axis=0, keepdims=True))
                    alphas.append(jnp.exp2(head(m, r) - m_new))
                    ms.append(m_new)
                    ps.append(jnp.exp2(sr - m_new).astype(BF16))
                pv = _dot(val_tile(c, g), jnp.concatenate(ps, axis=1))
                out.append((jnp.concatenate(ms, axis=1), jnp.concatenate(alphas, axis=1) * acc + pv))
            return tuple(out)

        init = tuple((jnp.full((1, nrep * tq), NEG, F32), jnp.zeros((V_ROWS, nrep * tq), F32)) for _ in groups)
        carry = lax.fori_loop(0, last, lambda c, cr: tile(c, cr, None), init)
        carry = tile(last, carry, causal_bias)
        for g in groups:
            acc_ref[0, g] = carry[g][1]

    gt_t = gt_ref[...]
    heads = []
    for g in groups:
        acc_s = acc_ref[0, g]
        acc_w = acc_ref[1, g]
        o_cmp_g = acc_ref[2, g]
        o_slc = acc_s[:dh, :] * (1.0 / acc_s[dh:dh + 1, :])
        o_win = acc_w[:dh, :] * (1.0 / acc_w[dh:dh + 1, :])
        for r in range(nrep):
            row = 3 * (g * nrep + r)
            heads.append(gt_t[row:row + 1, :] * head(o_cmp_g, r)[:dh, :]
                         + gt_t[row + 1:row + 2, :] * head(o_slc, r) + gt_t[row + 2:row + 3, :] * head(o_win, r))
    o_ref[...] = jnp.transpose(jnp.concatenate(heads, axis=0)).astype(o_ref.dtype)


def _nsa_attention(q, gt, kc, vc_t, ks, vs_t, kw, vw_t, kn, batch, seq):
    g = NSA_KV_GROUPS
    tiles_per_row = kn.shape[0] // batch
    nq = seq // ATT_TQ
    qw = (NSA_HEADS // g) * NSA_HEAD_DIM
    n_slc = seq // SLC_LEN
    n_cmp = (seq - CMP_LEN) // CMP_STRIDE + 1
    n_cmp_pad = kc.shape[2]
    c_start = np.arange(n_cmp)[:, None] * CMP_STRIDE
    s_start = np.arange(n_slc)[None, :] * SLC_LEN
    overlap = np.clip(np.minimum(c_start + CMP_LEN, s_start + SLC_LEN)
                      - np.maximum(c_start, s_start), 0, None) / CMP_LEN
    assert n_slc <= NSA_HEAD_DIM and n_slc % SUBLANES == 0 and seq % ATT_TK == 0
    c2s_t = np.zeros((NSA_HEAD_DIM, n_cmp_pad))
    c2s_t[:n_slc, :n_cmp] = overlap.T
    hd = NSA_HEADS * NSA_HEAD_DIM
    tok_spec = lambda width: pl.BlockSpec((ATT_TQ, width), lambda b, qi: (b * nq + qi, 0))
    key_spec = pl.BlockSpec((seq, g * LANES), lambda b, qi: (b, 0))
    val_spec = pl.BlockSpec((1, g, seq // LANES, V_ROWS, LANES), lambda b, qi: (b, 0, 0, 0, 0))
    return pl.pallas_call(
        functools.partial(_nsa_attn_kernel, n_slc=n_slc, tk=ATT_TK),
        grid=(batch, nq),
        in_specs=[pl.BlockSpec((hd, ATT_TQ), lambda b, qi: (0, b * nq + qi)),
                  pl.BlockSpec((LANES, ATT_TQ), lambda b, qi: (0, b * nq + qi)),
                  pl.BlockSpec((1, g, n_cmp_pad, LANES), lambda b, qi: (b, 0, 0, 0)),
                  pl.BlockSpec((1, g, V_ROWS, n_cmp_pad), lambda b, qi: (b, 0, 0, 0)),
                  key_spec, val_spec, key_spec, val_spec,
                  pl.BlockSpec((tiles_per_row,) + kn.shape[1:], lambda b, qi: (b, 0, 0, 0)),
                  pl.BlockSpec(c2s_t.shape, lambda b, qi: (0, 0))],
        out_specs=tok_spec(hd),
        out_shape=jax.ShapeDtypeStruct((batch * seq, hd), BF16),
        scratch_shapes=[pltpu.VMEM((n_slc, g * ATT_TQ), F32)] * 3
        + [pltpu.VMEM((3, g, V_ROWS, (NSA_HEADS // g) * ATT_TQ), F32)],
        compiler_params=_cparams(("parallel", "arbitrary")),
        name="nsa_attention",
    )(q, gt, kc, vc_t, ks, vs_t, kw, vw_t, kn, jnp.asarray(c2s_t, BF16))


def _block_diag(w):
    nb, bi, bo = w.shape
    eye = jnp.eye(nb, dtype=w.dtype)
    return (eye[:, None, :, None] * w[:, :, None, :]).reshape(nb * bi, nb * bo)


def _nsa_weight(w_in):
    return jnp.pad(w_in, ((0, 0), (0, LANES - 3 * NSA_HEADS))).astype(BF16)


def _ab_mixer(h, g_attn, w_in, conv_w, conv_b, ga_w, ga_b, gx_w, gx_b, lam, batch, seq):
    proj = _proj_ab(h, g_attn, w_in.astype(BF16))
    gate_w = jnp.concatenate([_block_diag(ga_w), _block_diag(gx_w)], axis=1).astype(BF16)
    gate_b = jnp.concatenate([ga_b, gx_b])[None, :]
    lru_out = _lru(proj, batch, seq, conv_w, conv_b[None, :], gate_w, gate_b, lam[None, :])
    return [lru_out, _retention(proj, batch, seq)]


def _nsa_mixer(h, g_attn, w_in, pe_k, k_w1, k_w2, pe_v, v_w1, v_w2, batch, seq):
    q, cmp_in, ks, vs_t, kw, vw_t, gt, kn = _proj_nsa(h, g_attn, _nsa_weight(w_in), seq)
    dh = NSA_HEAD_DIM
    n_blk = seq // CMP_STRIDE
    pe = jnp.broadcast_to(jnp.stack([pe_k, pe_v]).reshape(2, 1, CMP_LEN * dh), (2, SUBLANES, CMP_LEN * dh))
    w1 = jnp.stack([k_w1, v_w1]).astype(BF16)
    w1r = w1.reshape(2, CMP_LEN, dh, dh)
    first, second, zero = w1r[:, :CMP_STRIDE], w1r[:, CMP_STRIDE:], jnp.zeros_like(w1r[:, :CMP_STRIDE])
    wc = jnp.concatenate([jnp.concatenate([first, zero, second, zero], axis=-1),
                          jnp.concatenate([zero, first, zero, second], axis=-1)], axis=-2)
    w2k = k_w2.astype(BF16)
    z = jnp.zeros_like(w2k)
    w2k = jnp.concatenate([jnp.concatenate([w2k, z, z, z], axis=1), jnp.concatenate([z, z, w2k, z], axis=1)], axis=0)
    w2v_t = v_w2.T.astype(BF16)
    zt = jnp.zeros_like(w2v_t)
    w2v_t = jnp.concatenate([jnp.concatenate([w2v_t, zt], axis=1), jnp.concatenate([zt, w2v_t], axis=1)], axis=0)
    cos, sgn = _rot_tables(np.arange(n_blk) * CMP_STRIDE + CMP_LEN - 1, ROT_DIM, ROPE_THETA, 1.0)
    kc, vc_t = _compress(cmp_in, batch, seq, pe, w1, wc, w2k, w2v_t, cos, sgn)
    return [_nsa_attention(q, gt, kc, vc_t, ks, vs_t, kw, vw_t, kn, batch, seq)]


def kernel(x, attn_norm, ab_w_in, conv_w, conv_b, gate_a_w, gate_a_b, gate_x_w, gate_x_b, lru_lambda,
           ab_w_out, nsa_w_in, cmp_pe_k, cmp_k_w1, cmp_k_w2, cmp_pe_v, cmp_v_w1, cmp_v_w2, nsa_w_out,
           ffn_norm, ffn_w1, ffn_w3, ffn_w2, final_norm):
    batch, seq, d = x.shape
    depth = attn_norm.shape[0]
    h = x.reshape(batch * seq, d)
    for layer in range(depth):
        i = layer // 2
        g_attn = attn_norm[layer][None, :]
        if layer % 2 == 0:
            mixes = _ab_mixer(h, g_attn, ab_w_in[i], conv_w[i], conv_b[i], gate_a_w[i], gate_a_b[i],
                              gate_x_w[i], gate_x_b[i], lru_lambda[i], batch, seq)
            w_out = ab_w_out[i]
        else:
            mixes = _nsa_mixer(h, g_attn, nsa_w_in[i], cmp_pe_k[i], cmp_k_w1[i], cmp_k_w2[i],
                               cmp_pe_v[i], cmp_v_w1[i], cmp_v_w2[i], batch, seq)
            w_out = nsa_w_out[i]
        last = layer == depth - 1
        h = _mix_ffn(h, mixes, w_out.astype(BF16), ffn_norm[layer][None, :],
                     ffn_w1[layer].astype(BF16), ffn_w3[layer].astype(BF16), ffn_w2[layer].astype(BF16),
                     final_g=final_norm[None, :] if last else None)
    return h.reshape(batch, seq, d)
```

```python
import functools

import numpy as np
import jax
import jax.numpy as jnp
from jax import lax
from jax.experimental import pallas as pl
from jax.experimental.pallas import tpu as pltpu

F32 = jnp.float32
BF16 = jnp.bfloat16

EPS = 1e-6
NEG = -1e30
BIG = 1e30
LOG2E = float(np.log2(np.e))

LRU_BLOCKS = 8
CONV_WIDTH = 4
LRU_C = 8.0
RET_HEADS = 8
RET_DIM = 64
RET_THETA = 10000.0
NSA_HEADS = 16
NSA_KV_GROUPS = 4
NSA_HEAD_DIM = 64
CMP_LEN = 32
CMP_STRIDE = 16
SLC_LEN = 64
SLC_TOPK = 16
WINDOW = 512
ROPE_THETA = 500000.0
ROT_DIM = NSA_HEAD_DIM // 4

LANES = 128
SUBLANES = 8
VMEM_LIMIT = 56 * 1024 * 1024

RET_CHUNK = 256
TOK_TILE = 512
PROJ_AB_TILE = 1024
LRU_TILE = 1024
ATT_TQ = 256
ATT_TK = 512
V_ROWS = NSA_HEAD_DIM + 16
MAX_FIXED_SHIFT = 32.0
NORM_MARGIN = 1.01


def _cparams(sem):
    return pltpu.CompilerParams(dimension_semantics=sem, vmem_limit_bytes=VMEM_LIMIT)


def _dot(a, b):
    return jnp.dot(a, b, preferred_element_type=F32)


def _dot_nt(a, b):
    return lax.dot_general(a, b, (((1,), (1,)), ((), ())), preferred_element_type=F32)


def _rms(x, g):
    return x * lax.rsqrt(jnp.mean(x * x, axis=-1, keepdims=True) + EPS) * g


def _tile_lanes(t, reps):
    return t if reps == 1 else jnp.concatenate([t] * reps, axis=1)


def _rotate(x, cos, sgn, half):
    n = x.shape[1]
    lane = lax.broadcasted_iota(jnp.int32, x.shape, 1)
    first = (lane % NSA_HEAD_DIM) < half
    partner = jnp.where(first, pltpu.roll(x, n - half, axis=1), pltpu.roll(x, half, axis=1))
    return x * cos + partner * sgn


def _group_slabs(x, fill):
    lane = lax.broadcasted_iota(jnp.int32, (x.shape[0], LANES), 1)
    lo = lane < NSA_HEAD_DIM
    out = []
    for c in range(x.shape[1] // LANES):
        xc = x[:, c * LANES:(c + 1) * LANES]
        out.append(jnp.where(lo, xc, fill))
        out.append(jnp.where(lo, pltpu.roll(xc, NSA_HEAD_DIM, axis=1), fill))
    return jnp.concatenate(out, axis=1)


def _values_t(v_t, ncols):
    return jnp.concatenate([v_t, jnp.ones((V_ROWS - NSA_HEAD_DIM, ncols), F32)], axis=0).astype(BF16)


def _store_values_t(v, o_ref):
    rows = v.shape[0]
    v_t = jnp.transpose(v)
    for g in range(v.shape[1] // NSA_HEAD_DIM):
        slab = _values_t(v_t[g * NSA_HEAD_DIM:(g + 1) * NSA_HEAD_DIM, :], rows)
        for c in range(rows // LANES):
            o_ref[0, g, c] = slab[:, c * LANES:(c + 1) * LANES]


def _rot_tables(pos, rot_dim, theta, scale):
    half = rot_dim // 2
    inv = theta ** (-np.arange(0, rot_dim, 2, dtype=np.float64) / rot_dim)
    ang = np.asarray(pos, np.float64)[:, None] * inv
    cos = np.ones((len(pos), NSA_HEAD_DIM))
    sgn = np.zeros((len(pos), NSA_HEAD_DIM))
    cos[:, :half] = np.cos(ang)
    cos[:, half:rot_dim] = np.cos(ang)
    sgn[:, :half] = -np.sin(ang)
    sgn[:, half:rot_dim] = np.sin(ang)
    cos = np.tile(cos * scale, (1, 2))
    sgn = np.tile(sgn * scale, (1, 2))
    return jnp.asarray(cos, F32), jnp.asarray(sgn, F32)


def _ret_decay_tables(chunk):
    log_g = np.log1p(-(2.0 ** (-5.0 - np.arange(RET_HEADS, dtype=np.float64))))
    ci = np.arange(chunk, dtype=np.float64)
    diff = ci[:, None] - ci[None, :]
    inner = np.where(diff >= 0, np.exp(np.maximum(diff, 0.0) * log_g[:, None, None]), 0.0)
    lane_head = np.repeat(np.arange(RET_HEADS), RET_DIM)
    qdec = np.exp((ci[:, None] + 1.0) * log_g[lane_head][None, :])
    kdec = np.exp((chunk - 1.0 - ci[:, None]) * log_g[lane_head][None, :])
    cdec = np.exp(chunk * log_g[lane_head])[None, :]
    return (jnp.asarray(inner, F32), jnp.asarray(qdec, F32), jnp.asarray(kdec, F32),
            jnp.asarray(cdec, F32))


def _proj_ab_kernel(x_ref, g_ref, w_ref, o_ref):
    xn = _rms(x_ref[...], g_ref[...]).astype(BF16)
    o_ref[...] = _dot(xn, w_ref[...])


def _proj_ab(x2, g, w):
    t, d = x2.shape
    n = w.shape[1]
    return pl.pallas_call(
        _proj_ab_kernel,
        grid=(t // PROJ_AB_TILE,),
        in_specs=[pl.BlockSpec((PROJ_AB_TILE, d), lambda i: (i, 0)),
                  pl.BlockSpec((1, d), lambda i: (0, 0)),
                  pl.BlockSpec((d, n), lambda i: (0, 0), pipeline_mode=pl.Buffered(1))],
        out_specs=pl.BlockSpec((PROJ_AB_TILE, n), lambda i: (i, 0)),
        out_shape=jax.ShapeDtypeStruct((t, n), F32),
        compiler_params=_cparams(("parallel",)),
        name="proj_ab",
    )(x2, g, w)


def _lru_kernel(y_ref, x_ref, cw_ref, cb_ref, gw_ref, gb_ref, lam_ref, o_ref,
                xp_ref, a_ref, u_ref, h_ref, carry_ref):
    ts, w = x_ref.shape
    row = lax.broadcasted_iota(jnp.int32, (SUBLANES, w), 0)

    @pl.when(pl.program_id(1) == 0)
    def _():
        xp_ref[...] = jnp.zeros_like(xp_ref)
        carry_ref[...] = jnp.zeros_like(carry_ref)

    x = x_ref[...]
    prev = xp_ref[...]
    xc = cb_ref[...] + x * cw_ref[CONV_WIDTH - 1:CONV_WIDTH, :]
    for s in range(1, CONV_WIDTH):
        rolled = pltpu.roll(x, s, axis=0)
        head_rows = jnp.where(row < s, pltpu.roll(prev, s, axis=0), rolled[:SUBLANES, :])
        shifted = jnp.concatenate([head_rows, rolled[SUBLANES:, :]], axis=0)
        xc = xc + shifted * cw_ref[CONV_WIDTH - 1 - s:CONV_WIDTH - s, :]
    xp_ref[...] = x[ts - SUBLANES:, :]

    gates = _dot(xc.astype(BF16), gw_ref[...]) + gb_ref[...]
    r = jax.nn.sigmoid(gates[:, :w])
    i = jax.nn.sigmoid(gates[:, w:])
    log_a = -LRU_C * r * jax.nn.softplus(-lam_ref[...])
    a = jnp.exp(log_a)
    z = jnp.tanh(-log_a) * (a * a + 1.0)
    a_ref[...] = a
    u_ref[...] = jnp.where(z > 0.0, z * lax.rsqrt(z), 0.0) * (i * xc)

    def body(gi, carry):
        r0 = pl.multiple_of(gi * SUBLANES, SUBLANES)
        a = a_ref[pl.ds(r0, SUBLANES), :]
        u = u_ref[pl.ds(r0, SUBLANES), :]
        for d in (1, 2, 4):
            a_s = pltpu.roll(a, d, axis=0)
            u_s = pltpu.roll(u, d, axis=0)
            m = row >= d
            u = jnp.where(m, a * u_s + u, u)
            a = jnp.where(m, a * a_s, a)
        h = a * carry + u
        h_ref[pl.ds(r0, SUBLANES), :] = h
        return h[SUBLANES - 1:SUBLANES, :]

    carry_ref[...] = lax.fori_loop(0, ts // SUBLANES, body, carry_ref[...])
    o_ref[...] = (h_ref[...] * jax.nn.gelu(y_ref[...])).astype(o_ref.dtype)


def _lru(proj, batch, seq, conv_w, conv_b, gate_w, gate_b, lam):
    w = conv_w.shape[1]
    ns = seq // LRU_TILE
    return pl.pallas_call(
        _lru_kernel,
        grid=(batch, ns),
        in_specs=[pl.BlockSpec((LRU_TILE, w), lambda b, s: (b * ns + s, 0)),
                  pl.BlockSpec((LRU_TILE, w), lambda b, s: (b * ns + s, 1)),
                  pl.BlockSpec((CONV_WIDTH, w), lambda b, s: (0, 0)),
                  pl.BlockSpec((1, w), lambda b, s: (0, 0)),
                  pl.BlockSpec((w, 2 * w), lambda b, s: (0, 0)),
                  pl.BlockSpec((1, 2 * w), lambda b, s: (0, 0)),
                  pl.BlockSpec((1, w), lambda b, s: (0, 0))],
        out_specs=pl.BlockSpec((LRU_TILE, w), lambda b, s: (b * ns + s, 0)),
        out_shape=jax.ShapeDtypeStruct((batch * seq, w), BF16),
        scratch_shapes=[pltpu.VMEM((SUBLANES, w), F32),
                        pltpu.VMEM((LRU_TILE, w), F32),
                        pltpu.VMEM((LRU_TILE, w), F32),
                        pltpu.VMEM((LRU_TILE, w), F32),
                        pltpu.VMEM((1, w), F32)],
        compiler_params=_cparams(("parallel", "arbitrary")),
        name="rg_lru",
    )(proj, proj, conv_w, conv_b, gate_w, gate_b, lam)


def _split_bf16(x):
    hi = x.astype(BF16)
    lo = (x - hi.astype(F32)).astype(BF16)
    return hi, lo


def _ret_kernel(q_ref, k_ref, v_ref, g_ref, cq_ref, sq_ref, ck_ref, sk_ref,
                inner_ref, qdec_ref, kdec_ref, cdec_ref, avg_ref, o_ref, st_ref):
    c, w = q_ref.shape
    npair = w // LANES
    b = pl.program_id(1)

    @pl.when(pl.program_id(0) == 0)
    def _():
        st_ref[b] = jnp.zeros(st_ref.shape[1:], F32)

    half = RET_DIM // 2
    q = _rotate(q_ref[...], _tile_lanes(cq_ref[...], npair), _tile_lanes(sq_ref[...], npair), half)
    k = _rotate(k_ref[...], _tile_lanes(ck_ref[...], npair), _tile_lanes(sk_ref[...], npair), half)
    v = v_ref[...]
    qd = (q * qdec_ref[...]).astype(BF16)
    kd = k * kdec_ref[...]
    q = q.astype(BF16)
    kb = k.astype(BF16)
    vb = v.astype(BF16)

    lane = lax.broadcasted_iota(jnp.int32, (c, LANES), 1)
    lo = lane < RET_DIM
    rr = lax.broadcasted_iota(jnp.int32, (LANES, LANES), 0) < RET_DIM
    cc = lax.broadcasted_iota(jnp.int32, (LANES, LANES), 1) < RET_DIM
    same_head = rr == cc
    zero = jnp.zeros((c, LANES), BF16)

    outs = []
    for p in range(npair):
        sl = slice(p * LANES, (p + 1) * LANES)
        qp, kp, vp = q[:, sl], kb[:, sl], vb[:, sl]
        s_lo = _dot_nt(jnp.where(lo, qp, zero), kp) * inner_ref[2 * p]
        s_hi = _dot_nt(jnp.where(lo, zero, qp), kp) * inner_ref[2 * p + 1]
        o = jnp.where(lo, _dot(s_lo.astype(BF16), vp), _dot(s_hi.astype(BF16), vp))
        st = st_ref[b, p]
        o = o + _dot(qd[:, sl], st.astype(BF16))
        kv = _dot(jnp.transpose(kd[:, sl]).astype(BF16), vp)
        st_ref[b, p] = st * cdec_ref[:, sl] + jnp.where(same_head, kv, 0.0)
        outs.append(o)

    o_all = jnp.concatenate(outs, axis=0)
    avg = avg_ref[...]
    mu = _dot(o_all.astype(BF16), avg)
    dlt = o_all - mu
    var = _dot((dlt * dlt).astype(BF16), avg)
    y = dlt * lax.rsqrt(var + EPS)
    y = jnp.concatenate([y[p * c:(p + 1) * c, :] for p in range(npair)], axis=1)
    o_ref[...] = (y * jax.nn.silu(g_ref[...])).astype(o_ref.dtype)


def _retention(proj, batch, seq):
    w = RET_HEADS * RET_DIM
    c = RET_CHUNK
    nc = seq // c
    npair = w // LANES
    pos = np.arange(seq)
    cq, sq = _rot_tables(pos, RET_DIM, RET_THETA, 1.0)
    ck, sk = _rot_tables(pos, RET_DIM, RET_THETA, RET_DIM ** -0.5)
    inner, qdec, kdec, cdec = _ret_decay_tables(c)
    blk = np.arange(LANES) // RET_DIM
    avg = jnp.asarray((blk[:, None] == blk[None, :]) / RET_DIM, BF16)
    col0 = 2
    tok = lambda j: pl.BlockSpec((c, w), lambda ci, b, j=j: (b * nc + ci, j))
    tab = pl.BlockSpec((c, LANES), lambda ci, b: (ci, 0))
    const = lambda shape: pl.BlockSpec(shape, lambda ci, b: (0,) * len(shape))
    return pl.pallas_call(
        _ret_kernel,
        grid=(nc, batch),
        in_specs=[tok(col0), tok(col0 + 1), tok(col0 + 2), tok(col0 + 3),
                  tab, tab, tab, tab,
                  const((RET_HEADS, c, c)), const((c, w)), const((c, w)), const((1, w)),
                  const((LANES, LANES))],
        out_specs=pl.BlockSpec((c, w), lambda ci, b: (b * nc + ci, 0)),
        out_shape=jax.ShapeDtypeStruct((batch * seq, w), BF16),
        scratch_shapes=[pltpu.VMEM((batch, npair, LANES, LANES), F32)],
        compiler_params=_cparams(("arbitrary", "arbitrary")),
        name="retention",
    )(proj, proj, proj, proj, cq, sq, ck, sk, inner, qdec, kdec, cdec, avg)


def _mix_ffn_kernel(*refs, n_mix, final):
    h_ref = refs[0]
    mix_refs = refs[1:1 + n_mix]
    wo_ref, g_ref, w1_ref, w3_ref, w2_ref = refs[1 + n_mix:6 + n_mix]
    rest = refs[6 + n_mix:]
    fg_ref = rest[0] if final else None
    o_ref = rest[-1]

    mixed = mix_refs[0][...] if n_mix == 1 else jnp.concatenate([r[...] for r in mix_refs], axis=1)
    h1 = h_ref[...] + _dot(mixed, wo_ref[...])
    xn = _rms(h1, g_ref[...]).astype(BF16)
    act = (jax.nn.silu(_dot(xn, w1_ref[...])) * _dot(xn, w3_ref[...])).astype(BF16)
    h2 = h1 + _dot(act, w2_ref[...])
    if final:
        h2 = _rms(h2, fg_ref[...])
    o_ref[...] = h2


def _mix_ffn(h2d, mixes, wo, g, w1, w3, w2, final_g=None):
    t, d = h2d.shape
    hid = w1.shape[1]
    final = final_g is not None
    row = lambda width: pl.BlockSpec((TOK_TILE, width), lambda i: (i, 0))
    const = lambda shape: pl.BlockSpec(shape, lambda i: (0, 0), pipeline_mode=pl.Buffered(1))
    in_specs = [row(d)] + [row(m.shape[1]) for m in mixes] + [
        const(wo.shape), const((1, d)), const((d, hid)), const((d, hid)), const((hid, d))]
    args = [h2d, *mixes, wo, g, w1, w3, w2]
    if final:
        in_specs.append(const((1, d)))
        args.append(final_g)
    return pl.pallas_call(
        functools.partial(_mix_ffn_kernel, n_mix=len(mixes), final=final),
        grid=(t // TOK_TILE,),
        in_specs=in_specs,
        out_specs=row(d),
        out_shape=jax.ShapeDtypeStruct((t, d), F32),
        compiler_params=_cparams(("parallel",)),
        name="mix_ffn_final" if final else "mix_ffn",
    )(*args)


def _proj_nsa_kernel(x_ref, g_ref, w_ref, cq_ref, sq_ref, ck_ref, sk_ref,
                     q_ref, cmp_ref, ks_ref, vs_ref, kw_ref, vw_ref, gt_ref, kn_ref, *, seq_tiles):
    hd = q_ref.shape[0]
    kvw = NSA_KV_GROUPS * NSA_HEAD_DIM
    xn = _rms(x_ref[...], g_ref[...]).astype(BF16)
    proj = _dot(xn, w_ref[...])
    half = ROT_DIM // 2
    q = proj[:, :hd]
    q_ref[...] = jnp.transpose(_rotate(q, _tile_lanes(cq_ref[...], hd // LANES),
                                       _tile_lanes(sq_ref[...], hd // LANES), half))
    kv = proj[:, hd:hd + 6 * kvw]
    for c in range(cmp_ref.shape[0]):
        cmp_ref[c] = kv[:, c * LANES:(c + 1) * LANES]
    ck = _tile_lanes(ck_ref[...], kvw // LANES)
    sk = _tile_lanes(sk_ref[...], kvw // LANES)
    rows = x_ref.shape[0]
    tpos = (pl.program_id(0) % seq_tiles) * rows + lax.broadcasted_iota(jnp.int32, (rows, LANES), 0)
    lane = lax.broadcasted_iota(jnp.int32, (rows, LANES), 1)
    block_onehot = jnp.where(lane - NSA_HEAD_DIM == tpos // SLC_LEN, 1.0, 0.0)
    ks = _group_slabs(_rotate(kv[:, 2 * kvw:3 * kvw], ck, sk, half), block_onehot).astype(ks_ref.dtype)
    one_lane = jnp.where(lane == NSA_HEAD_DIM, 1.0, 0.0)
    kw = _group_slabs(_rotate(kv[:, 4 * kvw:5 * kvw], ck, sk, half), one_lane).astype(kw_ref.dtype)
    ks_ref[...] = ks
    kw_ref[...] = kw
    kn_ref[0, 0] = _max_key_norm2(ks)
    kn_ref[0, 1] = _max_key_norm2(kw)
    _store_values_t(kv[:, 3 * kvw:4 * kvw], vs_ref)
    _store_values_t(kv[:, 5 * kvw:6 * kvw], vw_ref)
    gt_ref[...] = jnp.transpose(jax.nn.sigmoid(proj[:, hd + 6 * kvw:]))


def _proj_nsa(x2, g, w, seq):
    t, d = x2.shape
    hd = NSA_HEADS * NSA_HEAD_DIM
    kvw = NSA_KV_GROUPS * NSA_HEAD_DIM
    gw = NSA_KV_GROUPS * LANES
    n = w.shape[1]
    ns = seq // TOK_TILE
    pos = np.arange(seq)
    cq, sq = _rot_tables(pos, ROT_DIM, ROPE_THETA, NSA_HEAD_DIM ** -0.5 * LOG2E)
    ck, sk = _rot_tables(pos, ROT_DIM, ROPE_THETA, 1.0)
    row = lambda width: pl.BlockSpec((TOK_TILE, width), lambda i: (i, 0))
    tab = pl.BlockSpec((TOK_TILE, LANES), lambda i: (i % ns, 0))
    chunks = TOK_TILE // LANES
    val_t = (pl.BlockSpec((1, NSA_KV_GROUPS, chunks, V_ROWS, LANES), lambda i: (i // ns, 0, i % ns, 0, 0)),
             jax.ShapeDtypeStruct((t // seq, NSA_KV_GROUPS, seq // LANES, V_ROWS, LANES), BF16))
    tok = lambda wd, dt: (row(wd), jax.ShapeDtypeStruct((t, wd), dt))
    cmp_planes = (pl.BlockSpec((2 * kvw // LANES, TOK_TILE, LANES), lambda i: (0, i, 0)),
                  jax.ShapeDtypeStruct((2 * kvw // LANES, t, LANES), F32))
    tok_t = lambda wd: (pl.BlockSpec((wd, TOK_TILE), lambda i: (0, i)), jax.ShapeDtypeStruct((wd, t), F32))
    key_norms = (pl.BlockSpec((1, 2, SUBLANES, LANES), lambda i: (i, 0, 0, 0)),
                 jax.ShapeDtypeStruct((t // TOK_TILE, 2, SUBLANES, LANES), F32))
    outs = [tok_t(hd), cmp_planes, tok(gw, BF16), val_t, tok(gw, BF16), val_t, tok_t(LANES), key_norms]
    return pl.pallas_call(
        functools.partial(_proj_nsa_kernel, seq_tiles=ns),
        grid=(t // TOK_TILE,),
        in_specs=[row(d), pl.BlockSpec((1, d), lambda i: (0, 0)),
                  pl.BlockSpec((d, n), lambda i: (0, 0), pipeline_mode=pl.Buffered(1)),
                  tab, tab, tab, tab],
        out_specs=[spec for spec, _ in outs],
        out_shape=[shape for _, shape in outs],
        compiler_params=_cparams(("parallel",)),
        name="proj_nsa",
    )(x2, g, w, cq, sq, ck, sk)


def _compress_kernel(x_ref, pe_ref, w1_ref, wc_ref, w2k_ref, w2vt_ref, cos_ref, sgn_ref, kc_ref, vct_ref):
    n = kc_ref.shape[2]
    dh = NSA_HEAD_DIM
    for kv in range(2):
        pe_term = _dot(pe_ref[kv].astype(BF16), w1_ref[kv])[0:1, :]
        pe_term = jnp.concatenate([pe_term, pe_term], axis=1)
        for cp in range(NSA_KV_GROUPS // 2):
            col = kv * (NSA_KV_GROUPS // 2) + cp
            acc = jnp.zeros((n, 2 * LANES), F32)
            for j in range(CMP_STRIDE):
                rows = x_ref[col, pl.ds(j, n, stride=CMP_STRIDE), :]
                acc = acc + _dot(rows.astype(BF16), wc_ref[kv, j])
            hid = jax.nn.gelu(acc[:, :LANES] + pltpu.roll(acc[:, LANES:], n - 1, axis=0) + pe_term).astype(BF16)
            if kv == 0:
                kc = _dot(hid, w2k_ref[...])
                kc = _rotate(kc, _tile_lanes(cos_ref[...], 2), _tile_lanes(sgn_ref[...], 2), ROT_DIM // 2)
                kc_ref[0, 2 * cp] = kc[:, :LANES].astype(kc_ref.dtype)
                kc_ref[0, 2 * cp + 1] = kc[:, LANES:].astype(kc_ref.dtype)
            else:
                vc_t = _dot_nt(w2vt_ref[...], hid)
                vct_ref[0, 2 * cp] = _values_t(vc_t[:dh, :], n)
                vct_ref[0, 2 * cp + 1] = _values_t(vc_t[dh:, :], n)


def _compress(cmp_in, batch, seq, pe, w1, wc, w2k, w2v_t, cos, sgn):
    g = NSA_KV_GROUPS
    n = seq // CMP_STRIDE
    full = lambda a: pl.BlockSpec(a.shape, lambda bi: (0,) * a.ndim)
    return pl.pallas_call(
        _compress_kernel,
        grid=(batch,),
        in_specs=[pl.BlockSpec((cmp_in.shape[0], seq, LANES), lambda bi: (0, bi, 0)),
                  full(pe), full(w1), full(wc), full(w2k), full(w2v_t), full(cos), full(sgn)],
        out_specs=[pl.BlockSpec((1, g, n, LANES), lambda bi: (bi, 0, 0, 0)),
                   pl.BlockSpec((1, g, V_ROWS, n), lambda bi: (bi, 0, 0, 0))],
        out_shape=[jax.ShapeDtypeStruct((batch, g, n, LANES), BF16),
                   jax.ShapeDtypeStruct((batch, g, V_ROWS, n), BF16)],
        compiler_params=_cparams(("parallel",)),
        name="compress",
    )(cmp_in, pe, w1, wc, w2k, w2v_t, cos, sgn)


def _topk_bias_t(imp_t, tpos, last_block, val_ref, cnt_ref):
    nb, tq = imp_t.shape
    j = lax.broadcasted_iota(jnp.int32, (nb, tq), 0)
    cur = tpos // SLC_LEN
    forced = (j == 0) | (j == cur) | (j == cur - 1)
    causal = j <= cur
    val_ref[...] = jnp.where(forced, BIG, jnp.where(causal, imp_t, NEG))
    cnt_ref[...] = jnp.zeros((nb, tq), F32)
    nchunk = nb // SUBLANES
    rows = lax.broadcasted_iota(jnp.int32, (SUBLANES, tq), 0)
    for c in range(nchunk):
        for ic in range(nchunk):
            @pl.when(max(c, ic) * SUBLANES <= last_block)
            def _():
                vc = val_ref[c * SUBLANES:(c + 1) * SUBLANES, :]
                vic = val_ref[ic * SUBLANES:(ic + 1) * SUBLANES, :]
                acc = cnt_ref[c * SUBLANES:(c + 1) * SUBLANES, :]
                for r in range(SUBLANES):
                    if c == ic and r == 0:
                        continue
                    vi = vic if r == 0 else pltpu.roll(vic, SUBLANES - r, axis=0)
                    if c > ic:
                        ahead = jnp.where(vi >= vc, 1.0, 0.0)
                    elif c < ic:
                        ahead = jnp.where(vi > vc, 1.0, 0.0)
                    else:
                        ahead = jnp.where(rows >= SUBLANES - r, jnp.where(vi >= vc, 1.0, 0.0),
                                          jnp.where(vi > vc, 1.0, 0.0))
                    acc = acc + ahead
                cnt_ref[c * SUBLANES:(c + 1) * SUBLANES, :] = acc
    return jnp.where((cnt_ref[...] < float(SLC_TOPK)) & causal, 0.0, NEG)


def _value_chunks_t(vt_ref, g, first, count):
    return jnp.concatenate([vt_ref[0, g, first + i] for i in range(count)], axis=1)


def _max_key_norm2(k_slabs):
    width = k_slabs.shape[1]
    row = lax.broadcasted_iota(jnp.int32, (width, LANES), 0)
    lane = lax.broadcasted_iota(jnp.int32, (width, LANES), 1)
    dims_of_group = jnp.where((row // LANES == lane) & (row % LANES < NSA_HEAD_DIM), 1.0, 0.0).astype(BF16)
    k = k_slabs.astype(F32)
    norm2 = _dot((k * k).astype(BF16), dims_of_group)
    return jnp.broadcast_to(jnp.max(norm2, axis=0, keepdims=True), (SUBLANES, LANES))


def _nsa_attn_kernel(q_ref, gt_ref, kc_ref, vct_ref, ks_ref, vst_ref, kw_ref, vwt_ref, kn_ref, c2s_ref, o_ref,
                     val_ref, cnt_ref, imp_ref, acc_ref, *, n_slc, tk):
    tq = q_ref.shape[1]
    ncmp = kc_ref.shape[2]
    ngrp = NSA_KV_GROUPS
    nrep = NSA_HEADS // ngrp
    dh = NSA_HEAD_DIM
    groups = range(ngrp)
    t0 = pl.program_id(1) * tq
    head = lambda x, r: x[:, r * tq:(r + 1) * tq]
    grp_lanes = lambda g: slice(g * LANES, (g + 1) * LANES)

    kn = jnp.max(kn_ref[...], axis=0)
    kmax = [jnp.sqrt(kn[j, 0:1, :]) * NORM_MARGIN for j in range(2)]

    tpos = t0 + lax.broadcasted_iota(jnp.int32, (1, tq), 1)
    q_t = []
    for g in groups:
        cols = [q_ref[(g * nrep + r) * dh:(g * nrep + r + 1) * dh, :] for r in range(nrep)]
        q_t.append(jnp.concatenate([jnp.concatenate(cols, axis=1), jnp.zeros((LANES - dh, nrep * tq), F32)], axis=0))

    q_norm = [jnp.sqrt(jnp.sum(q_t[g] * q_t[g], axis=0, keepdims=True)) for g in groups]
    shift_s = [q_norm[g] * kmax[0][:, g:g + 1] for g in groups]
    shift_w = [q_norm[g] * kmax[1][:, g:g + 1] for g in groups]
    shift_c = []
    for g in groups:
        kc = kc_ref[0, g].astype(F32)
        kc_max = jnp.sqrt(jnp.max(jnp.sum(kc * kc, axis=1, keepdims=True), axis=0, keepdims=True)) * NORM_MARGIN
        shift_c.append(q_norm[g] * kc_max)
    worst = functools.reduce(jnp.maximum, shift_s + shift_w + shift_c)
    fixed_shift_ok = jnp.max(worst) < MAX_FIXED_SHIFT

    last = (t0 + tq - 1) // tk
    k_last = last * tk + lax.broadcasted_iota(jnp.int32, (tk, tq), 0)
    causal_bias = jnp.where(k_last <= tpos, 0.0, NEG)
    span = WINDOW + tq
    start = pl.multiple_of(jnp.maximum(t0 - WINDOW, 0), tq)
    dist = tpos - (start + lax.broadcasted_iota(jnp.int32, (span, tq), 0))
    band_bias = jnp.where((dist >= 0) & (dist < WINDOW), 0.0, NEG)
    key_tile = lambda c, g: ks_ref[pl.ds(pl.multiple_of(c * tk, tk), tk), grp_lanes(g)]
    val_tile = lambda c, g: _value_chunks_t(vst_ref, g, c * (tk // LANES), tk // LANES)
    win_keys = lambda g: kw_ref[pl.ds(start, span), grp_lanes(g)]
    win_vals = lambda g: _value_chunks_t(vwt_ref, g, start // LANES, span // LANES)
    all_heads = lambda x: jnp.concatenate([x] * nrep, axis=1)

    cmp_end = lax.broadcasted_iota(jnp.int32, (ncmp, tq), 0) * CMP_STRIDE + (CMP_LEN - 1)
    vis_bias = jnp.where(cmp_end <= tpos, 0.0, NEG)
    any_vis = jnp.where(tpos >= CMP_LEN - 1, 1.0, 0.0)

    def compressed_branch(g):
        c2s = c2s_ref[...]
        s = _dot(kc_ref[0, g], q_t[g].astype(BF16))
        p_sum = jnp.zeros((ncmp, tq), F32)
        p_cmp = []
        for r in range(nrep):
            sr = head(s, r) + vis_bias
            e = jnp.exp2(sr - jnp.max(sr, axis=0, keepdims=True))
            pr = e * (any_vis / jnp.sum(e, axis=0, keepdims=True))
            p_sum = p_sum + pr
            p_cmp.append(pr.astype(BF16))
        acc_ref[2, g] = _dot(vct_ref[0, g], jnp.concatenate(p_cmp, axis=1))
        p_hi, p_lo = _split_bf16(p_sum)
        imp_ref[:, g * tq:(g + 1) * tq] = (_dot(c2s, p_hi) + _dot(c2s, p_lo))[:n_slc, :]

    @pl.when(fixed_shift_ok)
    def _():
        zeros_hi = jnp.zeros((dh - 1, nrep * tq), F32)
        for g in groups:
            qt_win = jnp.concatenate([q_t[g][:dh, :], -shift_w[g], zeros_hi], axis=0).astype(BF16)
            p = jnp.exp2(_dot(win_keys(g), qt_win) + all_heads(band_bias)).astype(BF16)
            acc_ref[1, g] = _dot(win_vals(g), p)
            s = _dot(kc_ref[0, g], q_t[g].astype(BF16))
            e = jnp.concatenate([jnp.exp2(head(s, r) + (vis_bias - head(shift_c[g], r))) for r in range(nrep)], axis=1)
            denom = jnp.sum(e, axis=0, keepdims=True)
            inv = jnp.where(denom > 0.0, all_heads(any_vis) / denom, 0.0)
            e_hi, e_lo = _split_bf16(e)
            acc_ref[2, g] = _dot(vct_ref[0, g], e_hi) * inv
            c2s = c2s_ref[...]
            imp = (_dot(c2s, e_hi) + _dot(c2s, e_lo)) * inv
            imp_ref[:, g * tq:(g + 1) * tq] = functools.reduce(
                lambda a, b: a + b, [head(imp, r) for r in range(nrep)])[:n_slc, :]

    @pl.when(jnp.logical_not(fixed_shift_ok))
    def _():
        for g in groups:
            compressed_branch(g)
            sw = _dot(win_keys(g), q_t[g].astype(BF16))
            p_win = []
            for r in range(nrep):
                sr = head(sw, r) + band_bias
                p_win.append(jnp.exp2(sr - jnp.max(sr, axis=0, keepdims=True)).astype(BF16))
            acc_ref[1, g] = _dot(win_vals(g), jnp.concatenate(p_win, axis=1))

    bias_t = _topk_bias_t(imp_ref[...], jnp.concatenate([tpos] * ngrp, axis=1),
                          (t0 + tq - 1) // SLC_LEN, val_ref, cnt_ref)
    if n_slc < dh:
        bias_t = jnp.concatenate([bias_t, jnp.full((dh - n_slc, ngrp * tq), NEG, F32)], axis=0)
    bias_rows = [jnp.concatenate([head(bias_t, g)] * nrep, axis=1) for g in groups]

    @pl.when(fixed_shift_ok)
    def _():
        qt_sel = [jnp.concatenate([q_t[g][:dh, :], bias_rows[g] - shift_s[g]], axis=0).astype(BF16) for g in groups]

        def partial_pv(c, g, token_bias=None, nkeys=tk):
            k0 = pl.multiple_of(c * tk, tk)
            sc = _dot(ks_ref[pl.ds(k0, nkeys), grp_lanes(g)], qt_sel[g])
            if token_bias is not None:
                sc = sc + all_heads(token_bias[:nkeys, :])
            vals = _value_chunks_t(vst_ref, g, c * (tk // LANES), nkeys // LANES)
            return _dot(vals, jnp.exp2(sc).astype(BF16))

        def body(i, accs):
            return tuple(accs[g] + partial_pv(2 * i, g) + partial_pv(2 * i + 1, g) for g in groups)

        accs = lax.fori_loop(0, last // 2, body, tuple(jnp.zeros((V_ROWS, nrep * tq), F32) for _ in groups))
        for g in groups:
            acc_ref[0, g] = accs[g]

        @pl.when(last % 2 == 1)
        def _():
            for g in groups:
                acc_ref[0, g] += partial_pv(last - 1, g)

        needs_half = (t0 % tk) + tq <= tk // 2

        @pl.when(needs_half)
        def _():
            for g in groups:
                acc_ref[0, g] += partial_pv(last, g, causal_bias, tk // 2)

        @pl.when(jnp.logical_not(needs_half))
        def _():
            for g in groups:
                acc_ref[0, g] += partial_pv(last, g, causal_bias)

    @pl.when(jnp.logical_not(fixed_shift_ok))
    def _():
        qt_sel = [jnp.concatenate([q_t[g][:dh, :], bias_rows[g]], axis=0).astype(BF16) for g in groups]

        def tile(c, carry, token_bias):
            out = []
            for g in groups:
                m, acc = carry[g]
                sc = _dot(key_tile(c, g), qt_sel[g])
                ps, ms, alphas = [], [], []
                for r in range(nrep):
                    sr = head(sc, r)
                    if token_bias is not None:
                        sr = sr + token_bias
                    m_new = jnp.maximum(head(m, r), jnp.max(sr, axis=0, keepdims=True))
                    alphas.append(jnp.exp2(head(m, r) - m_new))
                    ms.append(m_new)
                    ps.append(jnp.exp2(sr - m_new).astype(BF16))
                pv = _dot(val_tile(c, g), jnp.concatenate(ps, axis=1))
                out.append((jnp.concatenate(ms, axis=1), jnp.concatenate(alphas, axis=1) * acc + pv))
            return tuple(out)

        init = tuple((jnp.full((1, nrep * tq), NEG, F32), jnp.zeros((V_ROWS, nrep * tq), F32)) for _ in groups)
        carry = lax.fori_loop(0, last, lambda c, cr: tile(c, cr, None), init)
        carry = tile(last, carry, causal_bias)
        for g in groups:
            acc_ref[0, g] = carry[g][1]

    gt_t = gt_ref[...]
    heads = []
    for g in groups:
        acc_s = acc_ref[0, g]
        acc_w = acc_ref[1, g]
        o_cmp_g = acc_ref[2, g]
        o_slc = acc_s[:dh, :] * (1.0 / acc_s[dh:dh + 1, :])
        o_win = acc_w[:dh, :] * (1.0 / acc_w[dh:dh + 1, :])
        for r in range(nrep):
            row = 3 * (g * nrep + r)
            heads.append(gt_t[row:row + 1, :] * head(o_cmp_g, r)[:dh, :]
                         + gt_t[row + 1:row + 2, :] * head(o_slc, r) + gt_t[row + 2:row + 3, :] * head(o_win, r))
    o_ref[...] = jnp.transpose(jnp.concatenate(heads, axis=0)).astype(o_ref.dtype)


def _nsa_attention(q, gt, kc, vc_t, ks, vs_t, kw, vw_t, kn, batch, seq):
    g = NSA_KV_GROUPS
    tiles_per_row = kn.shape[0] // batch
    nq = seq // ATT_TQ
    qw = (NSA_HEADS // g) * NSA_HEAD_DIM
    n_slc = seq // SLC_LEN
    n_cmp = (seq - CMP_LEN) // CMP_STRIDE + 1
    n_cmp_pad = kc.shape[2]
    c_start = np.arange(n_cmp)[:, None] * CMP_STRIDE
    s_start = np.arange(n_slc)[None, :] * SLC_LEN
    overlap = np.clip(np.minimum(c_start + CMP_LEN, s_start + SLC_LEN)
                      - np.maximum(c_start, s_start), 0, None) / CMP_LEN
    assert n_slc <= NSA_HEAD_DIM and n_slc % SUBLANES == 0 and seq % ATT_TK == 0
    c2s_t = np.zeros((NSA_HEAD_DIM, n_cmp_pad))
    c2s_t[:n_slc, :n_cmp] = overlap.T
    hd = NSA_HEADS * NSA_HEAD_DIM
    tok_spec = lambda width: pl.BlockSpec((ATT_TQ, width), lambda b, qi: (b * nq + qi, 0))
    key_spec = pl.BlockSpec((seq, g * LANES), lambda b, qi: (b, 0))
    val_spec = pl.BlockSpec((1, g, seq // LANES, V_ROWS, LANES), lambda b, qi: (b, 0, 0, 0, 0))
    return pl.pallas_call(
        functools.partial(_nsa_attn_kernel, n_slc=n_slc, tk=ATT_TK),
        grid=(batch, nq),
        in_specs=[pl.BlockSpec((hd, ATT_TQ), lambda b, qi: (0, b * nq + qi)),
                  pl.BlockSpec((LANES, ATT_TQ), lambda b, qi: (0, b * nq + qi)),
                  pl.BlockSpec((1, g, n_cmp_pad, LANES), lambda b, qi: (b, 0, 0, 0)),
                  pl.BlockSpec((1, g, V_ROWS, n_cmp_pad), lambda b, qi: (b, 0, 0, 0)),
                  key_spec, val_spec, key_spec, val_spec,
                  pl.BlockSpec((tiles_per_row,) + kn.shape[1:], lambda b, qi: (b, 0, 0, 0)),
                  pl.BlockSpec(c2s_t.shape, lambda b, qi: (0, 0))],
        out_specs=tok_spec(hd),
        out_shape=jax.ShapeDtypeStruct((batch * seq, hd), BF16),
        scratch_shapes=[pltpu.VMEM((n_slc, g * ATT_TQ), F32)] * 3
        + [pltpu.VMEM((3, g, V_ROWS, (NSA_HEADS // g) * ATT_TQ), F32)],
        compiler_params=_cparams(("parallel", "arbitrary")),
        name="nsa_attention",
    )(q, gt, kc, vc_t, ks, vs_t, kw, vw_t, kn, jnp.asarray(c2s_t, BF16))


def _block_diag(w):
    nb, bi, bo = w.shape
    eye = jnp.eye(nb, dtype=w.dtype)
    return (eye[:, None, :, None] * w[:, :, None, :]).reshape(nb * bi, nb * bo)


def _nsa_weight(w_in):
    return jnp.pad(w_in, ((0, 0), (0, LANES - 3 * NSA_HEADS))).astype(BF16)


def _ab_mixer(h, g_attn, w_in, conv_w, conv_b, ga_w, ga_b, gx_w, gx_b, lam, batch, seq):
    proj = _proj_ab(h, g_attn, w_in.astype(BF16))
    gate_w = jnp.concatenate([_block_diag(ga_w), _block_diag(gx_w)], axis=1).astype(BF16)
    gate_b = jnp.concatenate([ga_b, gx_b])[None, :]
    lru_out = _lru(proj, batch, seq, conv_w, conv_b[None, :], gate_w, gate_b, lam[None, :])
    return [lru_out, _retention(proj, batch, seq)]


def _nsa_mixer(h, g_attn, w_in, pe_k, k_w1, k_w2, pe_v, v_w1, v_w2, batch, seq):
    q, cmp_in, ks, vs_t, kw, vw_t, gt, kn = _proj_nsa(h, g_attn, _nsa_weight(w_in), seq)
    dh = NSA_HEAD_DIM
    n_blk = seq // CMP_STRIDE
    pe = jnp.broadcast_to(jnp.stack([pe_k, pe_v]).reshape(2, 1, CMP_LEN * dh), (2, SUBLANES, CMP_LEN * dh))
    w1 = jnp.stack([k_w1, v_w1]).astype(BF16)
    w1r = w1.reshape(2, CMP_LEN, dh, dh)
    first, second, zero = w1r[:, :CMP_STRIDE], w1r[:, CMP_STRIDE:], jnp.zeros_like(w1r[:, :CMP_STRIDE])
    wc = jnp.concatenate([jnp.concatenate([first, zero, second, zero], axis=-1),
                          jnp.concatenate([zero, first, zero, second], axis=-1)], axis=-2)
    w2k = k_w2.astype(BF16)
    z = jnp.zeros_like(w2k)
    w2k = jnp.concatenate([jnp.concatenate([w2k, z, z, z], axis=1), jnp.concatenate([z, z, w2k, z], axis=1)], axis=0)
    w2v_t = v_w2.T.astype(BF16)
    zt = jnp.zeros_like(w2v_t)
    w2v_t = jnp.concatenate([jnp.concatenate([w2v_t, zt], axis=1), jnp.concatenate([zt, w2v_t], axis=1)], axis=0)
    cos, sgn = _rot_tables(np.arange(n_blk) * CMP_STRIDE + CMP_LEN - 1, ROT_DIM, ROPE_THETA, 1.0)
    kc, vc_t = _compress(cmp_in, batch, seq, pe, w1, wc, w2k, w2v_t, cos, sgn)
    return [_nsa_attention(q, gt, kc, vc_t, ks, vs_t, kw, vw_t, kn, batch, seq)]


def kernel(x, attn_norm, ab_w_in, conv_w, conv_b, gate_a_w, gate_a_b, gate_x_w, gate_x_b, lru_lambda,
           ab_w_out, nsa_w_in, cmp_pe_k, cmp_k_w1, cmp_k_w2, cmp_pe_v, cmp_v_w1, cmp_v_w2, nsa_w_out,
           ffn_norm, ffn_w1, ffn_w3, ffn_w2, final_norm):
    batch, seq, d = x.shape
    depth = attn_norm.shape[0]
    h = x.reshape(batch * seq, d)
    for layer in range(depth):
        i = layer // 2
        g_attn = attn_norm[layer][None, :]
        if layer % 2 == 0:
            mixes = _ab_mixer(h, g_attn, ab_w_in[i], conv_w[i], conv_b[i], gate_a_w[i], gate_a_b[i],
                              gate_x_w[i], gate_x_b[i], lru_lambda[i], batch, seq)
            w_out = ab_w_out[i]
        else:
            mixes = _nsa_mixer(h, g_attn, nsa_w_in[i], cmp_pe_k[i], cmp_k_w1[i], cmp_k_w2[i],
                               cmp_pe_v[i], cmp_v_w1[i], cmp_v_w2[i], batch, seq)
            w_out = nsa_w_out[i]
        last = layer == depth - 1
        h = _mix_ffn(h, mixes, w_out.astype(BF16), ffn_norm[layer][None, :],
                     ffn_w1[layer].astype(BF16), ffn_w3[layer].astype(BF16), ffn_w2[layer].astype(BF16),
                     final_g=final_norm[None, :] if last else None)
    return h.reshape(batch, seq, d)
```

```python
import functools

import numpy as np
import jax
import jax.numpy as jnp
from jax import lax
from jax.experimental import pallas as pl
from jax.experimental.pallas import tpu as pltpu

F32 = jnp.float32
BF16 = jnp.bfloat16

EPS = 1e-6
NEG = -1e30
BIG = 1e30
LOG2E = float(np.log2(np.e))

LRU_BLOCKS = 8
CONV_WIDTH = 4
LRU_C = 8.0
RET_HEADS = 8
RET_DIM = 64
RET_THETA = 10000.0
NSA_HEADS = 16
NSA_KV_GROUPS = 4
NSA_HEAD_DIM = 64
CMP_LEN = 32
CMP_STRIDE = 16
SLC_LEN = 64
SLC_TOPK = 16
WINDOW = 512
ROPE_THETA = 500000.0
ROT_DIM = NSA_HEAD_DIM // 4

LANES = 128
SUBLANES = 8
VMEM_LIMIT = 56 * 1024 * 1024

RET_CHUNK = 256
TOK_TILE = 512
PROJ_AB_TILE = 1024
LRU_TILE = 512
ATT_TQ = 256
ATT_TK = 1024
V_ROWS = NSA_HEAD_DIM + 16
MAX_FIXED_SHIFT = 32.0
NORM_MARGIN = 1.01


def _cparams(sem):
    return pltpu.CompilerParams(dimension_semantics=sem, vmem_limit_bytes=VMEM_LIMIT)


def _dot(a, b):
    return jnp.dot(a, b, preferred_element_type=F32)


def _dot_nt(a, b):
    return lax.dot_general(a, b, (((1,), (1,)), ((), ())), preferred_element_type=F32)


def _rms(x, g):
    return x * lax.rsqrt(jnp.mean(x * x, axis=-1, keepdims=True) + EPS) * g


def _tile_lanes(t, reps):
    return t if reps == 1 else jnp.concatenate([t] * reps, axis=1)


def _rotate(x, cos, sgn, half):
    n = x.shape[1]
    lane = lax.broadcasted_iota(jnp.int32, x.shape, 1)
    first = (lane % NSA_HEAD_DIM) < half
    partner = jnp.where(first, pltpu.roll(x, n - half, axis=1), pltpu.roll(x, half, axis=1))
    return x * cos + partner * sgn


def _group_slabs(x, fill):
    lane = lax.broadcasted_iota(jnp.int32, (x.shape[0], LANES), 1)
    lo = lane < NSA_HEAD_DIM
    out = []
    for c in range(x.shape[1] // LANES):
        xc = x[:, c * LANES:(c + 1) * LANES]
        out.append(jnp.where(lo, xc, fill))
        out.append(jnp.where(lo, pltpu.roll(xc, NSA_HEAD_DIM, axis=1), fill))
    return jnp.concatenate(out, axis=1)


def _values_t(v_t, ncols):
    return jnp.concatenate([v_t, jnp.ones((V_ROWS - NSA_HEAD_DIM, ncols), F32)], axis=0).astype(BF16)


def _store_values_t(v, o_ref):
    rows = v.shape[0]
    v_t = jnp.transpose(v)
    for g in range(v.shape[1] // NSA_HEAD_DIM):
        slab = _values_t(v_t[g * NSA_HEAD_DIM:(g + 1) * NSA_HEAD_DIM, :], rows)
        for c in range(rows // LANES):
            o_ref[0, g, c] = slab[:, c * LANES:(c + 1) * LANES]


def _rot_tables(pos, rot_dim, theta, scale):
    half = rot_dim // 2
    inv = theta ** (-np.arange(0, rot_dim, 2, dtype=np.float64) / rot_dim)
    ang = np.asarray(pos, np.float64)[:, None] * inv
    cos = np.ones((len(pos), NSA_HEAD_DIM))
    sgn = np.zeros((len(pos), NSA_HEAD_DIM))
    cos[:, :half] = np.cos(ang)
    cos[:, half:rot_dim] = np.cos(ang)
    sgn[:, :half] = -np.sin(ang)
    sgn[:, half:rot_dim] = np.sin(ang)
    cos = np.tile(cos * scale, (1, 2))
    sgn = np.tile(sgn * scale, (1, 2))
    return jnp.asarray(cos, F32), jnp.asarray(sgn, F32)


def _ret_decay_tables(chunk):
    log_g = np.log1p(-(2.0 ** (-5.0 - np.arange(RET_HEADS, dtype=np.float64))))
    ci = np.arange(chunk, dtype=np.float64)
    diff = ci[:, None] - ci[None, :]
    inner = np.where(diff >= 0, np.exp(np.maximum(diff, 0.0) * log_g[:, None, None]), 0.0)
    lane_head = np.repeat(np.arange(RET_HEADS), RET_DIM)
    qdec = np.exp((ci[:, None] + 1.0) * log_g[lane_head][None, :])
    kdec = np.exp((chunk - 1.0 - ci[:, None]) * log_g[lane_head][None, :])
    cdec = np.exp(chunk * log_g[lane_head])[None, :]
    return (jnp.asarray(inner, F32), jnp.asarray(qdec, F32), jnp.asarray(kdec, F32),
            jnp.asarray(cdec, F32))


def _proj_ab_kernel(x_ref, g_ref, w_ref, o_ref):
    xn = _rms(x_ref[...], g_ref[...]).astype(BF16)
    o_ref[...] = _dot(xn, w_ref[...])


def _proj_ab(x2, g, w):
    t, d = x2.shape
    n = w.shape[1]
    return pl.pallas_call(
        _proj_ab_kernel,
        grid=(t // PROJ_AB_TILE,),
        in_specs=[pl.BlockSpec((PROJ_AB_TILE, d), lambda i: (i, 0)),
                  pl.BlockSpec((1, d), lambda i: (0, 0)),
                  pl.BlockSpec((d, n), lambda i: (0, 0), pipeline_mode=pl.Buffered(1))],
        out_specs=pl.BlockSpec((PROJ_AB_TILE, n), lambda i: (i, 0)),
        out_shape=jax.ShapeDtypeStruct((t, n), F32),
        compiler_params=_cparams(("parallel",)),
        name="proj_ab",
    )(x2, g, w)


def _lru_kernel(y_ref, x_ref, cw_ref, cb_ref, gw_ref, gb_ref, lam_ref, o_ref,
                xp_ref, a_ref, u_ref, h_ref, carry_ref):
    ts, w = x_ref.shape
    row = lax.broadcasted_iota(jnp.int32, (SUBLANES, w), 0)

    @pl.when(pl.program_id(1) == 0)
    def _():
        xp_ref[...] = jnp.zeros_like(xp_ref)
        carry_ref[...] = jnp.zeros_like(carry_ref)

    x = x_ref[...]
    prev = xp_ref[...]
    xc = cb_ref[...] + x * cw_ref[CONV_WIDTH - 1:CONV_WIDTH, :]
    for s in range(1, CONV_WIDTH):
        rolled = pltpu.roll(x, s, axis=0)
        head_rows = jnp.where(row < s, pltpu.roll(prev, s, axis=0), rolled[:SUBLANES, :])
        shifted = jnp.concatenate([head_rows, rolled[SUBLANES:, :]], axis=0)
        xc = xc + shifted * cw_ref[CONV_WIDTH - 1 - s:CONV_WIDTH - s, :]
    xp_ref[...] = x[ts - SUBLANES:, :]

    gates = _dot(xc.astype(BF16), gw_ref[...]) + gb_ref[...]
    r = jax.nn.sigmoid(gates[:, :w])
    i = jax.nn.sigmoid(gates[:, w:])
    log_a = -LRU_C * r * jax.nn.softplus(-lam_ref[...])
    a = jnp.exp(log_a)
    z = jnp.tanh(-log_a) * (a * a + 1.0)
    a_ref[...] = a
    u_ref[...] = jnp.where(z > 0.0, z * lax.rsqrt(z), 0.0) * (i * xc)

    def body(gi, carry):
        r0 = pl.multiple_of(gi * SUBLANES, SUBLANES)
        a = a_ref[pl.ds(r0, SUBLANES), :]
        u = u_ref[pl.ds(r0, SUBLANES), :]
        for d in (1, 2, 4):
            a_s = pltpu.roll(a, d, axis=0)
            u_s = pltpu.roll(u, d, axis=0)
            m = row >= d
            u = jnp.where(m, a * u_s + u, u)
            a = jnp.where(m, a * a_s, a)
        h = a * carry + u
        h_ref[pl.ds(r0, SUBLANES), :] = h
        return h[SUBLANES - 1:SUBLANES, :]

    carry_ref[...] = lax.fori_loop(0, ts // SUBLANES, body, carry_ref[...])
    o_ref[...] = (h_ref[...] * jax.nn.gelu(y_ref[...])).astype(o_ref.dtype)


def _lru(proj, batch, seq, conv_w, conv_b, gate_w, gate_b, lam):
    w = conv_w.shape[1]
    ns = seq // LRU_TILE
    return pl.pallas_call(
        _lru_kernel,
        grid=(batch, ns),
        in_specs=[pl.BlockSpec((LRU_TILE, w), lambda b, s: (b * ns + s, 0)),
                  pl.BlockSpec((LRU_TILE, w), lambda b, s: (b * ns + s, 1)),
                  pl.BlockSpec((CONV_WIDTH, w), lambda b, s: (0, 0)),
                  pl.BlockSpec((1, w), lambda b, s: (0, 0)),
                  pl.BlockSpec((w, 2 * w), lambda b, s: (0, 0)),
                  pl.BlockSpec((1, 2 * w), lambda b, s: (0, 0)),
                  pl.BlockSpec((1, w), lambda b, s: (0, 0))],
        out_specs=pl.BlockSpec((LRU_TILE, w), lambda b, s: (b * ns + s, 0)),
        out_shape=jax.ShapeDtypeStruct((batch * seq, w), BF16),
        scratch_shapes=[pltpu.VMEM((SUBLANES, w), F32),
                        pltpu.VMEM((LRU_TILE, w), F32),
                        pltpu.VMEM((LRU_TILE, w), F32),
                        pltpu.VMEM((LRU_TILE, w), F32),
                        pltpu.VMEM((1, w), F32)],
        compiler_params=_cparams(("parallel", "arbitrary")),
        name="rg_lru",
    )(proj, proj, conv_w, conv_b, gate_w, gate_b, lam)


def _split_bf16(x):
    hi = x.astype(BF16)
    lo = (x - hi.astype(F32)).astype(BF16)
    return hi, lo


def _ret_kernel(q_ref, k_ref, v_ref, g_ref, cq_ref, sq_ref, ck_ref, sk_ref,
                inner_ref, qdec_ref, kdec_ref, cdec_ref, avg_ref, o_ref, st_ref):
    c, w = q_ref.shape
    npair = w // LANES
    b = pl.program_id(1)

    @pl.when(pl.program_id(0) == 0)
    def _():
        st_ref[b] = jnp.zeros(st_ref.shape[1:], F32)

    half = RET_DIM // 2
    q = _rotate(q_ref[...], _tile_lanes(cq_ref[...], npair), _tile_lanes(sq_ref[...], npair), half)
    k = _rotate(k_ref[...], _tile_lanes(ck_ref[...], npair), _tile_lanes(sk_ref[...], npair), half)
    v = v_ref[...]
    qd = (q * qdec_ref[...]).astype(BF16)
    kd = k * kdec_ref[...]
    q = q.astype(BF16)
    kb = k.astype(BF16)
    vb = v.astype(BF16)

    lane = lax.broadcasted_iota(jnp.int32, (c, LANES), 1)
    lo = lane < RET_DIM
    rr = lax.broadcasted_iota(jnp.int32, (LANES, LANES), 0) < RET_DIM
    cc = lax.broadcasted_iota(jnp.int32, (LANES, LANES), 1) < RET_DIM
    same_head = rr == cc
    zero = jnp.zeros((c, LANES), BF16)

    outs = []
    for p in range(npair):
        sl = slice(p * LANES, (p + 1) * LANES)
        qp, kp, vp = q[:, sl], kb[:, sl], vb[:, sl]
        s_lo = _dot_nt(jnp.where(lo, qp, zero), kp) * inner_ref[2 * p]
        s_hi = _dot_nt(jnp.where(lo, zero, qp), kp) * inner_ref[2 * p + 1]
        o = jnp.where(lo, _dot(s_lo.astype(BF16), vp), _dot(s_hi.astype(BF16), vp))
        st = st_ref[b, p]
        o = o + _dot(qd[:, sl], st.astype(BF16))
        kv = _dot(jnp.transpose(kd[:, sl]).astype(BF16), vp)
        st_ref[b, p] = st * cdec_ref[:, sl] + jnp.where(same_head, kv, 0.0)
        outs.append(o)

    o_all = jnp.concatenate(outs, axis=0)
    avg = avg_ref[...]
    mu = _dot(o_all.astype(BF16), avg)
    dlt = o_all - mu
    var = _dot((dlt * dlt).astype(BF16), avg)
    y = dlt * lax.rsqrt(var + EPS)
    y = jnp.concatenate([y[p * c:(p + 1) * c, :] for p in range(npair)], axis=1)
    o_ref[...] = (y * jax.nn.silu(g_ref[...])).astype(o_ref.dtype)


def _retention(proj, batch, seq):
    w = RET_HEADS * RET_DIM
    c = RET_CHUNK
    nc = seq // c
    npair = w // LANES
    pos = np.arange(seq)
    cq, sq = _rot_tables(pos, RET_DIM, RET_THETA, 1.0)
    ck, sk = _rot_tables(pos, RET_DIM, RET_THETA, RET_DIM ** -0.5)
    inner, qdec, kdec, cdec = _ret_decay_tables(c)
    blk = np.arange(LANES) // RET_DIM
    avg = jnp.asarray((blk[:, None] == blk[None, :]) / RET_DIM, BF16)
    col0 = 2
    tok = lambda j: pl.BlockSpec((c, w), lambda ci, b, j=j: (b * nc + ci, j))
    tab = pl.BlockSpec((c, LANES), lambda ci, b: (ci, 0))
    const = lambda shape: pl.BlockSpec(shape, lambda ci, b: (0,) * len(shape))
    return pl.pallas_call(
        _ret_kernel,
        grid=(nc, batch),
        in_specs=[tok(col0), tok(col0 + 1), tok(col0 + 2), tok(col0 + 3),
                  tab, tab, tab, tab,
                  const((RET_HEADS, c, c)), const((c, w)), const((c, w)), const((1, w)),
                  const((LANES, LANES))],
        out_specs=pl.BlockSpec((c, w), lambda ci, b: (b * nc + ci, 0)),
        out_shape=jax.ShapeDtypeStruct((batch * seq, w), BF16),
        scratch_shapes=[pltpu.VMEM((batch, npair, LANES, LANES), F32)],
        compiler_params=_cparams(("arbitrary", "arbitrary")),
        name="retention",
    )(proj, proj, proj, proj, cq, sq, ck, sk, inner, qdec, kdec, cdec, avg)


def _mix_ffn_kernel(*refs, n_mix, final):
    h_ref = refs[0]
    mix_refs = refs[1:1 + n_mix]
    wo_ref, g_ref, w1_ref, w3_ref, w2_ref = refs[1 + n_mix:6 + n_mix]
    rest = refs[6 + n_mix:]
    fg_ref = rest[0] if final else None
    o_ref = rest[-1]

    mixed = mix_refs[0][...] if n_mix == 1 else jnp.concatenate([r[...] for r in mix_refs], axis=1)
    h1 = h_ref[...] + _dot(mixed, wo_ref[...])
    xn = _rms(h1, g_ref[...]).astype(BF16)
    act = (jax.nn.silu(_dot(xn, w1_ref[...])) * _dot(xn, w3_ref[...])).astype(BF16)
    h2 = h1 + _dot(act, w2_ref[...])
    if final:
        h2 = _rms(h2, fg_ref[...])
    o_ref[...] = h2


def _mix_ffn(h2d, mixes, wo, g, w1, w3, w2, final_g=None):
    t, d = h2d.shape
    hid = w1.shape[1]
    final = final_g is not None
    row = lambda width: pl.BlockSpec((TOK_TILE, width), lambda i: (i, 0))
    const = lambda shape: pl.BlockSpec(shape, lambda i: (0, 0), pipeline_mode=pl.Buffered(1))
    in_specs = [row(d)] + [row(m.shape[1]) for m in mixes] + [
        const(wo.shape), const((1, d)), const((d, hid)), const((d, hid)), const((hid, d))]
    args = [h2d, *mixes, wo, g, w1, w3, w2]
    if final:
        in_specs.append(const((1, d)))
        args.append(final_g)
    return pl.pallas_call(
        functools.partial(_mix_ffn_kernel, n_mix=len(mixes), final=final),
        grid=(t // TOK_TILE,),
        in_specs=in_specs,
        out_specs=row(d),
        out_shape=jax.ShapeDtypeStruct((t, d), F32),
        compiler_params=_cparams(("parallel",)),
        name="mix_ffn_final" if final else "mix_ffn",
    )(*args)


def _proj_nsa_kernel(x_ref, g_ref, w_ref, cq_ref, sq_ref, ck_ref, sk_ref,
                     q_ref, cmp_ref, ks_ref, vs_ref, kw_ref, vw_ref, gt_ref, kn_ref, *, seq_tiles):
    hd = q_ref.shape[0]
    kvw = NSA_KV_GROUPS * NSA_HEAD_DIM
    xn = _rms(x_ref[...], g_ref[...]).astype(BF16)
    proj = _dot(xn, w_ref[...])
    half = ROT_DIM // 2
    q = proj[:, :hd]
    q_ref[...] = jnp.transpose(_rotate(q, _tile_lanes(cq_ref[...], hd // LANES),
                                       _tile_lanes(sq_ref[...], hd // LANES), half))
    kv = proj[:, hd:hd + 6 * kvw]
    for c in range(cmp_ref.shape[0]):
        cmp_ref[c] = kv[:, c * LANES:(c + 1) * LANES]
    ck = _tile_lanes(ck_ref[...], kvw // LANES)
    sk = _tile_lanes(sk_ref[...], kvw // LANES)
    rows = x_ref.shape[0]
    tpos = (pl.program_id(0) % seq_tiles) * rows + lax.broadcasted_iota(jnp.int32, (rows, LANES), 0)
    lane = lax.broadcasted_iota(jnp.int32, (rows, LANES), 1)
    block_onehot = jnp.where(lane - NSA_HEAD_DIM == tpos // SLC_LEN, 1.0, 0.0)
    ks = _group_slabs(_rotate(kv[:, 2 * kvw:3 * kvw], ck, sk, half), block_onehot).astype(ks_ref.dtype)
    one_lane = jnp.where(lane == NSA_HEAD_DIM, 1.0, 0.0)
    kw = _group_slabs(_rotate(kv[:, 4 * kvw:5 * kvw], ck, sk, half), one_lane).astype(kw_ref.dtype)
    ks_ref[...] = ks
    kw_ref[...] = kw
    kn_ref[0, 0] = _max_key_norm2(ks)
    kn_ref[0, 1] = _max_key_norm2(kw)
    _store_values_t(kv[:, 3 * kvw:4 * kvw], vs_ref)
    _store_values_t(kv[:, 5 * kvw:6 * kvw], vw_ref)
    gt_ref[...] = jnp.transpose(jax.nn.sigmoid(proj[:, hd + 6 * kvw:]))


def _proj_nsa(x2, g, w, seq):
    t, d = x2.shape
    hd = NSA_HEADS * NSA_HEAD_DIM
    kvw = NSA_KV_GROUPS * NSA_HEAD_DIM
    gw = NSA_KV_GROUPS * LANES
    n = w.shape[1]
    ns = seq // TOK_TILE
    pos = np.arange(seq)
    cq, sq = _rot_tables(pos, ROT_DIM, ROPE_THETA, NSA_HEAD_DIM ** -0.5 * LOG2E)
    ck, sk = _rot_tables(pos, ROT_DIM, ROPE_THETA, 1.0)
    row = lambda width: pl.BlockSpec((TOK_TILE, width), lambda i: (i, 0))
    tab = pl.BlockSpec((TOK_TILE, LANES), lambda i: (i % ns, 0))
    chunks = TOK_TILE // LANES
    val_t = (pl.BlockSpec((1, NSA_KV_GROUPS, chunks, V_ROWS, LANES), lambda i: (i // ns, 0, i % ns, 0, 0)),
             jax.ShapeDtypeStruct((t // seq, NSA_KV_GROUPS, seq // LANES, V_ROWS, LANES), BF16))
    tok = lambda wd, dt: (row(wd), jax.ShapeDtypeStruct((t, wd), dt))
    cmp_planes = (pl.BlockSpec((2 * kvw // LANES, TOK_TILE, LANES), lambda i: (0, i, 0)),
                  jax.ShapeDtypeStruct((2 * kvw // LANES, t, LANES), F32))
    tok_t = lambda wd: (pl.BlockSpec((wd, TOK_TILE), lambda i: (0, i)), jax.ShapeDtypeStruct((wd, t), F32))
    key_norms = (pl.BlockSpec((1, 2, SUBLANES, LANES), lambda i: (i, 0, 0, 0)),
                 jax.ShapeDtypeStruct((t // TOK_TILE, 2, SUBLANES, LANES), F32))
    outs = [tok_t(hd), cmp_planes, tok(gw, BF16), val_t, tok(gw, BF16), val_t, tok_t(LANES), key_norms]
    return pl.pallas_call(
        functools.partial(_proj_nsa_kernel, seq_tiles=ns),
        grid=(t // TOK_TILE,),
        in_specs=[row(d), pl.BlockSpec((1, d), lambda i: (0, 0)),
                  pl.BlockSpec((d, n), lambda i: (0, 0), pipeline_mode=pl.Buffered(1)),
                  tab, tab, tab, tab],
        out_specs=[spec for spec, _ in outs],
        out_shape=[shape for _, shape in outs],
        compiler_params=_cparams(("parallel",)),
        name="proj_nsa",
    )(x2, g, w, cq, sq, ck, sk)


def _compress_kernel(x_ref, pe_ref, w1_ref, wc_ref, w2k_ref, w2vt_ref, cos_ref, sgn_ref, kc_ref, vct_ref):
    n = kc_ref.shape[2]
    dh = NSA_HEAD_DIM
    for kv in range(2):
        pe_term = _dot(pe_ref[kv].astype(BF16), w1_ref[kv])[0:1, :]
        pe_term = jnp.concatenate([pe_term, pe_term], axis=1)
        for cp in range(NSA_KV_GROUPS // 2):
            col = kv * (NSA_KV_GROUPS // 2) + cp
            acc = jnp.zeros((n, 2 * LANES), F32)
            for j in range(CMP_STRIDE):
                rows = x_ref[col, pl.ds(j, n, stride=CMP_STRIDE), :]
                acc = acc + _dot(rows.astype(BF16), wc_ref[kv, j])
            hid = jax.nn.gelu(acc[:, :LANES] + pltpu.roll(acc[:, LANES:], n - 1, axis=0) + pe_term).astype(BF16)
            if kv == 0:
                kc = _dot(hid, w2k_ref[...])
                kc = _rotate(kc, _tile_lanes(cos_ref[...], 2), _tile_lanes(sgn_ref[...], 2), ROT_DIM // 2)
                kc_ref[0, 2 * cp] = kc[:, :LANES].astype(kc_ref.dtype)
                kc_ref[0, 2 * cp + 1] = kc[:, LANES:].astype(kc_ref.dtype)
            else:
                vc_t = _dot_nt(w2vt_ref[...], hid)
                vct_ref[0, 2 * cp] = _values_t(vc_t[:dh, :], n)
                vct_ref[0, 2 * cp + 1] = _values_t(vc_t[dh:, :], n)


def _compress(cmp_in, batch, seq, pe, w1, wc, w2k, w2v_t, cos, sgn):
    g = NSA_KV_GROUPS
    n = seq // CMP_STRIDE
    full = lambda a: pl.BlockSpec(a.shape, lambda bi: (0,) * a.ndim)
    return pl.pallas_call(
        _compress_kernel,
        grid=(batch,),
        in_specs=[pl.BlockSpec((cmp_in.shape[0], seq, LANES), lambda bi: (0, bi, 0)),
                  full(pe), full(w1), full(wc), full(w2k), full(w2v_t), full(cos), full(sgn)],
        out_specs=[pl.BlockSpec((1, g, n, LANES), lambda bi: (bi, 0, 0, 0)),
                   pl.BlockSpec((1, g, V_ROWS, n), lambda bi: (bi, 0, 0, 0))],
        out_shape=[jax.ShapeDtypeStruct((batch, g, n, LANES), BF16),
                   jax.ShapeDtypeStruct((batch, g, V_ROWS, n), BF16)],
        compiler_params=_cparams(("parallel",)),
        name="compress",
    )(cmp_in, pe, w1, wc, w2k, w2v_t, cos, sgn)


def _topk_bias_t(imp_t, tpos, last_block, val_ref, cnt_ref):
    nb, tq = imp_t.shape
    j = lax.broadcasted_iota(jnp.int32, (nb, tq), 0)
    cur = tpos // SLC_LEN
    forced = (j == 0) | (j == cur) | (j == cur - 1)
    causal = j <= cur
    val_ref[...] = jnp.where(forced, BIG, jnp.where(causal, imp_t, NEG))
    cnt_ref[...] = jnp.zeros((nb, tq), F32)
    nchunk = nb // SUBLANES
    rows = lax.broadcasted_iota(jnp.int32, (SUBLANES, tq), 0)
    for c in range(nchunk):
        for ic in range(nchunk):
            @pl.when(max(c, ic) * SUBLANES <= last_block)
            def _():
                vc = val_ref[c * SUBLANES:(c + 1) * SUBLANES, :]
                vic = val_ref[ic * SUBLANES:(ic + 1) * SUBLANES, :]
                acc = cnt_ref[c * SUBLANES:(c + 1) * SUBLANES, :]
                for r in range(SUBLANES):
                    if c == ic and r == 0:
                        continue
                    vi = vic if r == 0 else pltpu.roll(vic, SUBLANES - r, axis=0)
                    if c > ic:
                        ahead = jnp.where(vi >= vc, 1.0, 0.0)
                    elif c < ic:
                        ahead = jnp.where(vi > vc, 1.0, 0.0)
                    else:
                        ahead = jnp.where(rows >= SUBLANES - r, jnp.where(vi >= vc, 1.0, 0.0),
                                          jnp.where(vi > vc, 1.0, 0.0))
                    acc = acc + ahead
                cnt_ref[c * SUBLANES:(c + 1) * SUBLANES, :] = acc
    return jnp.where((cnt_ref[...] < float(SLC_TOPK)) & causal, 0.0, NEG)


def _value_chunks_t(vt_ref, g, first, count):
    return jnp.concatenate([vt_ref[0, g, first + i] for i in range(count)], axis=1)


def _max_key_norm2(k_slabs):
    width = k_slabs.shape[1]
    row = lax.broadcasted_iota(jnp.int32, (width, LANES), 0)
    lane = lax.broadcasted_iota(jnp.int32, (width, LANES), 1)
    dims_of_group = jnp.where((row // LANES == lane) & (row % LANES < NSA_HEAD_DIM), 1.0, 0.0).astype(BF16)
    k = k_slabs.astype(F32)
    norm2 = _dot((k * k).astype(BF16), dims_of_group)
    return jnp.broadcast_to(jnp.max(norm2, axis=0, keepdims=True), (SUBLANES, LANES))


def _nsa_attn_kernel(q_ref, gt_ref, kc_ref, vct_ref, ks_ref, vst_ref, kw_ref, vwt_ref, kn_ref, c2s_ref, o_ref,
                     val_ref, cnt_ref, imp_ref, acc_ref, *, n_slc, tk):
    tq = q_ref.shape[1]
    ncmp = kc_ref.shape[2]
    ngrp = NSA_KV_GROUPS
    nrep = NSA_HEADS // ngrp
    dh = NSA_HEAD_DIM
    groups = range(ngrp)
    t0 = pl.program_id(1) * tq
    head = lambda x, r: x[:, r * tq:(r + 1) * tq]
    grp_lanes = lambda g: slice(g * LANES, (g + 1) * LANES)

    kn = jnp.max(kn_ref[...], axis=0)
    kmax = [jnp.sqrt(kn[j, 0:1, :]) * NORM_MARGIN for j in range(2)]

    tpos = t0 + lax.broadcasted_iota(jnp.int32, (1, tq), 1)
    q_t = []
    for g in groups:
        cols = [q_ref[(g * nrep + r) * dh:(g * nrep + r + 1) * dh, :] for r in range(nrep)]
        q_t.append(jnp.concatenate([jnp.concatenate(cols, axis=1), jnp.zeros((LANES - dh, nrep * tq), F32)], axis=0))

    q_norm = [jnp.sqrt(jnp.sum(q_t[g] * q_t[g], axis=0, keepdims=True)) for g in groups]
    shift_s = [q_norm[g] * kmax[0][:, g:g + 1] for g in groups]
    shift_w = [q_norm[g] * kmax[1][:, g:g + 1] for g in groups]
    worst = functools.reduce(jnp.maximum, shift_s + shift_w)
    fixed_shift_ok = jnp.max(worst) < MAX_FIXED_SHIFT

    last = (t0 + tq - 1) // tk
    k_last = last * tk + lax.broadcasted_iota(jnp.int32, (tk, tq), 0)
    causal_bias = jnp.where(k_last <= tpos, 0.0, NEG)
    span = WINDOW + tq
    start = pl.multiple_of(jnp.maximum(t0 - WINDOW, 0), tq)
    dist = tpos - (start + lax.broadcasted_iota(jnp.int32, (span, tq), 0))
    band_bias = jnp.where((dist >= 0) & (dist < WINDOW), 0.0, NEG)
    key_tile = lambda c, g: ks_ref[pl.ds(pl.multiple_of(c * tk, tk), tk), grp_lanes(g)]
    val_tile = lambda c, g: _value_chunks_t(vst_ref, g, c * (tk // LANES), tk // LANES)
    win_keys = lambda g: kw_ref[pl.ds(start, span), grp_lanes(g)]
    win_vals = lambda g: _value_chunks_t(vwt_ref, g, start // LANES, span // LANES)
    all_heads = lambda x: jnp.concatenate([x] * nrep, axis=1)

    cmp_end = lax.broadcasted_iota(jnp.int32, (ncmp, tq), 0) * CMP_STRIDE + (CMP_LEN - 1)
    vis_bias = jnp.where(cmp_end <= tpos, 0.0, NEG)
    any_vis = jnp.where(tpos >= CMP_LEN - 1, 1.0, 0.0)

    def compressed_branch(g):
        c2s = c2s_ref[...]
        s = _dot(kc_ref[0, g], q_t[g].astype(BF16))
        p_sum = jnp.zeros((ncmp, tq), F32)
        p_cmp = []
        for r in range(nrep):
            sr = head(s, r) + vis_bias
            e = jnp.exp2(sr - jnp.max(sr, axis=0, keepdims=True))
            pr = e * (any_vis / jnp.sum(e, axis=0, keepdims=True))
            p_sum = p_sum + pr
            p_cmp.append(pr.astype(BF16))
        acc_ref[2, g] = _dot(vct_ref[0, g], jnp.concatenate(p_cmp, axis=1))
        p_hi, p_lo = _split_bf16(p_sum)
        imp_ref[:, g * tq:(g + 1) * tq] = (_dot(c2s, p_hi) + _dot(c2s, p_lo))[:n_slc, :]

    @pl.when(fixed_shift_ok)
    def _():
        zeros_hi = jnp.zeros((dh - 1, nrep * tq), F32)
        for g in groups:
            qt_win = jnp.concatenate([q_t[g][:dh, :], -shift_w[g], zeros_hi], axis=0).astype(BF16)
            p = jnp.exp2(_dot(win_keys(g), qt_win) + all_heads(band_bias)).astype(BF16)
            acc_ref[1, g] = _dot(win_vals(g), p)
            compressed_branch(g)

    @pl.when(jnp.logical_not(fixed_shift_ok))
    def _():
        for g in groups:
            compressed_branch(g)
            sw = _dot(win_keys(g), q_t[g].astype(BF16))
            p_win = []
            for r in range(nrep):
                sr = head(sw, r) + band_bias
                p_win.append(jnp.exp2(sr - jnp.max(sr, axis=0, keepdims=True)).astype(BF16))
            acc_ref[1, g] = _dot(win_vals(g), jnp.concatenate(p_win, axis=1))

    bias_t = _topk_bias_t(imp_ref[...], jnp.concatenate([tpos] * ngrp, axis=1),
                          (t0 + tq - 1) // SLC_LEN, val_ref, cnt_ref)
    if n_slc < dh:
        bias_t = jnp.concatenate([bias_t, jnp.full((dh - n_slc, ngrp * tq), NEG, F32)], axis=0)
    bias_rows = [jnp.concatenate([head(bias_t, g)] * nrep, axis=1) for g in groups]

    @pl.when(fixed_shift_ok)
    def _():
        qt_sel = [jnp.concatenate([q_t[g][:dh, :], bias_rows[g] - shift_s[g]], axis=0).astype(BF16) for g in groups]

        def partial_pv(c, g, token_bias=None, nkeys=tk):
            k0 = pl.multiple_of(c * tk, tk)
            sc = _dot(ks_ref[pl.ds(k0, nkeys), grp_lanes(g)], qt_sel[g])
            if token_bias is not None:
                sc = sc + all_heads(token_bias[:nkeys, :])
            vals = _value_chunks_t(vst_ref, g, c * (tk // LANES), nkeys // LANES)
            return _dot(vals, jnp.exp2(sc).astype(BF16))

        def body(i, accs):
            return tuple(accs[g] + partial_pv(2 * i, g) + partial_pv(2 * i + 1, g) for g in groups)

        accs = lax.fori_loop(0, last // 2, body, tuple(jnp.zeros((V_ROWS, nrep * tq), F32) for _ in groups))
        for g in groups:
            acc_ref[0, g] = accs[g]

        @pl.when(last % 2 == 1)
        def _():
            for g in groups:
                acc_ref[0, g] += partial_pv(last - 1, g)

        needs_half = (t0 % tk) + tq <= tk // 2

        @pl.when(needs_half)
        def _():
            for g in groups:
                acc_ref[0, g] += partial_pv(last, g, causal_bias, tk // 2)

        @pl.when(jnp.logical_not(needs_half))
        def _():
            for g in groups:
                acc_ref[0, g] += partial_pv(last, g, causal_bias)

    @pl.when(jnp.logical_not(fixed_shift_ok))
    def _():
        qt_sel = [jnp.concatenate([q_t[g][:dh, :], bias_rows[g]], axis=0).astype(BF16) for g in groups]

        def tile(c, carry, token_bias):
            out = []
            for g in groups:
                m, acc = carry[g]
                sc = _dot(key_tile(c, g), qt_sel[g])
                ps, ms, alphas = [], [], []
                for r in range(nrep):
                    sr = head(sc, r)
                    if token_bias is not None:
                        sr = sr + token_bias
                    m_new = jnp.maximum(head(m, r), jnp.max(sr, axis=0, keepdims=True))
                    alphas.append(jnp.exp2(head(m, r) - m_new))
                    ms.append(m_new)
                    ps.append(jnp.exp2(sr - m_new).astype(BF16))
                pv = _dot(val_tile(c, g), jnp.concatenate(ps, axis=1))
                out.append((jnp.concatenate(ms, axis=1), jnp.concatenate(alphas, axis=1) * acc + pv))
            return tuple(out)

        init = tuple((jnp.full((1, nrep * tq), NEG, F32), jnp.zeros((V_ROWS, nrep * tq), F32)) for _ in groups)
        carry = lax.fori_loop(0, last, lambda c, cr: tile(c, cr, None), init)
        carry = tile(last, carry, causal_bias)
        for g in groups:
            acc_ref[0, g] = carry[g][1]

    gt_t = gt_ref[...]
    heads = []
    for g in groups:
        acc_s = acc_ref[0, g]
        acc_w = acc_ref[1, g]
        o_cmp_g = acc_ref[2, g]
        o_slc = acc_s[:dh, :] * (1.0 / acc_s[dh:dh + 1, :])
        o_win = acc_w[:dh, :] * (1.0 / acc_w[dh:dh + 1, :])
        for r in range(nrep):
            row = 3 * (g * nrep + r)
            heads.append(gt_t[row:row + 1, :] * head(o_cmp_g, r)[:dh, :]
                         + gt_t[row + 1:row + 2, :] * head(o_slc, r) + gt_t[row + 2:row + 3, :] * head(o_win, r))
    o_ref[...] = jnp.transpose(jnp.concatenate(heads, axis=0)).astype(o_ref.dtype)


def _nsa_attention(q, gt, kc, vc_t, ks, vs_t, kw, vw_t, kn, batch, seq):
    g = NSA_KV_GROUPS
    tiles_per_row = kn.shape[0] // batch
    nq = seq // ATT_TQ
    qw = (NSA_HEADS // g) * NSA_HEAD_DIM
    n_slc = seq // SLC_LEN
    n_cmp = (seq - CMP_LEN) // CMP_STRIDE + 1
    n_cmp_pad = kc.shape[2]
    c_start = np.arange(n_cmp)[:, None] * CMP_STRIDE
    s_start = np.arange(n_slc)[None, :] * SLC_LEN
    overlap = np.clip(np.minimum(c_start + CMP_LEN, s_start + SLC_LEN)
                      - np.maximum(c_start, s_start), 0, None) / CMP_LEN
    assert n_slc <= NSA_HEAD_DIM and n_slc % SUBLANES == 0 and seq % ATT_TK == 0
    c2s_t = np.zeros((NSA_HEAD_DIM, n_cmp_pad))
    c2s_t[:n_slc, :n_cmp] = overlap.T
    hd = NSA_HEADS * NSA_HEAD_DIM
    tok_spec = lambda width: pl.BlockSpec((ATT_TQ, width), lambda b, qi: (b * nq + qi, 0))
    key_spec = pl.BlockSpec((seq, g * LANES), lambda b, qi: (b, 0))
    val_spec = pl.BlockSpec((1, g, seq // LANES, V_ROWS, LANES), lambda b, qi: (b, 0, 0, 0, 0))
    return pl.pallas_call(
        functools.partial(_nsa_attn_kernel, n_slc=n_slc, tk=ATT_TK),
        grid=(batch, nq),
        in_specs=[pl.BlockSpec((hd, ATT_TQ), lambda b, qi: (0, b * nq + qi)),
                  pl.BlockSpec((LANES, ATT_TQ), lambda b, qi: (0, b * nq + qi)),
                  pl.BlockSpec((1, g, n_cmp_pad, LANES), lambda b, qi: (b, 0, 0, 0)),
                  pl.BlockSpec((1, g, V_ROWS, n_cmp_pad), lambda b, qi: (b, 0, 0, 0)),
                  key_spec, val_spec, key_spec, val_spec,
                  pl.BlockSpec((tiles_per_row,) + kn.shape[1:], lambda b, qi: (b, 0, 0, 0)),
                  pl.BlockSpec(c2s_t.shape, lambda b, qi: (0, 0))],
        out_specs=tok_spec(hd),
        out_shape=jax.ShapeDtypeStruct((batch * seq, hd), BF16),
        scratch_shapes=[pltpu.VMEM((n_slc, g * ATT_TQ), F32)] * 3
        + [pltpu.VMEM((3, g, V_ROWS, (NSA_HEADS // g) * ATT_TQ), F32)],
        compiler_params=_cparams(("parallel", "arbitrary")),
        name="nsa_attention",
    )(q, gt, kc, vc_t, ks, vs_t, kw, vw_t, kn, jnp.asarray(c2s_t, BF16))


def _block_diag(w):
    nb, bi, bo = w.shape
    eye = jnp.eye(nb, dtype=w.dtype)
    return (eye[:, None, :, None] * w[:, :, None, :]).reshape(nb * bi, nb * bo)


def _nsa_weight(w_in):
    return jnp.pad(w_in, ((0, 0), (0, LANES - 3 * NSA_HEADS))).astype(BF16)


def _ab_mixer(h, g_attn, w_in, conv_w, conv_b, ga_w, ga_b, gx_w, gx_b, lam, batch, seq):
    proj = _proj_ab(h, g_attn, w_in.astype(BF16))
    gate_w = jnp.concatenate([_block_diag(ga_w), _block_diag(gx_w)], axis=1).astype(BF16)
    gate_b = jnp.concatenate([ga_b, gx_b])[None, :]
    lru_out = _lru(proj, batch, seq, conv_w, conv_b[None, :], gate_w, gate_b, lam[None, :])
    return [lru_out, _retention(proj, batch, seq)]


def _nsa_mixer(h, g_attn, w_in, pe_k, k_w1, k_w2, pe_v, v_w1, v_w2, batch, seq):
    q, cmp_in, ks, vs_t, kw, vw_t, gt, kn = _proj_nsa(h, g_attn, _nsa_weight(w_in), seq)
    dh = NSA_HEAD_DIM
    n_blk = seq // CMP_STRIDE
    pe = jnp.broadcast_to(jnp.stack([pe_k, pe_v]).reshape(2, 1, CMP_LEN * dh), (2, SUBLANES, CMP_LEN * dh))
    w1 = jnp.stack([k_w1, v_w1]).astype(BF16)
    w1r = w1.reshape(2, CMP_LEN, dh, dh)
    first, second, zero = w1r[:, :CMP_STRIDE], w1r[:, CMP_STRIDE:], jnp.zeros_like(w1r[:, :CMP_STRIDE])
    wc = jnp.concatenate([jnp.concatenate([first, zero, second, zero], axis=-1),
                          jnp.concatenate([zero, first, zero, second], axis=-1)], axis=-2)
    w2k = k_w2.astype(BF16)
    z = jnp.zeros_like(w2k)
    w2k = jnp.concatenate([jnp.concatenate([w2k, z, z, z], axis=1), jnp.concatenate([z, z, w2k, z], axis=1)], axis=0)
    w2v_t = v_w2.T.astype(BF16)
    zt = jnp.zeros_like(w2v_t)
    w2v_t = jnp.concatenate([jnp.concatenate([w2v_t, zt], axis=1), jnp.concatenate([zt, w2v_t], axis=1)], axis=0)
    cos, sgn = _rot_tables(np.arange(n_blk) * CMP_STRIDE + CMP_LEN - 1, ROT_DIM, ROPE_THETA, 1.0)
    kc, vc_t = _compress(cmp_in, batch, seq, pe, w1, wc, w2k, w2v_t, cos, sgn)
    return [_nsa_attention(q, gt, kc, vc_t, ks, vs_t, kw, vw_t, kn, batch, seq)]


def kernel(x, attn_norm, ab_w_in, conv_w, conv_b, gate_a_w, gate_a_b, gate_x_w, gate_x_b, lru_lambda,
           ab_w_out, nsa_w_in, cmp_pe_k, cmp_k_w1, cmp_k_w2, cmp_pe_v, cmp_v_w1, cmp_v_w2, nsa_w_out,
           ffn_norm, ffn_w1, ffn_w3, ffn_w2, final_norm):
    batch, seq, d = x.shape
    depth = attn_norm.shape[0]
    h = x.reshape(batch * seq, d)
    for layer in range(depth):
        i = layer // 2
        g_attn = attn_norm[layer][None, :]
        if layer % 2 == 0:
            mixes = _ab_mixer(h, g_attn, ab_w_in[i], conv_w[i], conv_b[i], gate_a_w[i], gate_a_b[i],
                              gate_x_w[i], gate_x_b[i], lru_lambda[i], batch, seq)
            w_out = ab_w_out[i]
        else:
            mixes = _nsa_mixer(h, g_attn, nsa_w_in[i], cmp_pe_k[i], cmp_k_w1[i], cmp_k_w2[i],
                               cmp_pe_v[i], cmp_v_w1[i], cmp_v_w2[i], batch, seq)
            w_out = nsa_w_out[i]
        last = layer == depth - 1
        h = _mix_ffn(h, mixes, w_out.astype(BF16), ffn_norm[layer][None, :],
                     ffn_w1[layer].astype(BF16), ffn_w3[layer].astype(BF16), ffn_w2[layer].astype(BF16),
                     final_g=final_norm[None, :] if last else None)
    return h.reshape(batch, seq, d)
```

```python
import functools

import numpy as np
import jax
import jax.numpy as jnp
from jax import lax
from jax.experimental import pallas as pl
from jax.experimental.pallas import tpu as pltpu

F32 = jnp.float32
BF16 = jnp.bfloat16

EPS = 1e-6
NEG = -1e30
BIG = 1e30
LOG2E = float(np.log2(np.e))

LRU_BLOCKS = 8
CONV_WIDTH = 4
LRU_C = 8.0
RET_HEADS = 8
RET_DIM = 64
RET_THETA = 10000.0
NSA_HEADS = 16
NSA_KV_GROUPS = 4
NSA_HEAD_DIM = 64
CMP_LEN = 32
CMP_STRIDE = 16
SLC_LEN = 64
SLC_TOPK = 16
WINDOW = 512
ROPE_THETA = 500000.0
ROT_DIM = NSA_HEAD_DIM // 4

LANES = 128
SUBLANES = 8
VMEM_LIMIT = 56 * 1024 * 1024

RET_CHUNK = 256
RET_ROWS = 2
TOK_TILE = 512
PROJ_AB_TILE = 1024
LRU_TILE = 512
ATT_TQ = 256
ATT_TK = 512
V_ROWS = NSA_HEAD_DIM + 16
MAX_FIXED_SHIFT = 32.0
NORM_MARGIN = 1.01


def _cparams(sem):
    return pltpu.CompilerParams(dimension_semantics=sem, vmem_limit_bytes=VMEM_LIMIT)


def _dot(a, b):
    return jnp.dot(a, b, preferred_element_type=F32)


def _dot_nt(a, b):
    return lax.dot_general(a, b, (((1,), (1,)), ((), ())), preferred_element_type=F32)


def _rms(x, g):
    return x * lax.rsqrt(jnp.mean(x * x, axis=-1, keepdims=True) + EPS) * g


def _tile_lanes(t, reps):
    return t if reps == 1 else jnp.concatenate([t] * reps, axis=1)


def _rotate(x, cos, sgn, half):
    n = x.shape[1]
    lane = lax.broadcasted_iota(jnp.int32, x.shape, 1)
    first = (lane % NSA_HEAD_DIM) < half
    partner = jnp.where(first, pltpu.roll(x, n - half, axis=1), pltpu.roll(x, half, axis=1))
    return x * cos + partner * sgn


def _group_slabs(x, fill):
    lane = lax.broadcasted_iota(jnp.int32, (x.shape[0], LANES), 1)
    lo = lane < NSA_HEAD_DIM
    out = []
    for c in range(x.shape[1] // LANES):
        xc = x[:, c * LANES:(c + 1) * LANES]
        out.append(jnp.where(lo, xc, fill))
        out.append(jnp.where(lo, pltpu.roll(xc, NSA_HEAD_DIM, axis=1), fill))
    return jnp.concatenate(out, axis=1)


def _values_t(v_t, ncols):
    return jnp.concatenate([v_t, jnp.ones((V_ROWS - NSA_HEAD_DIM, ncols), F32)], axis=0).astype(BF16)


def _store_values_t(v, o_ref):
    rows = v.shape[0]
    v_t = jnp.transpose(v)
    for g in range(v.shape[1] // NSA_HEAD_DIM):
        slab = _values_t(v_t[g * NSA_HEAD_DIM:(g + 1) * NSA_HEAD_DIM, :], rows)
        for c in range(rows // LANES):
            o_ref[0, g, c] = slab[:, c * LANES:(c + 1) * LANES]


def _rot_tables(pos, rot_dim, theta, scale):
    half = rot_dim // 2
    inv = theta ** (-np.arange(0, rot_dim, 2, dtype=np.float64) / rot_dim)
    ang = np.asarray(pos, np.float64)[:, None] * inv
    cos = np.ones((len(pos), NSA_HEAD_DIM))
    sgn = np.zeros((len(pos), NSA_HEAD_DIM))
    cos[:, :half] = np.cos(ang)
    cos[:, half:rot_dim] = np.cos(ang)
    sgn[:, :half] = -np.sin(ang)
    sgn[:, half:rot_dim] = np.sin(ang)
    cos = np.tile(cos * scale, (1, 2))
    sgn = np.tile(sgn * scale, (1, 2))
    return jnp.asarray(cos, F32), jnp.asarray(sgn, F32)


def _ret_decay_tables(chunk):
    log_g = np.log1p(-(2.0 ** (-5.0 - np.arange(RET_HEADS, dtype=np.float64))))
    ci = np.arange(chunk, dtype=np.float64)
    diff = ci[:, None] - ci[None, :]
    inner = np.where(diff >= 0, np.exp(np.maximum(diff, 0.0) * log_g[:, None, None]), 0.0)
    lane_head = np.repeat(np.arange(RET_HEADS), RET_DIM)
    qdec = np.exp((ci[:, None] + 1.0) * log_g[lane_head][None, :])
    kdec = np.exp((chunk - 1.0 - ci[:, None]) * log_g[lane_head][None, :])
    cdec = np.exp(chunk * log_g[lane_head])[None, :]
    return (jnp.asarray(inner, F32), jnp.asarray(qdec, F32), jnp.asarray(kdec, F32),
            jnp.asarray(cdec, F32))


def _proj_ab_kernel(x_ref, g_ref, w_ref, o_ref):
    xn = _rms(x_ref[...], g_ref[...]).astype(BF16)
    o_ref[...] = _dot(xn, w_ref[...])


def _proj_ab(x2, g, w):
    t, d = x2.shape
    n = w.shape[1]
    return pl.pallas_call(
        _proj_ab_kernel,
        grid=(t // PROJ_AB_TILE,),
        in_specs=[pl.BlockSpec((PROJ_AB_TILE, d), lambda i: (i, 0)),
                  pl.BlockSpec((1, d), lambda i: (0, 0)),
                  pl.BlockSpec((d, n), lambda i: (0, 0), pipeline_mode=pl.Buffered(1))],
        out_specs=pl.BlockSpec((PROJ_AB_TILE, n), lambda i: (i, 0)),
        out_shape=jax.ShapeDtypeStruct((t, n), F32),
        compiler_params=_cparams(("parallel",)),
        name="proj_ab",
    )(x2, g, w)


def _lru_kernel(y_ref, x_ref, cw_ref, cb_ref, gw_ref, gb_ref, lam_ref, o_ref,
                xp_ref, a_ref, u_ref, h_ref, carry_ref):
    ts, w = x_ref.shape
    row = lax.broadcasted_iota(jnp.int32, (SUBLANES, w), 0)

    @pl.when(pl.program_id(1) == 0)
    def _():
        xp_ref[...] = jnp.zeros_like(xp_ref)
        carry_ref[...] = jnp.zeros_like(carry_ref)

    x = x_ref[...]
    prev = xp_ref[...]
    xc = cb_ref[...] + x * cw_ref[CONV_WIDTH - 1:CONV_WIDTH, :]
    for s in range(1, CONV_WIDTH):
        rolled = pltpu.roll(x, s, axis=0)
        head_rows = jnp.where(row < s, pltpu.roll(prev, s, axis=0), rolled[:SUBLANES, :])
        shifted = jnp.concatenate([head_rows, rolled[SUBLANES:, :]], axis=0)
        xc = xc + shifted * cw_ref[CONV_WIDTH - 1 - s:CONV_WIDTH - s, :]
    xp_ref[...] = x[ts - SUBLANES:, :]

    gates = _dot(xc.astype(BF16), gw_ref[...]) + gb_ref[...]
    r = jax.nn.sigmoid(gates[:, :w])
    i = jax.nn.sigmoid(gates[:, w:])
    log_a = -LRU_C * r * jax.nn.softplus(-lam_ref[...])
    a = jnp.exp(log_a)
    z = jnp.tanh(-log_a) * (a * a + 1.0)
    a_ref[...] = a
    u_ref[...] = jnp.where(z > 0.0, z * lax.rsqrt(z), 0.0) * (i * xc)

    def body(gi, carry):
        r0 = pl.multiple_of(gi * SUBLANES, SUBLANES)
        a = a_ref[pl.ds(r0, SUBLANES), :]
        u = u_ref[pl.ds(r0, SUBLANES), :]
        for d in (1, 2, 4):
            a_s = pltpu.roll(a, d, axis=0)
            u_s = pltpu.roll(u, d, axis=0)
            m = row >= d
            u = jnp.where(m, a * u_s + u, u)
            a = jnp.where(m, a * a_s, a)
        h = a * carry + u
        h_ref[pl.ds(r0, SUBLANES), :] = h
        return h[SUBLANES - 1:SUBLANES, :]

    carry_ref[...] = lax.fori_loop(0, ts // SUBLANES, body, carry_ref[...])
    o_ref[...] = (h_ref[...] * jax.nn.gelu(y_ref[...])).astype(o_ref.dtype)


def _lru(proj, batch, seq, conv_w, conv_b, gate_w, gate_b, lam):
    w = conv_w.shape[1]
    ns = seq // LRU_TILE
    return pl.pallas_call(
        _lru_kernel,
        grid=(batch, ns),
        in_specs=[pl.BlockSpec((LRU_TILE, w), lambda b, s: (b * ns + s, 0)),
                  pl.BlockSpec((LRU_TILE, w), lambda b, s: (b * ns + s, 1)),
                  pl.BlockSpec((CONV_WIDTH, w), lambda b, s: (0, 0)),
                  pl.BlockSpec((1, w), lambda b, s: (0, 0)),
                  pl.BlockSpec((w, 2 * w), lambda b, s: (0, 0)),
                  pl.BlockSpec((1, 2 * w), lambda b, s: (0, 0)),
                  pl.BlockSpec((1, w), lambda b, s: (0, 0))],
        out_specs=pl.BlockSpec((LRU_TILE, w), lambda b, s: (b * ns + s, 0)),
        out_shape=jax.ShapeDtypeStruct((batch * seq, w), BF16),
        scratch_shapes=[pltpu.VMEM((SUBLANES, w), F32),
                        pltpu.VMEM((LRU_TILE, w), F32),
                        pltpu.VMEM((LRU_TILE, w), F32),
                        pltpu.VMEM((LRU_TILE, w), F32),
                        pltpu.VMEM((1, w), F32)],
        compiler_params=_cparams(("parallel", "arbitrary")),
        name="rg_lru",
    )(proj, proj, conv_w, conv_b, gate_w, gate_b, lam)


def _split_bf16(x):
    hi = x.astype(BF16)
    lo = (x - hi.astype(F32)).astype(BF16)
    return hi, lo


def _ret_kernel(q_ref, k_ref, v_ref, g_ref, cq_ref, sq_ref, ck_ref, sk_ref,
                inner_ref, qdec_ref, kdec_ref, cdec_ref, avg_ref, o_ref, st_ref):
    nrows, c, w = q_ref.shape
    npair = w // LANES
    half = RET_DIM // 2
    lane = lax.broadcasted_iota(jnp.int32, (c, LANES), 1)
    lo = lane < RET_DIM
    rr = lax.broadcasted_iota(jnp.int32, (LANES, LANES), 0) < RET_DIM
    cc = lax.broadcasted_iota(jnp.int32, (LANES, LANES), 1) < RET_DIM
    same_head = rr == cc
    zero = jnp.zeros((c, LANES), BF16)
    avg = avg_ref[...]

    for row in range(nrows):
        b = pl.program_id(1) * nrows + row

        @pl.when(pl.program_id(0) == 0)
        def _():
            st_ref[b] = jnp.zeros(st_ref.shape[1:], F32)

        q = _rotate(q_ref[row], _tile_lanes(cq_ref[...], npair), _tile_lanes(sq_ref[...], npair), half)
        k = _rotate(k_ref[row], _tile_lanes(ck_ref[...], npair), _tile_lanes(sk_ref[...], npair), half)
        qd = (q * qdec_ref[...]).astype(BF16)
        kd = k * kdec_ref[...]
        q = q.astype(BF16)
        kb = k.astype(BF16)
        vb = v_ref[row].astype(BF16)

        outs = []
        for p in range(npair):
            sl = slice(p * LANES, (p + 1) * LANES)
            qp, kp, vp = q[:, sl], kb[:, sl], vb[:, sl]
            s_lo = _dot_nt(jnp.where(lo, qp, zero), kp) * inner_ref[2 * p]
            s_hi = _dot_nt(jnp.where(lo, zero, qp), kp) * inner_ref[2 * p + 1]
            o = jnp.where(lo, _dot(s_lo.astype(BF16), vp), _dot(s_hi.astype(BF16), vp))
            st = st_ref[b, p]
            o = o + _dot(qd[:, sl], st.astype(BF16))
            kv = _dot(jnp.transpose(kd[:, sl]).astype(BF16), vp)
            st_ref[b, p] = st * cdec_ref[:, sl] + jnp.where(same_head, kv, 0.0)
            outs.append(o)

        o_all = jnp.concatenate(outs, axis=0)
        mu = _dot(o_all.astype(BF16), avg)
        dlt = o_all - mu
        var = _dot((dlt * dlt).astype(BF16), avg)
        y = dlt * lax.rsqrt(var + EPS)
        y = jnp.concatenate([y[p * c:(p + 1) * c, :] for p in range(npair)], axis=1)
        o_ref[row] = (y * jax.nn.silu(g_ref[row])).astype(o_ref.dtype)


def _retention(proj, batch, seq):
    w = RET_HEADS * RET_DIM
    c = RET_CHUNK
    nc = seq // c
    npair = w // LANES
    pos = np.arange(seq)
    cq, sq = _rot_tables(pos, RET_DIM, RET_THETA, 1.0)
    ck, sk = _rot_tables(pos, RET_DIM, RET_THETA, RET_DIM ** -0.5)
    inner, qdec, kdec, cdec = _ret_decay_tables(c)
    blk = np.arange(LANES) // RET_DIM
    avg = jnp.asarray((blk[:, None] == blk[None, :]) / RET_DIM, BF16)
    col0 = 2
    rows = RET_ROWS if batch % RET_ROWS == 0 else 1
    proj3 = proj.reshape(batch, seq, proj.shape[1])
    tok = lambda j: pl.BlockSpec((rows, c, w), lambda ci, b, j=j: (b, ci, j))
    tab = pl.BlockSpec((c, LANES), lambda ci, b: (ci, 0))
    const = lambda shape: pl.BlockSpec(shape, lambda ci, b: (0,) * len(shape))
    out = pl.pallas_call(
        _ret_kernel,
        grid=(nc, batch // rows),
        in_specs=[tok(col0), tok(col0 + 1), tok(col0 + 2), tok(col0 + 3),
                  tab, tab, tab, tab,
                  const((RET_HEADS, c, c)), const((c, w)), const((c, w)), const((1, w)),
                  const((LANES, LANES))],
        out_specs=pl.BlockSpec((rows, c, w), lambda ci, b: (b, ci, 0)),
        out_shape=jax.ShapeDtypeStruct((batch, seq, w), BF16),
        scratch_shapes=[pltpu.VMEM((batch, npair, LANES, LANES), F32)],
        compiler_params=_cparams(("arbitrary", "arbitrary")),
        name="retention",
    )(proj3, proj3, proj3, proj3, cq, sq, ck, sk, inner, qdec, kdec, cdec, avg)
    return out.reshape(batch * seq, w)


def _mix_ffn_kernel(*refs, n_mix, final):
    h_ref = refs[0]
    mix_refs = refs[1:1 + n_mix]
    wo_ref, g_ref, w1_ref, w3_ref, w2_ref = refs[1 + n_mix:6 + n_mix]
    rest = refs[6 + n_mix:]
    fg_ref = rest[0] if final else None
    o_ref = rest[-1]

    mixed = mix_refs[0][...] if n_mix == 1 else jnp.concatenate([r[...] for r in mix_refs], axis=1)
    h1 = h_ref[...] + _dot(mixed, wo_ref[...])
    xn = _rms(h1, g_ref[...]).astype(BF16)
    act = (jax.nn.silu(_dot(xn, w1_ref[...])) * _dot(xn, w3_ref[...])).astype(BF16)
    h2 = h1 + _dot(act, w2_ref[...])
    if final:
        h2 = _rms(h2, fg_ref[...])
    o_ref[...] = h2


def _mix_ffn(h2d, mixes, wo, g, w1, w3, w2, final_g=None):
    t, d = h2d.shape
    hid = w1.shape[1]
    final = final_g is not None
    row = lambda width: pl.BlockSpec((TOK_TILE, width), lambda i: (i, 0))
    const = lambda shape: pl.BlockSpec(shape, lambda i: (0, 0), pipeline_mode=pl.Buffered(1))
    in_specs = [row(d)] + [row(m.shape[1]) for m in mixes] + [
        const(wo.shape), const((1, d)), const((d, hid)), const((d, hid)), const((hid, d))]
    args = [h2d, *mixes, wo, g, w1, w3, w2]
    if final:
        in_specs.append(const((1, d)))
        args.append(final_g)
    return pl.pallas_call(
        functools.partial(_mix_ffn_kernel, n_mix=len(mixes), final=final),
        grid=(t // TOK_TILE,),
        in_specs=in_specs,
        out_specs=row(d),
        out_shape=jax.ShapeDtypeStruct((t, d), F32),
        compiler_params=_cparams(("parallel",)),
        name="mix_ffn_final" if final else "mix_ffn",
    )(*args)


def _proj_nsa_kernel(x_ref, g_ref, w_ref, cq_ref, sq_ref, ck_ref, sk_ref,
                     q_ref, cmp_ref, ks_ref, vs_ref, kw_ref, vw_ref, gt_ref, kn_ref, *, seq_tiles):
    hd = q_ref.shape[0]
    kvw = NSA_KV_GROUPS * NSA_HEAD_DIM
    xn = _rms(x_ref[...], g_ref[...]).astype(BF16)
    proj = _dot(xn, w_ref[...])
    half = ROT_DIM // 2
    q = proj[:, :hd]
    q_ref[...] = jnp.transpose(_rotate(q, _tile_lanes(cq_ref[...], hd // LANES),
                                       _tile_lanes(sq_ref[...], hd // LANES), half))
    kv = proj[:, hd:hd + 6 * kvw]
    for c in range(cmp_ref.shape[0]):
        cmp_ref[c] = kv[:, c * LANES:(c + 1) * LANES]
    ck = _tile_lanes(ck_ref[...], kvw // LANES)
    sk = _tile_lanes(sk_ref[...], kvw // LANES)
    rows = x_ref.shape[0]
    tpos = (pl.program_id(0) % seq_tiles) * rows + lax.broadcasted_iota(jnp.int32, (rows, LANES), 0)
    lane = lax.broadcasted_iota(jnp.int32, (rows, LANES), 1)
    block_onehot = jnp.where(lane - NSA_HEAD_DIM == tpos // SLC_LEN, 1.0, 0.0)
    ks = _group_slabs(_rotate(kv[:, 2 * kvw:3 * kvw], ck, sk, half), block_onehot).astype(ks_ref.dtype)
    one_lane = jnp.where(lane == NSA_HEAD_DIM, 1.0, 0.0)
    kw = _group_slabs(_rotate(kv[:, 4 * kvw:5 * kvw], ck, sk, half), one_lane).astype(kw_ref.dtype)
    ks_ref[...] = ks
    kw_ref[...] = kw
    kn_ref[0, 0] = _max_key_norm2(ks)
    kn_ref[0, 1] = _max_key_norm2(kw)
    _store_values_t(kv[:, 3 * kvw:4 * kvw], vs_ref)
    _store_values_t(kv[:, 5 * kvw:6 * kvw], vw_ref)
    gt_ref[...] = jnp.transpose(jax.nn.sigmoid(proj[:, hd + 6 * kvw:]))


def _proj_nsa(x2, g, w, seq):
    t, d = x2.shape
    hd = NSA_HEADS * NSA_HEAD_DIM
    kvw = NSA_KV_GROUPS * NSA_HEAD_DIM
    gw = NSA_KV_GROUPS * LANES
    n = w.shape[1]
    ns = seq // TOK_TILE
    pos = np.arange(seq)
    cq, sq = _rot_tables(pos, ROT_DIM, ROPE_THETA, NSA_HEAD_DIM ** -0.5 * LOG2E)
    ck, sk = _rot_tables(pos, ROT_DIM, ROPE_THETA, 1.0)
    row = lambda width: pl.BlockSpec((TOK_TILE, width), lambda i: (i, 0))
    tab = pl.BlockSpec((TOK_TILE, LANES), lambda i: (i % ns, 0))
    chunks = TOK_TILE // LANES
    val_t = (pl.BlockSpec((1, NSA_KV_GROUPS, chunks, V_ROWS, LANES), lambda i: (i // ns, 0, i % ns, 0, 0)),
             jax.ShapeDtypeStruct((t // seq, NSA_KV_GROUPS, seq // LANES, V_ROWS, LANES), BF16))
    tok = lambda wd, dt: (row(wd), jax.ShapeDtypeStruct((t, wd), dt))
    cmp_planes = (pl.BlockSpec((2 * kvw // LANES, TOK_TILE, LANES), lambda i: (0, i, 0)),
                  jax.ShapeDtypeStruct((2 * kvw // LANES, t, LANES), F32))
    tok_t = lambda wd: (pl.BlockSpec((wd, TOK_TILE), lambda i: (0, i)), jax.ShapeDtypeStruct((wd, t), F32))
    key_norms = (pl.BlockSpec((1, 2, SUBLANES, LANES), lambda i: (i, 0, 0, 0)),
                 jax.ShapeDtypeStruct((t // TOK_TILE, 2, SUBLANES, LANES), F32))
    outs = [tok_t(hd), cmp_planes, tok(gw, BF16), val_t, tok(gw, BF16), val_t, tok_t(LANES), key_norms]
    return pl.pallas_call(
        functools.partial(_proj_nsa_kernel, seq_tiles=ns),
        grid=(t // TOK_TILE,),
        in_specs=[row(d), pl.BlockSpec((1, d), lambda i: (0, 0)),
                  pl.BlockSpec((d, n), lambda i: (0, 0), pipeline_mode=pl.Buffered(1)),
                  tab, tab, tab, tab],
        out_specs=[spec for spec, _ in outs],
        out_shape=[shape for _, shape in outs],
        compiler_params=_cparams(("parallel",)),
        name="proj_nsa",
    )(x2, g, w, cq, sq, ck, sk)


def _compress_kernel(x_ref, pe_ref, w1_ref, wc_ref, w2k_ref, w2vt_ref, cos_ref, sgn_ref, kc_ref, vct_ref):
    n = kc_ref.shape[2]
    dh = NSA_HEAD_DIM
    for kv in range(2):
        pe_term = _dot(pe_ref[kv].astype(BF16), w1_ref[kv])[0:1, :]
        pe_term = jnp.concatenate([pe_term, pe_term], axis=1)
        for cp in range(NSA_KV_GROUPS // 2):
            col = kv * (NSA_KV_GROUPS // 2) + cp
            acc = jnp.zeros((n, 2 * LANES), F32)
            for j in range(CMP_STRIDE):
                rows = x_ref[col, pl.ds(j, n, stride=CMP_STRIDE), :]
                acc = acc + _dot(rows.astype(BF16), wc_ref[kv, j])
            hid = jax.nn.gelu(acc[:, :LANES] + pltpu.roll(acc[:, LANES:], n - 1, axis=0) + pe_term).astype(BF16)
            if kv == 0:
                kc = _dot(hid, w2k_ref[...])
                kc = _rotate(kc, _tile_lanes(cos_ref[...], 2), _tile_lanes(sgn_ref[...], 2), ROT_DIM // 2)
                kc_ref[0, 2 * cp] = kc[:, :LANES].astype(kc_ref.dtype)
                kc_ref[0, 2 * cp + 1] = kc[:, LANES:].astype(kc_ref.dtype)
            else:
                vc_t = _dot_nt(w2vt_ref[...], hid)
                vct_ref[0, 2 * cp] = _values_t(vc_t[:dh, :], n)
                vct_ref[0, 2 * cp + 1] = _values_t(vc_t[dh:, :], n)


def _compress(cmp_in, batch, seq, pe, w1, wc, w2k, w2v_t, cos, sgn):
    g = NSA_KV_GROUPS
    n = seq // CMP_STRIDE
    full = lambda a: pl.BlockSpec(a.shape, lambda bi: (0,) * a.ndim)
    return pl.pallas_call(
        _compress_kernel,
        grid=(batch,),
        in_specs=[pl.BlockSpec((cmp_in.shape[0], seq, LANES), lambda bi: (0, bi, 0)),
                  full(pe), full(w1), full(wc), full(w2k), full(w2v_t), full(cos), full(sgn)],
        out_specs=[pl.BlockSpec((1, g, n, LANES), lambda bi: (bi, 0, 0, 0)),
                   pl.BlockSpec((1, g, V_ROWS, n), lambda bi: (bi, 0, 0, 0))],
        out_shape=[jax.ShapeDtypeStruct((batch, g, n, LANES), BF16),
                   jax.ShapeDtypeStruct((batch, g, V_ROWS, n), BF16)],
        compiler_params=_cparams(("parallel",)),
        name="compress",
    )(cmp_in, pe, w1, wc, w2k, w2v_t, cos, sgn)


def _topk_bias_t(imp_t, tpos, last_block, val_ref, cnt_ref):
    nb, tq = imp_t.shape
    j = lax.broadcasted_iota(jnp.int32, (nb, tq), 0)
    cur = tpos // SLC_LEN
    forced = (j == 0) | (j == cur) | (j == cur - 1)
    causal = j <= cur
    val_ref[...] = jnp.where(forced, BIG, jnp.where(causal, imp_t, NEG))
    cnt_ref[...] = jnp.zeros((nb, tq), F32)
    nchunk = nb // SUBLANES
    rows = lax.broadcasted_iota(jnp.int32, (SUBLANES, tq), 0)
    for c in range(nchunk):
        for ic in range(nchunk):
            @pl.when(max(c, ic) * SUBLANES <= last_block)
            def _():
                vc = val_ref[c * SUBLANES:(c + 1) * SUBLANES, :]
                vic = val_ref[ic * SUBLANES:(ic + 1) * SUBLANES, :]
                acc = cnt_ref[c * SUBLANES:(c + 1) * SUBLANES, :]
                for r in range(SUBLANES):
                    if c == ic and r == 0:
                        continue
                    vi = vic if r == 0 else pltpu.roll(vic, SUBLANES - r, axis=0)
                    if c > ic:
                        ahead = jnp.where(vi >= vc, 1.0, 0.0)
                    elif c < ic:
                        ahead = jnp.where(vi > vc, 1.0, 0.0)
                    else:
                        ahead = jnp.where(rows >= SUBLANES - r, jnp.where(vi >= vc, 1.0, 0.0),
                                          jnp.where(vi > vc, 1.0, 0.0))
                    acc = acc + ahead
                cnt_ref[c * SUBLANES:(c + 1) * SUBLANES, :] = acc
    return jnp.where((cnt_ref[...] < float(SLC_TOPK)) & causal, 0.0, NEG)


def _value_chunks_t(vt_ref, g, first, count):
    return jnp.concatenate([vt_ref[0, g, first + i] for i in range(count)], axis=1)


def _max_key_norm2(k_slabs):
    width = k_slabs.shape[1]
    row = lax.broadcasted_iota(jnp.int32, (width, LANES), 0)
    lane = lax.broadcasted_iota(jnp.int32, (width, LANES), 1)
    dims_of_group = jnp.where((row // LANES == lane) & (row % LANES < NSA_HEAD_DIM), 1.0, 0.0).astype(BF16)
    k = k_slabs.astype(F32)
    norm2 = _dot((k * k).astype(BF16), dims_of_group)
    return jnp.broadcast_to(jnp.max(norm2, axis=0, keepdims=True), (SUBLANES, LANES))


def _nsa_attn_kernel(q_ref, gt_ref, kc_ref, vct_ref, ks_ref, vst_ref, kw_ref, vwt_ref, kn_ref, c2s_ref, o_ref,
                     val_ref, cnt_ref, imp_ref, acc_ref, *, n_slc, tk):
    tq = q_ref.shape[1]
    ncmp = kc_ref.shape[2]
    ngrp = NSA_KV_GROUPS
    nrep = NSA_HEADS // ngrp
    dh = NSA_HEAD_DIM
    groups = range(ngrp)
    t0 = pl.program_id(1) * tq
    head = lambda x, r: x[:, r * tq:(r + 1) * tq]
    grp_lanes = lambda g: slice(g * LANES, (g + 1) * LANES)

    kn = jnp.max(kn_ref[...], axis=0)
    kmax = [jnp.sqrt(kn[j, 0:1, :]) * NORM_MARGIN for j in range(2)]

    tpos = t0 + lax.broadcasted_iota(jnp.int32, (1, tq), 1)
    q_t = []
    for g in groups:
        cols = [q_ref[(g * nrep + r) * dh:(g * nrep + r + 1) * dh, :] for r in range(nrep)]
        q_t.append(jnp.concatenate([jnp.concatenate(cols, axis=1), jnp.zeros((LANES - dh, nrep * tq), F32)], axis=0))

    q_norm = [jnp.sqrt(jnp.sum(q_t[g] * q_t[g], axis=0, keepdims=True)) for g in groups]
    shift_s = [q_norm[g] * kmax[0][:, g:g + 1] for g in groups]
    shift_w = [q_norm[g] * kmax[1][:, g:g + 1] for g in groups]
    worst = functools.reduce(jnp.maximum, shift_s + shift_w)
    fixed_shift_ok = jnp.max(worst) < MAX_FIXED_SHIFT

    last = (t0 + tq - 1) // tk
    k_last = last * tk + lax.broadcasted_iota(jnp.int32, (tk, tq), 0)
    causal_bias = jnp.where(k_last <= tpos, 0.0, NEG)
    span = WINDOW + tq
    start = pl.multiple_of(jnp.maximum(t0 - WINDOW, 0), tq)
    dist = tpos - (start + lax.broadcasted_iota(jnp.int32, (span, tq), 0))
    band_bias = jnp.where((dist >= 0) & (dist < WINDOW), 0.0, NEG)
    key_tile = lambda c, g: ks_ref[pl.ds(pl.multiple_of(c * tk, tk), tk), grp_lanes(g)]
    val_tile = lambda c, g: _value_chunks_t(vst_ref, g, c * (tk // LANES), tk // LANES)
    win_keys = lambda g: kw_ref[pl.ds(start, span), grp_lanes(g)]
    win_vals = lambda g: _value_chunks_t(vwt_ref, g, start // LANES, span // LANES)
    all_heads = lambda x: jnp.concatenate([x] * nrep, axis=1)

    cmp_end = lax.broadcasted_iota(jnp.int32, (ncmp, tq), 0) * CMP_STRIDE + (CMP_LEN - 1)
    vis_bias = jnp.where(cmp_end <= tpos, 0.0, NEG)
    any_vis = jnp.where(tpos >= CMP_LEN - 1, 1.0, 0.0)

    def compressed_branch(g):
        c2s = c2s_ref[...]
        s = _dot(kc_ref[0, g], q_t[g].astype(BF16))
        p_sum = jnp.zeros((ncmp, tq), F32)
        p_cmp = []
        for r in range(nrep):
            sr = head(s, r) + vis_bias
            e = jnp.exp2(sr - jnp.max(sr, axis=0, keepdims=True))
            pr = e * (any_vis / jnp.sum(e, axis=0, keepdims=True))
            p_sum = p_sum + pr
            p_cmp.append(pr.astype(BF16))
        acc_ref[2, g] = _dot(vct_ref[0, g], jnp.concatenate(p_cmp, axis=1))
        p_hi, p_lo = _split_bf16(p_sum)
        imp_ref[:, g * tq:(g + 1) * tq] = (_dot(c2s, p_hi) + _dot(c2s, p_lo))[:n_slc, :]

    @pl.when(fixed_shift_ok)
    def _():
        zeros_hi = jnp.zeros((dh - 1, nrep * tq), F32)
        for g in groups:
            qt_win = jnp.concatenate([q_t[g][:dh, :], -shift_w[g], zeros_hi], axis=0).astype(BF16)
            p = jnp.exp2(_dot(win_keys(g), qt_win) + all_heads(band_bias)).astype(BF16)
            acc_ref[1, g] = _dot(win_vals(g), p)
            compressed_branch(g)

    @pl.when(jnp.logical_not(fixed_shift_ok))
    def _():
        for g in groups:
            compressed_branch(g)
            sw = _dot(win_keys(g), q_t[g].astype(BF16))
            p_win = []
            for r in range(nrep):
                sr = head(sw, r) + band_bias
                p_win.append(jnp.exp2(sr - jnp.max(sr, axis=0, keepdims=True)).astype(BF16))
            acc_ref[1, g] = _dot(win_vals(g), jnp.concatenate(p_win, axis=1))

    bias_t = _topk_bias_t(imp_ref[...], jnp.concatenate([tpos] * ngrp, axis=1),
                          (t0 + tq - 1) // SLC_LEN, val_ref, cnt_ref)
    if n_slc < dh:
        bias_t = jnp.concatenate([bias_t, jnp.full((dh - n_slc, ngrp * tq), NEG, F32)], axis=0)
    bias_rows = [jnp.concatenate([head(bias_t, g)] * nrep, axis=1) for g in groups]

    @pl.when(fixed_shift_ok)
    def _():
        qt_sel = [jnp.concatenate([q_t[g][:dh, :], bias_rows[g] - shift_s[g]], axis=0).astype(BF16) for g in groups]

        def partial_pv(c, g, token_bias=None, nkeys=tk):
            k0 = pl.multiple_of(c * tk, tk)
            sc = _dot(ks_ref[pl.ds(k0, nkeys), grp_lanes(g)], qt_sel[g])
            if token_bias is not None:
                sc = sc + all_heads(token_bias[:nkeys, :])
            vals = _value_chunks_t(vst_ref, g, c * (tk // LANES), nkeys // LANES)
            return _dot(vals, jnp.exp2(sc).astype(BF16))

        def body(i, accs):
            return tuple(accs[g] + partial_pv(2 * i, g) + partial_pv(2 * i + 1, g) for g in groups)

        accs = lax.fori_loop(0, last // 2, body, tuple(jnp.zeros((V_ROWS, nrep * tq), F32) for _ in groups))
        for g in groups:
            acc_ref[0, g] = accs[g]

        @pl.when(last % 2 == 1)
        def _():
            for g in groups:
                acc_ref[0, g] += partial_pv(last - 1, g)

        needs_half = (t0 % tk) + tq <= tk // 2

        @pl.when(needs_half)
        def _():
            for g in groups:
                acc_ref[0, g] += partial_pv(last, g, causal_bias, tk // 2)

        @pl.when(jnp.logical_not(needs_half))
        def _():
            for g in groups:
                acc_ref[0, g] += partial_pv(last, g, causal_bias)

    @pl.when(jnp.logical_not(fixed_shift_ok))
    def _():
        qt_sel = [jnp.concatenate([q_t[g][:dh, :], bias_rows[g]], axis=0).astype(BF16) for g in groups]

        def tile(c, carry, token_bias):
            out = []
            for g in groups:
                m, acc = carry[g]
                sc = _dot(key_tile(c, g), qt_sel[g])
                ps, ms, alphas = [], [], []
                for r in range(nrep):
                    sr = head(sc, r)
                    if token_bias is not None:
                        sr = sr + token_bias
                    m_new = jnp.maximum(head(m, r), jnp.max(sr, axis=0, keepdims=True))
                    alphas.append(jnp.exp2(head(m, r) - m_new))
                    ms.append(m_new)
                    ps.append(jnp.exp2(sr - m_new).astype(BF16))
                pv = _dot(val_tile(c, g), jnp.concatenate(ps, axis=1))
                out.append((jnp.concatenate(ms, axis=1), jnp.concatenate(alphas, axis=1) * acc + pv))
            return tuple(out)

        init = tuple((jnp.full((1, nrep * tq), NEG, F32), jnp.zeros((V_ROWS, nrep * tq), F32)) for _ in groups)
        carry = lax.fori_loop(0, last, lambda c, cr: tile(c, cr, None), init)
        carry = tile(last, carry, causal_bias)
        for g in groups:
            acc_ref[0, g] = carry[g][1]

    gt_t = gt_ref[...]
    heads = []
    for g in groups:
        acc_s = acc_ref[0, g]
        acc_w = acc_ref[1, g]
        o_cmp_g = acc_ref[2, g]
        o_slc = acc_s[:dh, :] * (1.0 / acc_s[dh:dh + 1, :])
        o_win = acc_w[:dh, :] * (1.0 / acc_w[dh:dh + 1, :])
        for r in range(nrep):
            row = 3 * (g * nrep + r)
            heads.append(gt_t[row:row + 1, :] * head(o_cmp_g, r)[:dh, :]
                         + gt_t[row + 1:row + 2, :] * head(o_slc, r) + gt_t[row + 2:row + 3, :] * head(o_win, r))
    o_ref[...] = jnp.transpose(jnp.concatenate(heads, axis=0)).astype(o_ref.dtype)


def _nsa_attention(q, gt, kc, vc_t, ks, vs_t, kw, vw_t, kn, batch, seq):
    g = NSA_KV_GROUPS
    tiles_per_row = kn.shape[0] // batch
    nq = seq // ATT_TQ
    qw = (NSA_HEADS // g) * NSA_HEAD_DIM
    n_slc = seq // SLC_LEN
    n_cmp = (seq - CMP_LEN) // CMP_STRIDE + 1
    n_cmp_pad = kc.shape[2]
    c_start = np.arange(n_cmp)[:, None] * CMP_STRIDE
    s_start = np.arange(n_slc)[None, :] * SLC_LEN
    overlap = np.clip(np.minimum(c_start + CMP_LEN, s_start + SLC_LEN)
                      - np.maximum(c_start, s_start), 0, None) / CMP_LEN
    assert n_slc <= NSA_HEAD_DIM and n_slc % SUBLANES == 0 and seq % ATT_TK == 0
    c2s_t = np.zeros((NSA_HEAD_DIM, n_cmp_pad))
    c2s_t[:n_slc, :n_cmp] = overlap.T
    hd = NSA_HEADS * NSA_HEAD_DIM
    tok_spec = lambda width: pl.BlockSpec((ATT_TQ, width), lambda b, qi: (b * nq + qi, 0))
    key_spec = pl.BlockSpec((seq, g * LANES), lambda b, qi: (b, 0))
    val_spec = pl.BlockSpec((1, g, seq // LANES, V_ROWS, LANES), lambda b, qi: (b, 0, 0, 0, 0))
    return pl.pallas_call(
        functools.partial(_nsa_attn_kernel, n_slc=n_slc, tk=ATT_TK),
        grid=(batch, nq),
        in_specs=[pl.BlockSpec((hd, ATT_TQ), lambda b, qi: (0, b * nq + qi)),
                  pl.BlockSpec((LANES, ATT_TQ), lambda b, qi: (0, b * nq + qi)),
                  pl.BlockSpec((1, g, n_cmp_pad, LANES), lambda b, qi: (b, 0, 0, 0)),
                  pl.BlockSpec((1, g, V_ROWS, n_cmp_pad), lambda b, qi: (b, 0, 0, 0)),
                  key_spec, val_spec, key_spec, val_spec,
                  pl.BlockSpec((tiles_per_row,) + kn.shape[1:], lambda b, qi: (b, 0, 0, 0)),
                  pl.BlockSpec(c2s_t.shape, lambda b, qi: (0, 0))],
        out_specs=tok_spec(hd),
        out_shape=jax.ShapeDtypeStruct((batch * seq, hd), BF16),
        scratch_shapes=[pltpu.VMEM((n_slc, g * ATT_TQ), F32)] * 3
        + [pltpu.VMEM((3, g, V_ROWS, (NSA_HEADS // g) * ATT_TQ), F32)],
        compiler_params=_cparams(("parallel", "arbitrary")),
        name="nsa_attention",
    )(q, gt, kc, vc_t, ks, vs_t, kw, vw_t, kn, jnp.asarray(c2s_t, BF16))


def _block_diag(w):
    nb, bi, bo = w.shape
    eye = jnp.eye(nb, dtype=w.dtype)
    return (eye[:, None, :, None] * w[:, :, None, :]).reshape(nb * bi, nb * bo)


def _nsa_weight(w_in):
    return jnp.pad(w_in, ((0, 0), (0, LANES - 3 * NSA_HEADS))).astype(BF16)


def _ab_mixer(h, g_attn, w_in, conv_w, conv_b, ga_w, ga_b, gx_w, gx_b, lam, batch, seq):
    proj = _proj_ab(h, g_attn, w_in.astype(BF16))
    gate_w = jnp.concatenate([_block_diag(ga_w), _block_diag(gx_w)], axis=1).astype(BF16)
    gate_b = jnp.concatenate([ga_b, gx_b])[None, :]
    lru_out = _lru(proj, batch, seq, conv_w, conv_b[None, :], gate_w, gate_b, lam[None, :])
    return [lru_out, _retention(proj, batch, seq)]


def _nsa_mixer(h, g_attn, w_in, pe_k, k_w1, k_w2, pe_v, v_w1, v_w2, batch, seq):
    q, cmp_in, ks, vs_t, kw, vw_t, gt, kn = _proj_nsa(h, g_attn, _nsa_weight(w_in), seq)
    dh = NSA_HEAD_DIM
    n_blk = seq // CMP_STRIDE
    pe = jnp.broadcast_to(jnp.stack([pe_k, pe_v]).reshape(2, 1, CMP_LEN * dh), (2, SUBLANES, CMP_LEN * dh))
    w1 = jnp.stack([k_w1, v_w1]).astype(BF16)
    w1r = w1.reshape(2, CMP_LEN, dh, dh)
    first, second, zero = w1r[:, :CMP_STRIDE], w1r[:, CMP_STRIDE:], jnp.zeros_like(w1r[:, :CMP_STRIDE])
    wc = jnp.concatenate([jnp.concatenate([first, zero, second, zero], axis=-1),
                          jnp.concatenate([zero, first, zero, second], axis=-1)], axis=-2)
    w2k = k_w2.astype(BF16)
    z = jnp.zeros_like(w2k)
    w2k = jnp.concatenate([jnp.concatenate([w2k, z, z, z], axis=1), jnp.concatenate([z, z, w2k, z], axis=1)], axis=0)
    w2v_t = v_w2.T.astype(BF16)
    zt = jnp.zeros_like(w2v_t)
    w2v_t = jnp.concatenate([jnp.concatenate([w2v_t, zt], axis=1), jnp.concatenate([zt, w2v_t], axis=1)], axis=0)
    cos, sgn = _rot_tables(np.arange(n_blk) * CMP_STRIDE + CMP_LEN - 1, ROT_DIM, ROPE_THETA, 1.0)
    kc, vc_t = _compress(cmp_in, batch, seq, pe, w1, wc, w2k, w2v_t, cos, sgn)
    return [_nsa_attention(q, gt, kc, vc_t, ks, vs_t, kw, vw_t, kn, batch, seq)]


def kernel(x, attn_norm, ab_w_in, conv_w, conv_b, gate_a_w, gate_a_b, gate_x_w, gate_x_b, lru_lambda,
           ab_w_out, nsa_w_in, cmp_pe_k, cmp_k_w1, cmp_k_w2, cmp_pe_v, cmp_v_w1, cmp_v_w2, nsa_w_out,
           ffn_norm, ffn_w1, ffn_w3, ffn_w2, final_norm):
    batch, seq, d = x.shape
    depth = attn_norm.shape[0]
    h = x.reshape(batch * seq, d)
    for layer in range(depth):
        i = layer // 2
        g_attn = attn_norm[layer][None, :]
        if layer % 2 == 0:
            mixes = _ab_mixer(h, g_attn, ab_w_in[i], conv_w[i], conv_b[i], gate_a_w[i], gate_a_b[i],
                              gate_x_w[i], gate_x_b[i], lru_lambda[i], batch, seq)
            w_out = ab_w_out[i]
        else:
            mixes = _nsa_mixer(h, g_attn, nsa_w_in[i], cmp_pe_k[i], cmp_k_w1[i], cmp_k_w2[i],
                               cmp_pe_v[i], cmp_v_w1[i], cmp_v_w2[i], batch, seq)
            w_out = nsa_w_out[i]
        last = layer == depth - 1
        h = _mix_ffn(h, mixes, w_out.astype(BF16), ffn_norm[layer][None, :],
                     ffn_w1[layer].astype(BF16), ffn_w3[layer].astype(BF16), ffn_w2[layer].astype(BF16),
                     final_g=final_norm[None, :] if last else None)
    return h.reshape(batch, seq, d)
```
